```python
import math
import numpy as np
import jax
import jax.numpy as jnp
from jax import lax

D_MODEL = 1024
BATCH = 4
SEQ = 4096
DEPTH = 2

GRID_W = 64
CTX_LEN = 256
N_EVEN = (DEPTH + 1) // 2
N_ODD = DEPTH // 2
N_MOD = 6
NORM_EPS = 1e-6
ROPE_BASE = 10000.0

MLA_HEADS = 8
MLA_NOPE = 64
MLA_ROPE = 32
MLA_V = 64
Q_LORA = 384
KV_LORA = 256
MLA_Q_BLOCK = 128
MLA_SCALE = (MLA_NOPE + MLA_ROPE) ** -0.5

NA_HEADS = 8
NA_HEAD_DIM = 64
NA_WIN_H = 8
NA_WIN_W = 16
NA_SCALE = NA_HEAD_DIM ** -0.5

EVEN_IN = Q_LORA + KV_LORA + MLA_ROPE + 3 * NA_HEADS * NA_HEAD_DIM
EVEN_OUT = MLA_HEADS * MLA_V + NA_HEADS * NA_HEAD_DIM

S5_WIDTH = D_MODEL
S5_GROUP = 16
S5_GROUPS = S5_WIDTH // S5_GROUP
S5_STATE = 64

D_FF = 2816
N_EXPERTS = 8
TOP_K = 2
D_FF_EXPERT = 3584

kernel_name = 'hybrid_mla_natten_s5_moe_dit'


def rms_norm(x, g):
    xf = x.astype(jnp.float32)
    y = xf * lax.rsqrt(jnp.mean(xf * xf, axis=-1, keepdims=True) + NORM_EPS)
    return (y * g.astype(jnp.float32)).astype(x.dtype)


def modulate(x, g, shift, scale):
    return rms_norm(x, g) * (1 + scale) + shift


def adaln_params(cond, w, b):
    m = (jax.nn.silu(cond) @ w + b)[..., None, :]
    return jnp.split(m, N_MOD, axis=-1)


def softmax_f32(s):
    return jax.nn.softmax(s.astype(jnp.float32), axis=-1)


def dense_attend(s, v):
    p = softmax_f32(s).astype(v.dtype)
    return jnp.einsum('bhqk,bkhd->bqhd', p, v)


def axial_rope(x, row, col):
    half = x.shape[-1] // 2
    quarter = half // 2
    inv_freq = ROPE_BASE ** (-jnp.arange(quarter, dtype=jnp.float32) / quarter)
    bshape = (1, x.shape[1]) + (1,) * (x.ndim - 3) + (quarter,)

    def rotate(xa, pos):
        ang = pos.astype(jnp.float32)[:, None] * inv_freq[None, :]
        cos = jnp.cos(ang).reshape(bshape).astype(x.dtype)
        sin = jnp.sin(ang).reshape(bshape).astype(x.dtype)
        x1, x2 = xa[..., :quarter], xa[..., quarter:]
        return jnp.concatenate([x1 * cos - x2 * sin, x1 * sin + x2 * cos], axis=-1)

    return jnp.concatenate([rotate(x[..., :half], row), rotate(x[..., half:], col)], axis=-1)


def even_project(t, w_in, q_norm_g, w_qb, kv_norm_g, w_kvb):
    b, l, _ = t.shape
    p = t @ w_in
    cq, ckv, kr, qkv = jnp.split(p, [Q_LORA, Q_LORA + KV_LORA, Q_LORA + KV_LORA + MLA_ROPE], axis=-1)
    q = (rms_norm(cq, q_norm_g) @ w_qb).reshape(b, l, MLA_HEADS, MLA_NOPE + MLA_ROPE)
    kv = (rms_norm(ckv, kv_norm_g) @ w_kvb).reshape(b, l, MLA_HEADS, MLA_NOPE + MLA_V)
    qkv = qkv.reshape(b, l, 3, NA_HEADS, NA_HEAD_DIM)
    return (q[..., :MLA_NOPE], q[..., MLA_NOPE:], kv[..., :MLA_NOPE], kr, kv[..., MLA_NOPE:],
            qkv[:, :, 0], qkv[:, :, 1], qkv[:, :, 2])


def mla_scores(qn, qr, kn, kr):
    return (jnp.einsum('bqhd,bkhd->bhqk', qn, kn) + jnp.einsum('bqhr,bkr->bhqk', qr, kr)) * MLA_SCALE


def mla_latent(qn, qr, kn, kr, v, kn_c, kr_c, v_c):
    b, l, h, _ = qn.shape
    nb = l // MLA_Q_BLOCK

    def to_blocks(t):
        return t.reshape((b, nb, MLA_Q_BLOCK) + t.shape[2:]).swapaxes(0, 1)

    def block(qs):
        qn_b, qr_b = qs
        s = jnp.concatenate([mla_scores(qn_b, qr_b, kn, kr), mla_scores(qn_b, qr_b, kn_c, kr_c)], axis=-1)
        p = softmax_f32(s).astype(v.dtype)
        return (jnp.einsum('bhqk,bkhd->bqhd', p[..., :l], v)
                + jnp.einsum('bhqk,bkhd->bqhd', p[..., l:], v_c))

    out = lax.map(block, (to_blocks(qn), to_blocks(qr)))
    return out.swapaxes(0, 1).reshape(b, l, h * MLA_V)


def na_latent(q, k, v, k_c, v_c, rpb):
    b, l, h, d = q.shape
    rows = l // GRID_W
    kh = min(NA_WIN_H, rows)
    r = np.arange(rows)
    row_start = np.clip(r - kh // 2, 0, rows - kh)
    row_off = row_start[:, None] + np.arange(kh)[None, :] - r[:, None] + (NA_WIN_H - 1)
    cidx = np.arange(GRID_W)
    col_start = np.clip(cidx - NA_WIN_W // 2, 0, GRID_W - NA_WIN_W)
    col_idx = col_start[:, None] + np.arange(NA_WIN_W)[None, :]
    col_off = col_idx - cidx[:, None] + (NA_WIN_W - 1)
    bias = rpb[:, row_off][..., col_off].transpose(1, 0, 3, 2, 4)
    n_win = kh * NA_WIN_W
    qg = (q * NA_SCALE).reshape(b, rows, GRID_W, h, d).swapaxes(0, 1)
    kg = k.reshape(b, rows, GRID_W, h, d)
    vg = v.reshape(b, rows, GRID_W, h, d)

    def row_block(xs):
        q_r, start, bias_r = xs
        k_win = lax.dynamic_slice_in_dim(kg, start, kh, axis=1)[:, :, col_idx]
        v_win = lax.dynamic_slice_in_dim(vg, start, kh, axis=1)[:, :, col_idx]
        s_win = jnp.einsum('bqhd,biqjhd->bhqij', q_r, k_win) + bias_r
        s_ctx = jnp.einsum('bqhd,bkhd->bhqk', q_r, k_c)
        s = jnp.concatenate([s_win.reshape(b, h, GRID_W, n_win), s_ctx], axis=-1)
        p = softmax_f32(s).astype(v.dtype)
        p_win = p[..., :n_win].reshape(b, h, GRID_W, kh, NA_WIN_W)
        return (jnp.einsum('bhqij,biqjhd->bqhd', p_win, v_win)
                + jnp.einsum('bhqk,bkhd->bqhd', p[..., n_win:], v_c))

    out = lax.map(row_block, (qg, jnp.asarray(row_start, jnp.int32), bias))
    return out.swapaxes(0, 1).reshape(b, l, h * d)


def even_mixer(h, hc, w_in, q_norm_g, w_qb, kv_norm_g, w_kvb, rpb, w_out, need_ctx):
    b, l, _ = h.shape
    qn, qr, kn, kr, v, nq, nk, nv = even_project(h, w_in, q_norm_g, w_qb, kv_norm_g, w_kvb)
    qn_c, qr_c, kn_c, kr_c, v_c, nq_c, nk_c, nv_c = even_project(hc, w_in, q_norm_g, w_qb, kv_norm_g, w_kvb)
    t = jnp.arange(l)
    row, col = t // GRID_W, t % GRID_W
    qr = axial_rope(qr, row, col)
    kr = axial_rope(kr, row, col)
    mla = mla_latent(qn, qr, kn, kr, v, kn_c, kr_c, v_c)
    na = na_latent(nq, nk, nv, nk_c, nv_c, rpb)
    y = jnp.concatenate([mla, na], axis=-1) @ w_out
    if not need_ctx:
        return y, None
    lc = hc.shape[1]
    mla_c = dense_attend(mla_scores(qn_c, qr_c, kn_c, kr_c), v_c).reshape(b, lc, MLA_HEADS * MLA_V)
    na_c = dense_attend(jnp.einsum('bqhd,bkhd->bhqk', nq_c * NA_SCALE, nk_c), nv_c).reshape(b, lc, NA_HEADS * NA_HEAD_DIM)
    yc = jnp.concatenate([mla_c, na_c], axis=-1) @ w_out
    return y, yc


def s5_discretize(a_re, a_im, log_step, b_re, b_im):
    dt = jnp.exp(log_step)[:, None]
    decay = jnp.exp(a_re * dt)
    ab_re = decay * jnp.cos(a_im * dt)
    ab_im = decay * jnp.sin(a_im * dt)
    den = a_re * a_re + a_im * a_im
    f_re = ((ab_re - 1) * a_re + ab_im * a_im) / den
    f_im = (ab_im * a_re - (ab_re - 1) * a_im) / den
    bb_re = f_re[..., None] * b_re - f_im[..., None] * b_im
    bb_im = f_re[..., None] * b_im + f_im[..., None] * b_re
    return ab_re, ab_im, bb_re, bb_im


def linear_recurrence(e1, e2):
    a1r, a1i, b1r, b1i = e1
    a2r, a2i, b2r, b2i = e2
    return (a2r * a1r - a2i * a1i, a2r * a1i + a2i * a1r,
            a2r * b1r - a2i * b1i + b2r, a2r * b1i + a2i * b1r + b2i)


def ssm_scan(bu_re, bu_im, ab_re, ab_im, h0, reverse):
    if reverse:
        bu_re, bu_im = jnp.flip(bu_re, axis=1), jnp.flip(bu_im, axis=1)
    if h0 is not None:
        h0r, h0i = h0
        bu_re = bu_re.at[:, 0].add(ab_re * h0r - ab_im * h0i)
        bu_im = bu_im.at[:, 0].add(ab_re * h0i + ab_im * h0r)
    l = bu_re.shape[1]
    a_re = jnp.broadcast_to(ab_re, (1, l) + ab_re.shape)
    a_im = jnp.broadcast_to(ab_im, (1, l) + ab_im.shape)
    _, _, hr, hi = lax.associative_scan(linear_recurrence, (a_re, a_im, bu_re, bu_im), axis=1)
    if reverse:
        hr, hi = jnp.flip(hr, axis=1), jnp.flip(hi, axis=1)
    return hr, hi


def glu(t, w_glu):
    z = t @ w_glu
    return z[..., :D_MODEL] * jax.nn.sigmoid(z[..., D_MODEL:])


def s5_mixer(h, hc, w_in, a_re, a_im, log_step, b_re, b_im, c_re, c_im, d_skip, w_glu, need_ctx):
    b, l, _ = h.shape
    lc = hc.shape[1]
    u = h @ w_in
    u_c = hc @ w_in
    ug = u.reshape(b, l, S5_GROUPS, S5_GROUP)
    ug_c = u_c.reshape(b, lc, S5_GROUPS, S5_GROUP)
    y = u * d_skip
    y_c = u_c * d_skip if need_ctx else None
    for direction in range(2):
        reverse = direction == 1
        ab_re, ab_im, bb_re, bb_im = s5_discretize(a_re[direction], a_im[direction], log_step[direction],
                                                   b_re[direction], b_im[direction])
        cr, ci = c_re[direction], c_im[direction]

        def drive(t):
            return (jnp.einsum('blgi,gpi->blgp', t, bb_re), jnp.einsum('blgi,gpi->blgp', t, bb_im))

        def readout(hr, hi):
            out = jnp.einsum('blgp,gip->blgi', hr, cr) - jnp.einsum('blgp,gip->blgi', hi, ci)
            return out.reshape(hr.shape[0], hr.shape[1], S5_WIDTH)

        hc_re, hc_im = ssm_scan(*drive(ug_c), ab_re, ab_im, None, reverse)
        end = 0 if reverse else -1
        h_re, h_im = ssm_scan(*drive(ug), ab_re, ab_im, (hc_re[:, end], hc_im[:, end]), reverse)
        y = y + readout(h_re, h_im)
        if need_ctx:
            y_c = y_c + readout(hc_re, hc_im)
    out = glu(jax.nn.gelu(y), w_glu)
    out_c = glu(jax.nn.gelu(y_c), w_glu) if need_ctx else None
    return out, out_c


def swiglu(t, w_gate, w_up, w_down):
    return (jax.nn.silu(t @ w_gate) * (t @ w_up)) @ w_down


def moe_swiglu(t, w_router, w_gate, w_up, w_down):
    logits = (t @ w_router).astype(jnp.float32)
    top_val, top_idx = lax.top_k(logits, TOP_K)
    top_w = jax.nn.softmax(top_val, axis=-1)
    comb = jnp.sum(jax.nn.one_hot(top_idx, N_EXPERTS, dtype=jnp.float32) * top_w[..., None], axis=-2).astype(t.dtype)
    out = comb[..., 0:1] * swiglu(t, w_gate[0], w_up[0], w_down[0])
    for e in range(1, N_EXPERTS):
        out = out + comb[..., e:e + 1] * swiglu(t, w_gate[e], w_up[e], w_down[e])
    return out


def setup_inputs(seed: int = 0) -> dict:
    key = jax.random.key(seed)
    ks = iter(jax.random.split(key, 48))
    D = D_MODEL

    def nrm(shape, scale):
        return scale * jax.random.normal(next(ks), shape, jnp.float32)

    def gain(shape):
        return 1.0 + nrm(shape, 0.05)

    inp = {}
    inp['x'] = nrm((BATCH, SEQ, D), 1.0)
    inp['c'] = nrm((BATCH, D), 1.0)
    inp['ctx'] = nrm((BATCH, CTX_LEN, D), 1.0)
    inp['c_ctx'] = nrm((D,), 1.0)
    inp['mod_w'] = nrm((DEPTH, D, N_MOD * D), 0.5 * D ** -0.5)
    inp['mod_b'] = nrm((DEPTH, N_MOD * D), 0.02)
    inp['norm1_g'] = gain((DEPTH, D))
    inp['norm2_g'] = gain((DEPTH, D))
    inp['ev_w_in'] = nrm((N_EVEN, D, EVEN_IN), D ** -0.5)
    inp['ev_q_norm_g'] = gain((N_EVEN, Q_LORA))
    inp['ev_w_qb'] = nrm((N_EVEN, Q_LORA, MLA_HEADS * (MLA_NOPE + MLA_ROPE)), Q_LORA ** -0.5)
    inp['ev_kv_norm_g'] = gain((N_EVEN, KV_LORA))
    inp['ev_w_kvb'] = nrm((N_EVEN, KV_LORA, MLA_HEADS * (MLA_NOPE + MLA_V)), KV_LORA ** -0.5)
    inp['ev_na_rpb'] = nrm((N_EVEN, NA_HEADS, 2 * NA_WIN_H - 1, 2 * NA_WIN_W - 1), 0.2)
    inp['ev_w_out'] = nrm((N_EVEN, EVEN_OUT, D), EVEN_OUT ** -0.5)
    inp['ev_ffn_w_gate'] = nrm((N_EVEN, D, D_FF), D ** -0.5)
    inp['ev_ffn_w_up'] = nrm((N_EVEN, D, D_FF), D ** -0.5)
    inp['ev_ffn_w_down'] = nrm((N_EVEN, D_FF, D), D_FF ** -0.5)
    inp['od_w_in'] = nrm((N_ODD, D, S5_WIDTH), D ** -0.5)
    inp['od_a_re'] = -0.5 + nrm((N_ODD, 2, S5_GROUPS, S5_STATE), 0.01)
    inp['od_a_im'] = math.pi * jnp.arange(S5_STATE, dtype=jnp.float32) + nrm((N_ODD, 2, S5_GROUPS, S5_STATE), 0.01)
    inp['od_log_step'] = jax.random.uniform(next(ks), (N_ODD, 2, S5_GROUPS), jnp.float32,
                                            math.log(1e-3), math.log(1e-1))
    inp['od_b_re'] = nrm((N_ODD, 2, S5_GROUPS, S5_STATE, S5_GROUP), (2 * S5_GROUP) ** -0.5)
    inp['od_b_im'] = nrm((N_ODD, 2, S5_GROUPS, S5_STATE, S5_GROUP), (2 * S5_GROUP) ** -0.5)
    inp['od_c_re'] = nrm((N_ODD, 2, S5_GROUPS, S5_GROUP, S5_STATE), S5_STATE ** -0.5)
    inp['od_c_im'] = nrm((N_ODD, 2, S5_GROUPS, S5_GROUP, S5_STATE), S5_STATE ** -0.5)
    inp['od_d'] = nrm((N_ODD, S5_WIDTH), 1.0)
    inp['od_w_glu'] = nrm((N_ODD, S5_WIDTH, 2 * D), S5_WIDTH ** -0.5)
    inp['moe_w_router'] = nrm((N_ODD, D, N_EXPERTS), D ** -0.5)
    inp['moe_w_gate'] = nrm((N_ODD, N_EXPERTS, D, D_FF_EXPERT), D ** -0.5)
    inp['moe_w_up'] = nrm((N_ODD, N_EXPERTS, D, D_FF_EXPERT), D ** -0.5)
    inp['moe_w_down'] = nrm((N_ODD, N_EXPERTS, D_FF_EXPERT, D), D_FF_EXPERT ** -0.5)
    inp['final_g'] = gain((D,))
    return inp


def reference(x, c, ctx, c_ctx, mod_w, mod_b, norm1_g, norm2_g,
              ev_w_in, ev_q_norm_g, ev_w_qb, ev_kv_norm_g, ev_w_kvb, ev_na_rpb, ev_w_out,
              ev_ffn_w_gate, ev_ffn_w_up, ev_ffn_w_down,
              od_w_in, od_a_re, od_a_im, od_log_step, od_b_re, od_b_im, od_c_re, od_c_im, od_d, od_w_glu,
              moe_w_router, moe_w_gate, moe_w_up, moe_w_down, final_g):
    xc = ctx
    for layer in range(DEPTH):
        last = layer == DEPTH - 1
        i = layer // 2
        sh1, sc1, g1, sh2, sc2, g2 = adaln_params(c, mod_w[layer], mod_b[layer])
        sh1c, sc1c, g1c, sh2c, sc2c, g2c = adaln_params(c_ctx, mod_w[layer], mod_b[layer])
        h = modulate(x, norm1_g[layer], sh1, sc1)
        hc = modulate(xc, norm1_g[layer], sh1c, sc1c)
        if layer % 2 == 0:
            y, yc = even_mixer(h, hc, ev_w_in[i], ev_q_norm_g[i], ev_w_qb[i], ev_kv_norm_g[i],
                               ev_w_kvb[i], ev_na_rpb[i], ev_w_out[i], not last)

            def channel_mix(t):
                return swiglu(t, ev_ffn_w_gate[i], ev_ffn_w_up[i], ev_ffn_w_down[i])
        else:
            y, yc = s5_mixer(h, hc, od_w_in[i], od_a_re[i], od_a_im[i], od_log_step[i], od_b_re[i],
                             od_b_im[i], od_c_re[i], od_c_im[i], od_d[i], od_w_glu[i], not last)

            def channel_mix(t):
                return moe_swiglu(t, moe_w_router[i], moe_w_gate[i], moe_w_up[i], moe_w_down[i])
        x = x + g1 * y
        x = x + g2 * channel_mix(modulate(x, norm2_g[layer], sh2, sc2))
        if not last:
            xc = xc + g1c * yc
            xc = xc + g2c * channel_mix(modulate(xc, norm2_g[layer], sh2c, sc2c))
    return rms_norm(x, final_g)
```

```python
import functools
import math

import numpy as np
import jax
import jax.numpy as jnp
from jax import lax
from jax.experimental import pallas as pl
from jax.experimental.pallas import tpu as pltpu

F32 = jnp.float32
BF16 = jnp.bfloat16

LANES = 128
VMEM_LIMIT_BYTES = 52 * 1024 * 1024

NORM_EPS = 1e-6
ROPE_BASE = 10000.0
GRID_W = 64
N_MOD = 6

MLA_HEADS = 8
MLA_NOPE = 64
MLA_ROPE = 32
MLA_V = 64
Q_LORA = 384
KV_LORA = 256
MLA_SCALE = (MLA_NOPE + MLA_ROPE) ** -0.5

NA_HEADS = 8
NA_HEAD_DIM = 64
NA_WIN_H = 8
NA_WIN_W = 16
NA_SCALE = NA_HEAD_DIM ** -0.5
NA_ROWS_PER_BLOCK = 4
NA_KEY_ROWS = NA_ROWS_PER_BLOCK + NA_WIN_H - 1

S5_GROUP = 16
S5_STATE = 64
S5_CHUNK = 16

N_EXPERTS = 8
TOP_K = 2

NEG_BIG = -1e30


def _cparams(*sem):
    return pltpu.CompilerParams(dimension_semantics=sem, vmem_limit_bytes=VMEM_LIMIT_BYTES)


def _rms(x):
    return x * lax.rsqrt(jnp.mean(x * x, axis=-1, keepdims=True) + NORM_EPS)


def _norm_mod(x, g, scale, shift):
    return (_rms(x) * g) * (1 + scale) + shift


def _dot(a, b):
    return jnp.dot(a, b, preferred_element_type=F32)


def _dot_nt(a, b):
    return lax.dot_general(a, b, (((1,), (1,)), ((), ())), preferred_element_type=F32)


def _adaln_kernel(c_ref, w_ref, b_ref, o_ref):
    c = c_ref[...]
    s = (c * jax.nn.sigmoid(c)).astype(BF16)
    o_ref[0] = _dot(s, w_ref[0].astype(BF16)) + b_ref[0]


def _adaln(cond, mod_w, mod_b):
    nl, d, n = mod_w.shape
    tn = 1536
    return pl.pallas_call(
        _adaln_kernel,
        grid=(nl, n // tn),
        in_specs=[pl.BlockSpec((8, d), lambda l, j: (0, 0)),
                  pl.BlockSpec((1, d, tn), lambda l, j: (l, 0, j)),
                  pl.BlockSpec((1, 1, tn), lambda l, j: (l, 0, j))],
        out_specs=pl.BlockSpec((1, 8, tn), lambda l, j: (l, 0, j)),
        out_shape=jax.ShapeDtypeStruct((nl, 8, n), F32),
        compiler_params=_cparams("parallel", "parallel"),
        name="adaln",
    )(cond, mod_w, mod_b.reshape(nl, 1, n))


def _even_proj_kernel(x_ref, g_ref, sc_ref, sh_ref, win_ref, qg_ref, kvg_ref, wq_ref, wqr_ref, wkk_ref,
                      wkv_ref, cos_ref, sin_ref, q_ref, k_ref, v_ref, nq_ref, nk_ref, nv_ref):
    h = _norm_mod(x_ref[0], g_ref[...], sc_ref[0], sh_ref[0]).astype(BF16)
    p = _dot(h, win_ref[...])
    c0, c1, c2, c3 = Q_LORA, Q_LORA + KV_LORA, Q_LORA + KV_LORA + LANES, Q_LORA + KV_LORA + 2 * LANES
    cqn = (_rms(p[:, :c0]) * qg_ref[...]).astype(BF16)
    ckvn = (_rms(p[:, c0:c1]) * kvg_ref[...]).astype(BF16)
    cos = cos_ref[...]
    sin = sin_ref[...]
    kr = p[:, c1:c2] * cos + p[:, c2:c3] * sin
    qa = _dot(cqn, wq_ref[...])
    qb = _dot(cqn, wqr_ref[...])
    kk = _dot(ckvn, wkk_ref[...])
    for hd in range(MLA_HEADS):
        sl = slice(hd * LANES, (hd + 1) * LANES)
        q_ref[0, :, sl] = (qa[:, sl] * cos + qb[:, sl] * sin).astype(BF16)
        k_ref[0, :, sl] = (kk[:, sl] + kr).astype(BF16)
    v_ref[0] = _dot(ckvn, wkv_ref[...]).astype(BF16)
    w = NA_HEADS * NA_HEAD_DIM
    nq_ref[0] = (p[:, c3:c3 + w] * NA_SCALE).astype(BF16)
    nk_ref[0] = p[:, c3 + w:c3 + 2 * w].astype(BF16)
    nv_ref[0] = p[:, c3 + 2 * w:c3 + 3 * w].astype(BF16)


def _rot_half_cols(w):
    q = MLA_ROPE // 4
    return jnp.concatenate([-w[:, q:2 * q], w[:, :q], -w[:, 3 * q:], w[:, 2 * q:3 * q]], axis=1)


def _rope_tables(seq, ctx_len):
    q = MLA_ROPE // 4
    t = np.arange(seq)
    inv_freq = ROPE_BASE ** (-jnp.arange(q, dtype=F32) / q)
    ang_r = jnp.asarray(t // GRID_W, F32)[:, None] * inv_freq[None, :]
    ang_c = jnp.asarray(t % GRID_W, F32)[:, None] * inv_freq[None, :]
    cos32 = jnp.concatenate([jnp.cos(ang_r)] * 2 + [jnp.cos(ang_c)] * 2, axis=1)
    sin32 = jnp.concatenate([jnp.sin(ang_r)] * 2 + [jnp.sin(ang_c)] * 2, axis=1)
    cos = jnp.ones((seq + ctx_len, LANES), F32).at[:seq, MLA_NOPE:MLA_NOPE + MLA_ROPE].set(cos32)
    sin = jnp.zeros((seq + ctx_len, LANES), F32).at[:seq, MLA_NOPE:MLA_NOPE + MLA_ROPE].set(sin32)
    return cos, sin


def _even_weights(w_in, w_qb, w_kvb):
    d = w_in.shape[0]
    c1 = Q_LORA + KV_LORA
    wkr = w_in[:, c1:c1 + MLA_ROPE]
    pad = lambda w: jnp.zeros((d, LANES), F32).at[:, MLA_NOPE:MLA_NOPE + MLA_ROPE].set(w)
    win = jnp.concatenate([w_in[:, :c1], pad(wkr), pad(_rot_half_cols(wkr)), w_in[:, c1 + MLA_ROPE:]], axis=1)
    hq = MLA_NOPE + MLA_ROPE
    wq = w_qb.reshape(Q_LORA, MLA_HEADS, hq)
    zq = jnp.zeros((Q_LORA, MLA_HEADS, LANES - hq), F32)
    wq_main = jnp.concatenate([wq, zq], axis=2).reshape(Q_LORA, MLA_HEADS * LANES)
    rot = jnp.stack([_rot_half_cols(wq[:, h, MLA_NOPE:]) for h in range(MLA_HEADS)], axis=1)
    wq_rot = jnp.concatenate([jnp.zeros((Q_LORA, MLA_HEADS, MLA_NOPE), F32), rot, zq], axis=2)
    wq_rot = wq_rot.reshape(Q_LORA, MLA_HEADS * LANES)
    wkv = w_kvb.reshape(KV_LORA, MLA_HEADS, MLA_NOPE + MLA_V)
    wkk = jnp.concatenate([wkv[:, :, :MLA_NOPE], jnp.zeros((KV_LORA, MLA_HEADS, LANES - MLA_NOPE), F32)], axis=2)
    wkk = wkk.reshape(KV_LORA, MLA_HEADS * LANES)
    wv = wkv[:, :, MLA_NOPE:].reshape(KV_LORA, MLA_HEADS * MLA_V)
    return tuple(a.astype(BF16) for a in (win, wq_main, wq_rot, wkk, wv))


def _mod_spec(n_lat_tiles, nb, d):
    return pl.BlockSpec((1, 1, d), lambda b, t: (jnp.where(t < n_lat_tiles, b, nb), 0, 0))


def _even_project(xu, norm_g, scale, shift, weights, q_norm_g, kv_norm_g, cos, sin, seq):
    nb, lt, d = xu.shape
    tm = 256
    win, wq, wqr, wkk, wv = weights
    n_lat = seq // tm
    row = lambda b, t: (b, t, 0)
    full = lambda a: pl.BlockSpec(a.shape, lambda b, t: (0,) * a.ndim)
    mod = _mod_spec(n_lat, nb, d)
    g2 = norm_g.reshape(1, d)
    qg2 = q_norm_g.reshape(1, Q_LORA)
    kvg2 = kv_norm_g.reshape(1, KV_LORA)
    wide = MLA_HEADS * LANES
    half = MLA_HEADS * MLA_V
    outs = [jax.ShapeDtypeStruct((nb, lt, wide), BF16), jax.ShapeDtypeStruct((nb, lt, wide), BF16)] + \
           [jax.ShapeDtypeStruct((nb, lt, half), BF16)] * 4
    return pl.pallas_call(
        _even_proj_kernel,
        grid=(nb, lt // tm),
        in_specs=[pl.BlockSpec((1, tm, d), row), full(g2), mod, mod, full(win), full(qg2), full(kvg2),
                  full(wq), full(wqr), full(wkk), full(wv),
                  pl.BlockSpec((tm, LANES), lambda b, t: (t, 0)), pl.BlockSpec((tm, LANES), lambda b, t: (t, 0))],
        out_specs=[pl.BlockSpec((1, tm, wide), row), pl.BlockSpec((1, tm, wide), row)] +
                  [pl.BlockSpec((1, tm, half), row)] * 4,
        out_shape=outs,
        compiler_params=_cparams("parallel", "parallel"),
        name="even_project",
    )(xu, g2, scale, shift, win, qg2, kvg2, wq, wqr, wkk, wv, cos, sin)


def _softmax_step(q, kc, vc, carry, scale):
    m, l, acc = carry
    s = _dot_nt(q, kc)
    if scale is not None:
        s = s * scale
    m_new = jnp.maximum(m, jnp.max(s, axis=-1, keepdims=True))
    alpha = jnp.exp(m - m_new)
    p = jnp.exp(s - m_new)
    l = alpha * l + jnp.sum(p, axis=-1, keepdims=True)
    acc = alpha * acc + _dot(p.astype(BF16), vc)
    return m_new, l, acc


def _mla_kernel(q_ref, k_ref, v_ref, o_ref, *, seq, ctx_len, tk):
    tq = q_ref.shape[1]
    n_lat_tiles = seq // tq
    t = pl.program_id(1)
    lane = lax.broadcasted_iota(jnp.int32, (tq, LANES), 1)

    def run(with_latent):
        for pair in range(MLA_HEADS // 2):
            vsl = slice(pair * LANES, (pair + 1) * LANES)
            outs = []
            for sub in range(2):
                hd = 2 * pair + sub
                ksl = slice(hd * LANES, (hd + 1) * LANES)
                q = q_ref[0, :, ksl]
                carry = (jnp.full((tq, 1), NEG_BIG, F32), jnp.zeros((tq, 1), F32), jnp.zeros((tq, LANES), F32))
                if with_latent:
                    def body(i, c):
                        off = pl.multiple_of(i * tk, tk)
                        return _softmax_step(q, k_ref[0, pl.ds(off, tk), ksl], v_ref[0, pl.ds(off, tk), vsl],
                                             c, MLA_SCALE)
                    carry = lax.fori_loop(0, seq // tk, body, carry)
                carry = _softmax_step(q, k_ref[0, seq:seq + ctx_len, ksl], v_ref[0, seq:seq + ctx_len, vsl],
                                      carry, MLA_SCALE)
                _, l, acc = carry
                outs.append(acc / l)
            o_ref[0, :, vsl] = jnp.where(lane < MLA_V, outs[0], outs[1]).astype(BF16)

    @pl.when(t < n_lat_tiles)
    def _():
        run(True)

    @pl.when(t >= n_lat_tiles)
    def _():
        run(False)


def _mla_attention(q, k, v, seq, ctx_len):
    nb, lt, wide = q.shape
    half = v.shape[2]
    tq = 256
    kern = functools.partial(_mla_kernel, seq=seq, ctx_len=ctx_len, tk=512)
    return pl.pallas_call(
        kern,
        grid=(nb, lt // tq),
        in_specs=[pl.BlockSpec((1, tq, wide), lambda b, t: (b, t, 0)),
                  pl.BlockSpec((1, lt, wide), lambda b, t: (b, 0, 0)),
                  pl.BlockSpec((1, lt, half), lambda b, t: (b, 0, 0))],
        out_specs=pl.BlockSpec((1, tq, half), lambda b, t: (b, t, 0)),
        out_shape=jax.ShapeDtypeStruct((nb, lt, half), BF16),
        compiler_params=_cparams("parallel", "parallel"),
        name="mla_attention",
    )(q, k, v)


def _na_bias_tables(rpb, rows):
    kh = min(NA_WIN_H, rows)
    last_r0 = rows - NA_ROWS_PER_BLOCK
    tabs = []
    for r0 in (0, 2 * NA_ROWS_PER_BLOCK, last_r0):
        kstart = int(np.clip(r0 - kh // 2, 0, rows - NA_KEY_ROWS))
        r = r0 + np.arange(NA_ROWS_PER_BLOCK)
        rs = np.clip(r - kh // 2, 0, rows - kh)
        kr = kstart + np.arange(NA_KEY_ROWS)
        row_ok = (kr[None, :] >= rs[:, None]) & (kr[None, :] < rs[:, None] + kh)
        row_off = np.clip(kr[None, :] - r[:, None] + (NA_WIN_H - 1), 0, 2 * NA_WIN_H - 2)
        c = np.arange(GRID_W)
        cs = np.clip(c - NA_WIN_W // 2, 0, GRID_W - NA_WIN_W)
        col_ok = (c[None, :] >= cs[:, None]) & (c[None, :] < cs[:, None] + NA_WIN_W)
        col_off = np.clip(c[None, :] - c[:, None] + (NA_WIN_W - 1), 0, 2 * NA_WIN_W - 2)
        b = rpb[:, row_off][..., col_off]
        ok = row_ok[:, :, None, None] & col_ok[None, None, :, :]
        b = jnp.where(jnp.asarray(ok)[None], b, NEG_BIG).transpose(0, 1, 3, 2, 4)
        tabs.append(b.reshape(rpb.shape[0], NA_ROWS_PER_BLOCK * GRID_W, NA_KEY_ROWS * GRID_W))
    return jnp.stack(tabs)


def _na_kernel(q_ref, k_ref, v_ref, bias_ref, o_ref, *, seq, ctx_len):
    tq = q_ref.shape[1]
    rows = seq // GRID_W
    n_lat_tiles = seq // tq
    nkeys = NA_KEY_ROWS * GRID_W
    t = pl.program_id(1)
    lane = lax.broadcasted_iota(jnp.int32, (tq, LANES), 1)
    kstart = jnp.clip(t * NA_ROWS_PER_BLOCK - NA_WIN_H // 2, 0, rows - NA_KEY_ROWS)
    koff = pl.multiple_of(kstart * GRID_W, GRID_W)

    def run(with_window):
        for pair in range(NA_HEADS // 2):
            sl = slice(pair * LANES, (pair + 1) * LANES)
            q2 = q_ref[0, :, sl].astype(F32)
            kc = k_ref[0, seq:seq + ctx_len, sl]
            vc = v_ref[0, seq:seq + ctx_len, sl]
            if with_window:
                kw = k_ref[0, pl.ds(koff, nkeys), sl]
                vw = v_ref[0, pl.ds(koff, nkeys), sl]
            outs = []
            for sub in range(2):
                keep = (lane < NA_HEAD_DIM) if sub == 0 else (lane >= NA_HEAD_DIM)
                q = jnp.where(keep, q2, 0.0).astype(BF16)
                s_c = _dot_nt(q, kc)
                m = jnp.max(s_c, axis=-1, keepdims=True)
                if with_window:
                    s_w = _dot_nt(q, kw) + bias_ref[0, 2 * pair + sub]
                    m = jnp.maximum(m, jnp.max(s_w, axis=-1, keepdims=True))
                p_c = jnp.exp(s_c - m)
                l = jnp.sum(p_c, axis=-1, keepdims=True)
                acc = _dot(p_c.astype(BF16), vc)
                if with_window:
                    p_w = jnp.exp(s_w - m)
                    l = l + jnp.sum(p_w, axis=-1, keepdims=True)
                    acc = acc + _dot(p_w.astype(BF16), vw)
                outs.append(acc / l)
            o_ref[0, :, sl] = jnp.where(lane < NA_HEAD_DIM, outs[0], outs[1]).astype(BF16)

    @pl.when(t < n_lat_tiles)
    def _():
        run(True)

    @pl.when(t >= n_lat_tiles)
    def _():
        run(False)


def _na_attention(q, k, v, bias, seq, ctx_len):
    nb, lt, w = q.shape
    tq = NA_ROWS_PER_BLOCK * GRID_W
    n_lat = seq // tq
    kern = functools.partial(_na_kernel, seq=seq, ctx_len=ctx_len)

    def variant(b, t):
        return (jnp.where(t == 0, 0, jnp.where(t >= n_lat - 1, 2, 1)), 0, 0, 0)

    return pl.pallas_call(
        kern,
        grid=(nb, lt // tq),
        in_specs=[pl.BlockSpec((1, tq, w), lambda b, t: (b, t, 0)),
                  pl.BlockSpec((1, lt, w), lambda b, t: (b, 0, 0)),
                  pl.BlockSpec((1, lt, w), lambda b, t: (b, 0, 0)),
                  pl.BlockSpec((1,) + bias.shape[1:], variant)],
        out_specs=pl.BlockSpec((1, tq, w), lambda b, t: (b, t, 0)),
        out_shape=jax.ShapeDtypeStruct((nb, lt, w), BF16),
        compiler_params=_cparams("parallel", "parallel"),
        name="na_attention",
    )(q, k, v, bias)


def _out_proj_kernel(x_ref, a_ref, b_ref, wa_ref, wb_ref, gate_ref, o_ref):
    y = _dot(a_ref[0], wa_ref[...]) + _dot(b_ref[0], wb_ref[...])
    o_ref[0] = x_ref[0] + gate_ref[0] * y


def _out_proj(xu, a, b, w_out, gate, seq):
    nb, lt, d = xu.shape
    tm = 256
    ka = a.shape[2]
    wa = w_out[:ka].astype(BF16)
    wb = w_out[ka:].astype(BF16)
    row = lambda bb, t: (bb, t, 0)
    full = lambda arr: pl.BlockSpec(arr.shape, lambda bb, t: (0,) * arr.ndim)
    return pl.pallas_call(
        _out_proj_kernel,
        grid=(nb, lt // tm),
        in_specs=[pl.BlockSpec((1, tm, d), row), pl.BlockSpec((1, tm, ka), row),
                  pl.BlockSpec((1, tm, b.shape[2]), row), full(wa), full(wb), _mod_spec(seq // tm, nb, d)],
        out_specs=pl.BlockSpec((1, tm, d), row),
        out_shape=jax.ShapeDtypeStruct(xu.shape, F32),
        compiler_params=_cparams("parallel", "parallel"),
        name="out_proj",
    )(xu, a, b, wa, wb, gate)


def _ffn_kernel(x_ref, g_ref, sc_ref, sh_ref, gate_ref, wg_ref, wu_ref, wd_ref, o_ref, h_scr, acc_scr):
    f = pl.program_id(2)

    @pl.when(f == 0)
    def _():
        h_scr[...] = _norm_mod(x_ref[0], g_ref[...], sc_ref[0], sh_ref[0]).astype(BF16)
        acc_scr[...] = jnp.zeros_like(acc_scr)

    h = h_scr[...]
    a = _dot(h, wg_ref[...])
    u = _dot(h, wu_ref[...])
    acc_scr[...] += _dot((a * jax.nn.sigmoid(a) * u).astype(BF16), wd_ref[...])

    @pl.when(f == pl.num_programs(2) - 1)
    def _():
        o_ref[0] = x_ref[0] + gate_ref[0] * acc_scr[...]


def _ffn(xu, norm_g, scale, shift, gate, w_gate, w_up, w_down, seq):
    nb, lt, d = xu.shape
    dff = w_gate.shape[1]
    tm = 256
    tf = dff // 2 if (dff // 2) % LANES == 0 else dff
    row = lambda b, t, f: (b, t, 0)
    n_lat = seq // tm
    mod = pl.BlockSpec((1, 1, d), lambda b, t, f: (jnp.where(t < n_lat, b, nb), 0, 0))
    g2 = norm_g.reshape(1, d)
    return pl.pallas_call(
        _ffn_kernel,
        grid=(nb, lt // tm, dff // tf),
        in_specs=[pl.BlockSpec((1, tm, d), row), pl.BlockSpec((1, d), lambda b, t, f: (0, 0)), mod, mod, mod,
                  pl.BlockSpec((d, tf), lambda b, t, f: (0, f)), pl.BlockSpec((d, tf), lambda b, t, f: (0, f)),
                  pl.BlockSpec((tf, d), lambda b, t, f: (f, 0))],
        out_specs=pl.BlockSpec((1, tm, d), row),
        out_shape=jax.ShapeDtypeStruct(xu.shape, F32),
        scratch_shapes=[pltpu.VMEM((tm, d), BF16), pltpu.VMEM((tm, d), F32)],
        compiler_params=_cparams("parallel", "parallel", "arbitrary"),
        name="dense_swiglu",
    )(xu, g2, scale, shift, gate, w_gate.astype(BF16), w_up.astype(BF16), w_down.astype(BF16))


def _s5_in_kernel(x_ref, g_ref, sc_ref, sh_ref, w_ref, o_ref):
    h = _norm_mod(x_ref[0], g_ref[...], sc_ref[0], sh_ref[0]).astype(BF16)
    o_ref[0] = _dot(h, w_ref[...])


def _s5_in(xu, norm_g, scale, shift, w_in, seq):
    nb, lt, d = xu.shape
    tm = 256
    w = w_in.astype(BF16)
    row = lambda b, t: (b, t, 0)
    mod = _mod_spec(seq // tm, nb, d)
    g2 = norm_g.reshape(1, d)
    return pl.pallas_call(
        _s5_in_kernel,
        grid=(nb, lt // tm),
        in_specs=[pl.BlockSpec((1, tm, d), row), pl.BlockSpec((1, d), lambda b, t: (0, 0)), mod, mod,
                  pl.BlockSpec(w.shape, lambda b, t: (0, 0))],
        out_specs=pl.BlockSpec((1, tm, w.shape[1]), row),
        out_shape=jax.ShapeDtypeStruct((nb, lt, w.shape[1]), F32),
        compiler_params=_cparams("parallel", "parallel"),
        name="s5_in",
    )(xu, g2, scale, shift, w)


def _s5_operators(a_re, a_im, log_step, b_re, b_im, c_re, c_im):
    hp = lax.Precision.HIGHEST
    t_len = S5_CHUNK
    ops = []
    for direction in range(2):
        dt = jnp.exp(log_step[direction])[:, None]
        lre, lim = a_re[direction] * dt, a_im[direction] * dt
        decay = jnp.exp(lre)
        ab_re, ab_im = decay * jnp.cos(lim), decay * jnp.sin(lim)
        den = a_re[direction] ** 2 + a_im[direction] ** 2
        f_re = ((ab_re - 1) * a_re[direction] + ab_im * a_im[direction]) / den
        f_im = (ab_im * a_re[direction] - (ab_re - 1) * a_im[direction]) / den
        bb_re = f_re[..., None] * b_re[direction] - f_im[..., None] * b_im[direction]
        bb_im = f_re[..., None] * b_im[direction] + f_im[..., None] * b_re[direction]
        cr, ci = c_re[direction], c_im[direction]
        tau = jnp.arange(t_len + 1, dtype=F32)[:, None, None]
        pw = jnp.exp(tau * lre[None])
        pw_re, pw_im = pw * jnp.cos(tau * lim[None]), pw * jnp.sin(tau * lim[None])
        ab_b_re = pw_re[..., None] * bb_re[None] - pw_im[..., None] * bb_im[None]
        ab_b_im = pw_re[..., None] * bb_im[None] + pw_im[..., None] * bb_re[None]
        kk = (jnp.einsum('gip,tgpj->tgij', cr, ab_b_re, precision=hp)
              - jnp.einsum('gip,tgpj->tgij', ci, ab_b_im, precision=hp))
        ca_re = cr[None] * pw_re[:, :, None, :] - ci[None] * pw_im[:, :, None, :]
        ca_im = cr[None] * pw_im[:, :, None, :] + ci[None] * pw_re[:, :, None, :]
        s = np.arange(t_len)
        if direction == 0:
            lag = s[None, :] - s[:, None]
            z_pow = t_len - 1 - s
            c_pow = s + 1
        else:
            lag = s[:, None] - s[None, :]
            z_pow = s
            c_pow = t_len - s
        ok = lag >= 0
        m = jnp.where(jnp.asarray(ok)[:, :, None, None, None], kk[np.clip(lag, 0, t_len)], 0.0)
        g = m.shape[2]
        m = m.transpose(2, 0, 4, 1, 3).reshape(g, t_len * S5_GROUP, t_len * S5_GROUP)
        wz_re = ab_b_re[z_pow].transpose(1, 0, 3, 2).reshape(g, t_len * S5_GROUP, S5_STATE)
        wz_im = ab_b_im[z_pow].transpose(1, 0, 3, 2).reshape(g, t_len * S5_GROUP, S5_STATE)
        wc_re = ca_re[c_pow].transpose(1, 3, 0, 2).reshape(g, S5_STATE, t_len * S5_GROUP)
        wc_im = -ca_im[c_pow].transpose(1, 3, 0, 2).reshape(g, S5_STATE, t_len * S5_GROUP)
        a_t = jnp.stack([pw_re[t_len], pw_im[t_len]], axis=1)
        ops.append((m.astype(BF16), wz_re.astype(BF16), wz_im.astype(BF16),
                    wc_re.astype(BF16), wc_im.astype(BF16), a_t))
    return [jnp.stack([ops[0][i], ops[1][i]], axis=1) for i in range(6)]


def _s5_core_kernel(u_ref, m_ref, wzr_ref, wzi_ref, wcr_ref, wci_ref, at_ref, y_ref,
                    zr_scr, zi_scr, hr_scr, hi_scr, *, nb, n_chunks, n_ctx_chunks):
    u = u_ref[0]
    n_lat = n_chunks - n_ctx_chunks
    for direction in range(2):
        zr_scr[direction] = _dot(u, wzr_ref[0, direction])
        zi_scr[direction] = _dot(u, wzi_ref[0, direction])

    a_re = [at_ref[0, d, 0:1, :] for d in range(2)]
    a_im = [at_ref[0, d, 1:2, :] for d in range(2)]

    def step(k, carry):
        cf = jnp.where(k < n_ctx_chunks, n_lat + k, k - n_ctx_chunks)
        cb = n_chunks - 1 - k
        new = []
        for direction, c in ((0, cf), (1, cb)):
            for b in range(nb):
                sr, si = carry[direction * nb + b]
                r = b * n_chunks + c
                hr_scr[direction, pl.ds(r, 1), :] = sr
                hi_scr[direction, pl.ds(r, 1), :] = si
                zr = zr_scr[direction, pl.ds(r, 1), :]
                zi = zi_scr[direction, pl.ds(r, 1), :]
                new.append((a_re[direction] * sr - a_im[direction] * si + zr,
                            a_re[direction] * si + a_im[direction] * sr + zi))
        return tuple(new)

    zero = jnp.zeros((1, S5_STATE), F32)
    lax.fori_loop(0, n_chunks, step, tuple((zero, zero) for _ in range(2 * nb)))

    y = _dot(u, m_ref[0, 0]) + _dot(u, m_ref[0, 1])
    for direction in range(2):
        y = y + _dot(hr_scr[direction].astype(BF16), wcr_ref[0, direction])
        y = y + _dot(hi_scr[direction].astype(BF16), wci_ref[0, direction])
    y_ref[0] = y


def _s5_core(u_t, ops, nb, n_chunks, n_ctx_chunks):
    g, r, w = u_t.shape
    m, wzr, wzi, wcr, wci, a_t = ops
    kern = functools.partial(_s5_core_kernel, nb=nb, n_chunks=n_chunks, n_ctx_chunks=n_ctx_chunks)
    blk = lambda a: pl.BlockSpec((1,) + a.shape[1:], lambda i: (i,) + (0,) * (a.ndim - 1))
    return pl.pallas_call(
        kern,
        grid=(g,),
        in_specs=[blk(u_t), blk(m), blk(wzr), blk(wzi), blk(wcr), blk(wci), blk(a_t)],
        out_specs=pl.BlockSpec((1, r, w), lambda i: (i, 0, 0)),
        out_shape=jax.ShapeDtypeStruct((g, r, w), F32),
        scratch_shapes=[pltpu.VMEM((2, r, S5_STATE), F32)] * 4,
        compiler_params=_cparams("parallel"),
        name="s5_core",
    )(u_t, m, wzr, wzi, wcr, wci, a_t)


def _s5_out_kernel(x_ref, u_ref, y_ref, d_ref, w_ref, gate_ref, o_ref):
    d = x_ref.shape[2]
    y = u_ref[0] * d_ref[...] + y_ref[0]
    z = _dot(jax.nn.gelu(y).astype(BF16), w_ref[...])
    o_ref[0] = x_ref[0] + gate_ref[0] * (z[:, :d] * jax.nn.sigmoid(z[:, d:]))


def _s5_out(x, u, y, d_skip, w_glu, gate):
    nb, seq, d = x.shape
    tm = 256
    w = u.shape[2]
    row = lambda b, t: (b, t, 0)
    wg = w_glu.astype(BF16)
    return pl.pallas_call(
        _s5_out_kernel,
        grid=(nb, seq // tm),
        in_specs=[pl.BlockSpec((1, tm, d), row), pl.BlockSpec((1, tm, w), row), pl.BlockSpec((1, tm, w), row),
                  pl.BlockSpec((1, w), lambda b, t: (0, 0)), pl.BlockSpec(wg.shape, lambda b, t: (0, 0)),
                  pl.BlockSpec((1, 1, d), lambda b, t: (b, 0, 0))],
        out_specs=pl.BlockSpec((1, tm, d), row),
        out_shape=jax.ShapeDtypeStruct(x.shape, F32),
        compiler_params=_cparams("parallel", "parallel"),
        name="s5_out",
    )(x, u, y, d_skip.reshape(1, w), wg, gate)


def _router_kernel(x_ref, g_ref, sc_ref, sh_ref, w_ref, t_ref, idx_ref, wt_ref):
    t = _norm_mod(x_ref[0], g_ref[...], sc_ref[0], sh_ref[0]).astype(BF16)
    t_ref[0] = t
    logits = _dot(t, w_ref[...])
    lane = lax.broadcasted_iota(jnp.int32, logits.shape, 1)
    big = jnp.int32(LANES)
    l1 = jnp.where(lane < N_EXPERTS, logits, -jnp.inf)
    v1 = jnp.max(l1, axis=-1, keepdims=True)
    i1 = jnp.min(jnp.where(l1 == v1, lane, big), axis=-1, keepdims=True)
    l2 = jnp.where(lane == i1, -jnp.inf, l1)
    v2 = jnp.max(l2, axis=-1, keepdims=True)
    i2 = jnp.min(jnp.where(l2 == v2, lane, big), axis=-1, keepdims=True)
    e2 = jnp.exp(v2 - v1)
    w1 = 1.0 / (1.0 + e2)
    w2 = e2 / (1.0 + e2)
    idx_ref[0] = jnp.where(lane == 0, i1, jnp.where(lane == 1, i2, 0))
    wt_ref[0] = jnp.where(lane == 0, w1, jnp.where(lane == 1, w2, 0.0))


def _router(x, norm_g, scale, shift, w_router):
    nb, seq, d = x.shape
    tm = 256
    wr = jnp.zeros((d, LANES), F32).at[:, :N_EXPERTS].set(w_router).astype(BF16)
    row = lambda b, t: (b, t, 0)
    mod = pl.BlockSpec((1, 1, d), lambda b, t: (b, 0, 0))
    g2 = norm_g.reshape(1, d)
    return pl.pallas_call(
        _router_kernel,
        grid=(nb, seq // tm),
        in_specs=[pl.BlockSpec((1, tm, d), row), pl.BlockSpec((1, d), lambda b, t: (0, 0)), mod, mod,
                  pl.BlockSpec(wr.shape, lambda b, t: (0, 0))],
        out_specs=[pl.BlockSpec((1, tm, d), row), pl.BlockSpec((1, tm, LANES), row),
                   pl.BlockSpec((1, tm, LANES), row)],
        out_shape=[jax.ShapeDtypeStruct((nb, seq, d), BF16), jax.ShapeDtypeStruct((nb, seq, LANES), jnp.int32),
                   jax.ShapeDtypeStruct((nb, seq, LANES), F32)],
        compiler_params=_cparams("parallel", "parallel"),
        name="router",
    )(x, g2, scale, shift, wr)


def _expert_kernel(te_ref, tv_ref, x_ref, rw_ref, wg_ref, wu_ref, wd_ref, o_ref, acc_scr):
    i = pl.program_id(0)
    f = pl.program_id(1)
    nf = pl.num_programs(1)

    @pl.when(tv_ref[i] > 0)
    def _():
        @pl.when(f == 0)
        def _():
            acc_scr[...] = jnp.zeros_like(acc_scr)

        x = x_ref[...]
        a = _dot(x, wg_ref[0])
        u = _dot(x, wu_ref[0])
        acc_scr[...] += _dot((a * jax.nn.sigmoid(a) * u).astype(BF16), wd_ref[0])

        @pl.when(f == nf - 1)
        def _():
            o_ref[...] = rw_ref[...] * acc_scr[...]


def _experts(xs, row_w, tile_expert, tile_valid, w_gate, w_up, w_down, tm, tf):
    n_rows, d = xs.shape
    dff = w_gate.shape[2]
    nf = dff // tf

    def fsel(i, f, te, tv):
        return jnp.where(tv[i] > 0, f, nf - 1)

    grid_spec = pltpu.PrefetchScalarGridSpec(
        num_scalar_prefetch=2,
        grid=(n_rows // tm, nf),
        in_specs=[pl.BlockSpec((tm, d), lambda i, f, te, tv: (i, 0)),
                  pl.BlockSpec((tm, 1), lambda i, f, te, tv: (i, 0)),
                  pl.BlockSpec((1, d, tf), lambda i, f, te, tv: (te[i], 0, fsel(i, f, te, tv))),
                  pl.BlockSpec((1, d, tf), lambda i, f, te, tv: (te[i], 0, fsel(i, f, te, tv))),
                  pl.BlockSpec((1, tf, d), lambda i, f, te, tv: (te[i], fsel(i, f, te, tv), 0))],
        out_specs=pl.BlockSpec((tm, d), lambda i, f, te, tv: (i, 0)),
        scratch_shapes=[pltpu.VMEM((tm, d), F32)],
    )
    return pl.pallas_call(
        _expert_kernel,
        grid_spec=grid_spec,
        out_shape=jax.ShapeDtypeStruct((n_rows, d), F32),
        compiler_params=_cparams("arbitrary", "arbitrary"),
        name="experts",
    )(tile_expert, tile_valid, xs, row_w, w_gate, w_up, w_down)


def _final_kernel(x_ref, y0_ref, y1_ref, gate_ref, g_ref, o_ref):
    x = x_ref[0] + gate_ref[0] * (y0_ref[0] + y1_ref[0])
    o_ref[0] = _rms(x) * g_ref[...]


def _final(x, y0, y1, gate, final_g):
    nb, seq, d = x.shape
    tm = 512
    row = lambda b, t: (b, t, 0)
    return pl.pallas_call(
        _final_kernel,
        grid=(nb, seq // tm),
        in_specs=[pl.BlockSpec((1, tm, d), row)] * 3 +
                 [pl.BlockSpec((1, 1, d), lambda b, t: (b, 0, 0)), pl.BlockSpec((1, d), lambda b, t: (0, 0))],
        out_specs=pl.BlockSpec((1, tm, d), row),
        out_shape=jax.ShapeDtypeStruct(x.shape, F32),
        compiler_params=_cparams("parallel", "parallel"),
        name="combine_final_norm",
    )(x, y0, y1, gate, final_g.reshape(1, d))


def _moe(x, norm_g, scale, shift, gate, w_router, w_gate, w_up, w_down, final_g):
    nb, seq, d = x.shape
    n_tok = nb * seq
    tm, tf = 512, 512
    t, idx, wt = _router(x, norm_g, scale, shift, w_router)
    t = t.reshape(n_tok, d)
    e_flat = idx[:, :, :TOP_K].reshape(n_tok * TOP_K)
    w_flat = wt[:, :, :TOP_K].reshape(n_tok * TOP_K)
    order = jnp.argsort(e_flat, stable=True)
    counts = jnp.bincount(e_flat, length=N_EXPERTS)
    padded = ((counts + tm - 1) // tm) * tm
    start = jnp.cumsum(counts) - counts
    pstart = jnp.cumsum(padded) - padded
    e_sorted = e_flat[order]
    dest_sorted = pstart[e_sorted] + (jnp.arange(n_tok * TOP_K) - start[e_sorted])
    n_rows = n_tok * TOP_K + N_EXPERTS * tm
    src_tok = jnp.zeros((n_rows,), jnp.int32).at[dest_sorted].set((order // TOP_K).astype(jnp.int32))
    row_w = jnp.zeros((n_rows,), F32).at[dest_sorted].set(w_flat[order])
    dest = jnp.zeros((n_tok * TOP_K,), jnp.int32).at[order].set(dest_sorted.astype(jnp.int32))
    n_tiles = n_rows // tm
    tile_start = jnp.arange(n_tiles) * tm
    pend = jnp.cumsum(padded)
    tile_expert = jnp.minimum(jnp.searchsorted(pend, tile_start, side='right'), N_EXPERTS - 1).astype(jnp.int32)
    tile_valid = (tile_start < pend[-1]).astype(jnp.int32)
    last_e = tile_expert[jnp.maximum(pend[-1] // tm - 1, 0)]
    tile_expert = jnp.where(tile_valid > 0, tile_expert, last_e)

    xs = t[src_tok]
    ys = _experts(xs, row_w.reshape(n_rows, 1), tile_expert, tile_valid,
                  w_gate.astype(BF16), w_up.astype(BF16), w_down.astype(BF16), tm, tf)
    yp = ys[dest].reshape(nb, seq, TOP_K, d)
    return _final(x, yp[:, :, 0], yp[:, :, 1], gate, final_g)


def kernel(x, c, ctx, c_ctx, mod_w, mod_b, norm1_g, norm2_g, ev_w_in, ev_q_norm_g, ev_w_qb, ev_kv_norm_g, ev_w_kvb, ev_na_rpb, ev_w_out, ev_ffn_w_gate, ev_ffn_w_up, ev_ffn_w_down, od_w_in, od_a_re, od_a_im, od_log_step, od_b_re, od_b_im, od_c_re, od_c_im, od_d, od_w_glu, moe_w_router, moe_w_gate, moe_w_up, moe_w_down, final_g):
    nb, seq, d = x.shape
    ctx_len = ctx.shape[1]
    assert mod_w.shape[0] == 2 and nb < 8
    assert seq % (NA_ROWS_PER_BLOCK * GRID_W) == 0 and ctx_len == NA_ROWS_PER_BLOCK * GRID_W

    cond = jnp.zeros((8, d), F32).at[:nb].set(c).at[nb].set(c_ctx)
    mods = _adaln(cond, mod_w, mod_b)

    def mod_parts(layer):
        return [mods[layer, :, i * d:(i + 1) * d].reshape(8, 1, d) for i in range(N_MOD)]

    xu = jnp.concatenate([x, ctx], axis=1)

    sh1, sc1, g1, sh2, sc2, g2 = mod_parts(0)
    cos, sin = _rope_tables(seq, ctx_len)
    weights = _even_weights(ev_w_in[0], ev_w_qb[0], ev_w_kvb[0])
    q, k, v, nq, nk, nv = _even_project(xu, norm1_g[0], sc1, sh1, weights, ev_q_norm_g[0], ev_kv_norm_g[0],
                                        cos, sin, seq)
    mla = _mla_attention(q, k, v, seq, ctx_len)
    bias = _na_bias_tables(ev_na_rpb[0], seq // GRID_W)
    na = _na_attention(nq, nk, nv, bias, seq, ctx_len)
    xu = _out_proj(xu, mla, na, ev_w_out[0], g1, seq)
    xu = _ffn(xu, norm2_g[0], sc2, sh2, g2, ev_ffn_w_gate[0], ev_ffn_w_up[0], ev_ffn_w_down[0], seq)

    sh1, sc1, g1, sh2, sc2, g2 = mod_parts(1)
    u = _s5_in(xu, norm1_g[1], sc1, sh1, od_w_in[0], seq)
    lt = seq + ctx_len
    n_chunks = lt // S5_CHUNK
    n_groups = u.shape[2] // S5_GROUP
    u_t = u.astype(BF16).reshape(nb, n_chunks, S5_CHUNK, n_groups, S5_GROUP)
    u_t = u_t.transpose(3, 0, 1, 2, 4).reshape(n_groups, nb * n_chunks, S5_CHUNK * S5_GROUP)
    ops = _s5_operators(od_a_re[0], od_a_im[0], od_log_step[0], od_b_re[0], od_b_im[0], od_c_re[0], od_c_im[0])
    y_t = _s5_core(u_t, ops, nb, n_chunks, ctx_len // S5_CHUNK)
    y = y_t.reshape(n_groups, nb, n_chunks, S5_CHUNK, S5_GROUP)[:, :, :seq // S5_CHUNK]
    y = y.transpose(1, 2, 3, 0, 4).reshape(nb, seq, n_groups * S5_GROUP)
    xl = xu[:, :seq]
    xl = _s5_out(xl, u, y, od_d[0], od_w_glu[0], g1[:nb])
    return _moe(xl, norm2_g[1], sc2[:nb], sh2[:nb], g2[:nb], moe_w_router[0], moe_w_gate[0], moe_w_up[0],
                moe_w_down[0], final_g)
```

```python
import functools
import math

import numpy as np
import jax
import jax.numpy as jnp
from jax import lax
from jax.experimental import pallas as pl
from jax.experimental.pallas import tpu as pltpu

F32 = jnp.float32
BF16 = jnp.bfloat16

LANES = 128
VMEM_LIMIT_BYTES = 52 * 1024 * 1024

NORM_EPS = 1e-6
ROPE_BASE = 10000.0
GRID_W = 64
N_MOD = 6

MLA_HEADS = 8
MLA_NOPE = 64
MLA_ROPE = 32
MLA_V = 64
Q_LORA = 384
KV_LORA = 256
MLA_SCALE = (MLA_NOPE + MLA_ROPE) ** -0.5
MLA_EXP2_SCALE = MLA_SCALE * math.log2(math.e)

NA_HEADS = 8
NA_HEAD_DIM = 64
NA_WIN_H = 8
NA_WIN_W = 16
NA_SCALE = NA_HEAD_DIM ** -0.5
NA_ROWS_PER_BLOCK = 4
NA_KEY_ROWS = NA_ROWS_PER_BLOCK + NA_WIN_H - 1

S5_GROUP = 16
S5_STATE = 64
S5_CHUNK = 16

N_EXPERTS = 8
TOP_K = 2

NEG_BIG = -1e30


def _cparams(*sem):
    return pltpu.CompilerParams(dimension_semantics=sem, vmem_limit_bytes=VMEM_LIMIT_BYTES)


def _rms(x):
    return x * lax.rsqrt(jnp.mean(x * x, axis=-1, keepdims=True) + NORM_EPS)


def _norm_mod(x, g, scale, shift):
    return (_rms(x) * g) * (1 + scale) + shift


def _dot(a, b):
    return jnp.dot(a, b, preferred_element_type=F32)


def _dot_nt(a, b):
    return lax.dot_general(a, b, (((1,), (1,)), ((), ())), preferred_element_type=F32)


def _adaln_kernel(c_ref, w_ref, b_ref, o_ref):
    c = c_ref[...]
    s = (c * jax.nn.sigmoid(c)).astype(BF16)
    o_ref[0] = _dot(s, w_ref[0].astype(BF16)) + b_ref[0]


def _adaln(cond, mod_w, mod_b):
    nl, d, n = mod_w.shape
    tn = 1536
    return pl.pallas_call(
        _adaln_kernel,
        grid=(nl, n // tn),
        in_specs=[pl.BlockSpec((8, d), lambda l, j: (0, 0)),
                  pl.BlockSpec((1, d, tn), lambda l, j: (l, 0, j)),
                  pl.BlockSpec((1, 1, tn), lambda l, j: (l, 0, j))],
        out_specs=pl.BlockSpec((1, 8, tn), lambda l, j: (l, 0, j)),
        out_shape=jax.ShapeDtypeStruct((nl, 8, n), F32),
        compiler_params=_cparams("parallel", "parallel"),
        name="adaln",
    )(cond, mod_w, mod_b.reshape(nl, 1, n))


def _even_proj_kernel(x_ref, g_ref, sc_ref, sh_ref, win_ref, qg_ref, kvg_ref, wq_ref, wqr_ref, wkk_ref,
                      wkv_ref, cos_ref, sin_ref, q_ref, k_ref, v_ref, nq_ref, nk_ref, nv_ref):
    h = _norm_mod(x_ref[0], g_ref[...], sc_ref[0], sh_ref[0]).astype(BF16)
    p = _dot(h, win_ref[...])
    c0, c1, c2, c3 = Q_LORA, Q_LORA + KV_LORA, Q_LORA + KV_LORA + LANES, Q_LORA + KV_LORA + 2 * LANES
    cqn = (_rms(p[:, :c0]) * qg_ref[...]).astype(BF16)
    ckvn = (_rms(p[:, c0:c1]) * kvg_ref[...]).astype(BF16)
    cos = cos_ref[...]
    sin = sin_ref[...]
    kr = p[:, c1:c2] * cos + p[:, c2:c3] * sin
    qa = _dot(cqn, wq_ref[...])
    qb = _dot(cqn, wqr_ref[...])
    kk = _dot(ckvn, wkk_ref[...])
    for hd in range(MLA_HEADS):
        sl = slice(hd * LANES, (hd + 1) * LANES)
        q_ref[0, :, sl] = (qa[:, sl] * cos + qb[:, sl] * sin).astype(BF16)
        k_ref[0, :, sl] = (kk[:, sl] + kr).astype(BF16)
    v_ref[0] = _dot(ckvn, wkv_ref[...]).astype(BF16)
    w = NA_HEADS * NA_HEAD_DIM
    nq_ref[0] = (p[:, c3:c3 + w] * NA_SCALE).astype(BF16)
    nk_ref[0] = p[:, c3 + w:c3 + 2 * w].astype(BF16)
    nv_ref[0] = p[:, c3 + 2 * w:c3 + 3 * w].astype(BF16)


def _rot_half_cols(w):
    q = MLA_ROPE // 4
    return jnp.concatenate([-w[:, q:2 * q], w[:, :q], -w[:, 3 * q:], w[:, 2 * q:3 * q]], axis=1)


def _rope_tables(seq, ctx_len):
    q = MLA_ROPE // 4
    t = np.arange(seq)
    inv_freq = ROPE_BASE ** (-jnp.arange(q, dtype=F32) / q)
    ang_r = jnp.asarray(t // GRID_W, F32)[:, None] * inv_freq[None, :]
    ang_c = jnp.asarray(t % GRID_W, F32)[:, None] * inv_freq[None, :]
    cos32 = jnp.concatenate([jnp.cos(ang_r)] * 2 + [jnp.cos(ang_c)] * 2, axis=1)
    sin32 = jnp.concatenate([jnp.sin(ang_r)] * 2 + [jnp.sin(ang_c)] * 2, axis=1)
    cos = jnp.ones((seq + ctx_len, LANES), F32).at[:seq, MLA_NOPE:MLA_NOPE + MLA_ROPE].set(cos32)
    sin = jnp.zeros((seq + ctx_len, LANES), F32).at[:seq, MLA_NOPE:MLA_NOPE + MLA_ROPE].set(sin32)
    return cos, sin


def _even_weights(w_in, w_qb, w_kvb):
    d = w_in.shape[0]
    c1 = Q_LORA + KV_LORA
    wkr = w_in[:, c1:c1 + MLA_ROPE]
    pad = lambda w: jnp.zeros((d, LANES), F32).at[:, MLA_NOPE:MLA_NOPE + MLA_ROPE].set(w)
    win = jnp.concatenate([w_in[:, :c1], pad(wkr), pad(_rot_half_cols(wkr)), w_in[:, c1 + MLA_ROPE:]], axis=1)
    hq = MLA_NOPE + MLA_ROPE
    wq = w_qb.reshape(Q_LORA, MLA_HEADS, hq)
    zq = jnp.zeros((Q_LORA, MLA_HEADS, LANES - hq), F32)
    wq_main = jnp.concatenate([wq, zq], axis=2).reshape(Q_LORA, MLA_HEADS * LANES)
    rot = jnp.stack([_rot_half_cols(wq[:, h, MLA_NOPE:]) for h in range(MLA_HEADS)], axis=1)
    wq_rot = jnp.concatenate([jnp.zeros((Q_LORA, MLA_HEADS, MLA_NOPE), F32), rot, zq], axis=2)
    wq_rot = wq_rot.reshape(Q_LORA, MLA_HEADS * LANES)
    wkv = w_kvb.reshape(KV_LORA, MLA_HEADS, MLA_NOPE + MLA_V)
    wkk = jnp.concatenate([wkv[:, :, :MLA_NOPE], jnp.zeros((KV_LORA, MLA_HEADS, LANES - MLA_NOPE), F32)], axis=2)
    wkk = wkk.reshape(KV_LORA, MLA_HEADS * LANES)
    wv = wkv[:, :, MLA_NOPE:].reshape(KV_LORA, MLA_HEADS * MLA_V)
    return tuple(a.astype(BF16) for a in (win, wq_main, wq_rot, wkk, wv))


def _mod_spec(n_lat_tiles, nb, d):
    return pl.BlockSpec((1, 1, d), lambda b, t: (jnp.where(t < n_lat_tiles, b, nb), 0, 0))


def _even_project(xu, norm_g, scale, shift, weights, q_norm_g, kv_norm_g, cos, sin, seq):
    nb, lt, d = xu.shape
    tm = 256
    win, wq, wqr, wkk, wv = weights
    n_lat = seq // tm
    row = lambda b, t: (b, t, 0)
    full = lambda a: pl.BlockSpec(a.shape, lambda b, t: (0,) * a.ndim)
    mod = _mod_spec(n_lat, nb, d)
    g2 = norm_g.reshape(1, d)
    qg2 = q_norm_g.reshape(1, Q_LORA)
    kvg2 = kv_norm_g.reshape(1, KV_LORA)
    wide = MLA_HEADS * LANES
    half = MLA_HEADS * MLA_V
    outs = [jax.ShapeDtypeStruct((nb, lt, wide), BF16), jax.ShapeDtypeStruct((nb, lt, wide), BF16)] + \
           [jax.ShapeDtypeStruct((nb, lt, half), BF16)] * 4
    return pl.pallas_call(
        _even_proj_kernel,
        grid=(nb, lt // tm),
        in_specs=[pl.BlockSpec((1, tm, d), row), full(g2), mod, mod, full(win), full(qg2), full(kvg2),
                  full(wq), full(wqr), full(wkk), full(wv),
                  pl.BlockSpec((tm, LANES), lambda b, t: (t, 0)), pl.BlockSpec((tm, LANES), lambda b, t: (t, 0))],
        out_specs=[pl.BlockSpec((1, tm, wide), row), pl.BlockSpec((1, tm, wide), row)] +
                  [pl.BlockSpec((1, tm, half), row)] * 4,
        out_shape=outs,
        compiler_params=_cparams("parallel", "parallel"),
        name="even_project",
    )(xu, g2, scale, shift, win, qg2, kvg2, wq, wqr, wkk, wv, cos, sin)


def _mla_pair_step(qs, kcs, vc, carry):
    out = []
    for q, kc, (m, l, acc) in zip(qs, kcs, carry):
        s = _dot_nt(q, kc)
        m_new = jnp.maximum(m, jnp.max(s, axis=-1, keepdims=True))
        alpha = jnp.exp2((m - m_new) * MLA_EXP2_SCALE)
        p = jnp.exp2((s - m_new) * MLA_EXP2_SCALE)
        l = alpha * l + jnp.sum(p, axis=-1, keepdims=True)
        acc = alpha * acc + _dot(p.astype(BF16), vc)
        out.append((m_new, l, acc))
    return tuple(out)


def _mla_kernel(q_ref, k_ref, v_ref, o_ref, *, seq, ctx_len, tk):
    tq = q_ref.shape[1]
    n_lat_tiles = seq // tq
    t = pl.program_id(1)
    lane = lax.broadcasted_iota(jnp.int32, (tq, LANES), 1)

    def run(with_latent):
        for pair in range(MLA_HEADS // 2):
            vsl = slice(pair * LANES, (pair + 1) * LANES)
            ksls = [slice((2 * pair + sub) * LANES, (2 * pair + sub + 1) * LANES) for sub in range(2)]
            qs = [q_ref[0, :, ksl] for ksl in ksls]
            init = (jnp.full((tq, 1), NEG_BIG, F32), jnp.zeros((tq, 1), F32), jnp.zeros((tq, LANES), F32))
            carry = (init, init)
            if with_latent:
                def body(i, c):
                    off = pl.multiple_of(i * tk, tk)
                    return _mla_pair_step(qs, [k_ref[0, pl.ds(off, tk), ksl] for ksl in ksls],
                                          v_ref[0, pl.ds(off, tk), vsl], c)
                carry = lax.fori_loop(0, seq // tk, body, carry)
            carry = _mla_pair_step(qs, [k_ref[0, seq:seq + ctx_len, ksl] for ksl in ksls],
                                   v_ref[0, seq:seq + ctx_len, vsl], carry)
            outs = [acc / l for (_, l, acc) in carry]
            o_ref[0, :, vsl] = jnp.where(lane < MLA_V, outs[0], outs[1]).astype(BF16)

    @pl.when(t < n_lat_tiles)
    def _():
        run(True)

    @pl.when(t >= n_lat_tiles)
    def _():
        run(False)


def _mla_attention(q, k, v, seq, ctx_len):
    nb, lt, wide = q.shape
    half = v.shape[2]
    tq = 256
    kern = functools.partial(_mla_kernel, seq=seq, ctx_len=ctx_len, tk=512)
    return pl.pallas_call(
        kern,
        grid=(nb, lt // tq),
        in_specs=[pl.BlockSpec((1, tq, wide), lambda b, t: (b, t, 0)),
                  pl.BlockSpec((1, lt, wide), lambda b, t: (b, 0, 0)),
                  pl.BlockSpec((1, lt, half), lambda b, t: (b, 0, 0))],
        out_specs=pl.BlockSpec((1, tq, half), lambda b, t: (b, t, 0)),
        out_shape=jax.ShapeDtypeStruct((nb, lt, half), BF16),
        compiler_params=_cparams("parallel", "parallel"),
        name="mla_attention",
    )(q, k, v)


def _na_bias_tables(rpb, rows):
    kh = min(NA_WIN_H, rows)
    last_r0 = rows - NA_ROWS_PER_BLOCK
    tabs = []
    for r0 in (0, 2 * NA_ROWS_PER_BLOCK, last_r0):
        kstart = int(np.clip(r0 - kh // 2, 0, rows - NA_KEY_ROWS))
        r = r0 + np.arange(NA_ROWS_PER_BLOCK)
        rs = np.clip(r - kh // 2, 0, rows - kh)
        kr = kstart + np.arange(NA_KEY_ROWS)
        row_ok = (kr[None, :] >= rs[:, None]) & (kr[None, :] < rs[:, None] + kh)
        row_off = np.clip(kr[None, :] - r[:, None] + (NA_WIN_H - 1), 0, 2 * NA_WIN_H - 2)
        c = np.arange(GRID_W)
        cs = np.clip(c - NA_WIN_W // 2, 0, GRID_W - NA_WIN_W)
        col_ok = (c[None, :] >= cs[:, None]) & (c[None, :] < cs[:, None] + NA_WIN_W)
        col_off = np.clip(c[None, :] - c[:, None] + (NA_WIN_W - 1), 0, 2 * NA_WIN_W - 2)
        b = rpb[:, row_off][..., col_off]
        ok = row_ok[:, :, None, None] & col_ok[None, None, :, :]
        b = jnp.where(jnp.asarray(ok)[None], b, NEG_BIG).transpose(0, 1, 3, 2, 4)
        tabs.append(b.reshape(rpb.shape[0], NA_ROWS_PER_BLOCK * GRID_W, NA_KEY_ROWS * GRID_W))
    return jnp.stack(tabs)


def _na_kernel(q_ref, k_ref, v_ref, bias_ref, o_ref, *, seq, ctx_len):
    tq = q_ref.shape[1]
    rows = seq // GRID_W
    n_lat_tiles = seq // tq
    nkeys = NA_KEY_ROWS * GRID_W
    t = pl.program_id(1)
    lane = lax.broadcasted_iota(jnp.int32, (tq, LANES), 1)
    kstart = jnp.clip(t * NA_ROWS_PER_BLOCK - NA_WIN_H // 2, 0, rows - NA_KEY_ROWS)
    koff = pl.multiple_of(kstart * GRID_W, GRID_W)

    def run(with_window):
        for pair in range(NA_HEADS // 2):
            sl = slice(pair * LANES, (pair + 1) * LANES)
            q2 = q_ref[0, :, sl].astype(F32)
            kc = k_ref[0, seq:seq + ctx_len, sl]
            vc = v_ref[0, seq:seq + ctx_len, sl]
            if with_window:
                kw = k_ref[0, pl.ds(koff, nkeys), sl]
                vw = v_ref[0, pl.ds(koff, nkeys), sl]
            outs = []
            for sub in range(2):
                keep = (lane < NA_HEAD_DIM) if sub == 0 else (lane >= NA_HEAD_DIM)
                q = jnp.where(keep, q2, 0.0).astype(BF16)
                s_c = _dot_nt(q, kc)
                m = jnp.max(s_c, axis=-1, keepdims=True)
                if with_window:
                    s_w = _dot_nt(q, kw) + bias_ref[0, 2 * pair + sub]
                    m = jnp.maximum(m, jnp.max(s_w, axis=-1, keepdims=True))
                p_c = jnp.exp(s_c - m)
                l = jnp.sum(p_c, axis=-1, keepdims=True)
                acc = _dot(p_c.astype(BF16), vc)
                if with_window:
                    p_w = jnp.exp(s_w - m)
                    l = l + jnp.sum(p_w, axis=-1, keepdims=True)
                    acc = acc + _dot(p_w.astype(BF16), vw)
                outs.append(acc / l)
            o_ref[0, :, sl] = jnp.where(lane < NA_HEAD_DIM, outs[0], outs[1]).astype(BF16)

    @pl.when(t < n_lat_tiles)
    def _():
        run(True)

    @pl.when(t >= n_lat_tiles)
    def _():
        run(False)


def _na_attention(q, k, v, bias, seq, ctx_len):
    nb, lt, w = q.shape
    tq = NA_ROWS_PER_BLOCK * GRID_W
    n_lat = seq // tq
    kern = functools.partial(_na_kernel, seq=seq, ctx_len=ctx_len)

    def variant(b, t):
        return (jnp.where(t == 0, 0, jnp.where(t >= n_lat - 1, 2, 1)), 0, 0, 0)

    return pl.pallas_call(
        kern,
        grid=(nb, lt // tq),
        in_specs=[pl.BlockSpec((1, tq, w), lambda b, t: (b, t, 0)),
                  pl.BlockSpec((1, lt, w), lambda b, t: (b, 0, 0)),
                  pl.BlockSpec((1, lt, w), lambda b, t: (b, 0, 0)),
                  pl.BlockSpec((1,) + bias.shape[1:], variant)],
        out_specs=pl.BlockSpec((1, tq, w), lambda b, t: (b, t, 0)),
        out_shape=jax.ShapeDtypeStruct((nb, lt, w), BF16),
        compiler_params=_cparams("parallel", "parallel"),
        name="na_attention",
    )(q, k, v, bias)


def _out_proj_kernel(x_ref, a_ref, b_ref, wa_ref, wb_ref, gate_ref, o_ref):
    y = _dot(a_ref[0], wa_ref[...]) + _dot(b_ref[0], wb_ref[...])
    o_ref[0] = x_ref[0] + gate_ref[0] * y


def _out_proj(xu, a, b, w_out, gate, seq):
    nb, lt, d = xu.shape
    tm = 256
    ka = a.shape[2]
    wa = w_out[:ka].astype(BF16)
    wb = w_out[ka:].astype(BF16)
    row = lambda bb, t: (bb, t, 0)
    full = lambda arr: pl.BlockSpec(arr.shape, lambda bb, t: (0,) * arr.ndim)
    return pl.pallas_call(
        _out_proj_kernel,
        grid=(nb, lt // tm),
        in_specs=[pl.BlockSpec((1, tm, d), row), pl.BlockSpec((1, tm, ka), row),
                  pl.BlockSpec((1, tm, b.shape[2]), row), full(wa), full(wb), _mod_spec(seq // tm, nb, d)],
        out_specs=pl.BlockSpec((1, tm, d), row),
        out_shape=jax.ShapeDtypeStruct(xu.shape, F32),
        compiler_params=_cparams("parallel", "parallel"),
        name="out_proj",
    )(xu, a, b, wa, wb, gate)


def _ffn_kernel(x_ref, g_ref, sc_ref, sh_ref, gate_ref, wg_ref, wu_ref, wd_ref, o_ref, h_scr, acc_scr):
    f = pl.program_id(2)

    @pl.when(f == 0)
    def _():
        h_scr[...] = _norm_mod(x_ref[0], g_ref[...], sc_ref[0], sh_ref[0]).astype(BF16)
        acc_scr[...] = jnp.zeros_like(acc_scr)

    h = h_scr[...]
    a = _dot(h, wg_ref[...])
    u = _dot(h, wu_ref[...])
    acc_scr[...] += _dot((a * jax.nn.sigmoid(a) * u).astype(BF16), wd_ref[...])

    @pl.when(f == pl.num_programs(2) - 1)
    def _():
        o_ref[0] = x_ref[0] + gate_ref[0] * acc_scr[...]


def _ffn(xu, norm_g, scale, shift, gate, w_gate, w_up, w_down, seq):
    nb, lt, d = xu.shape
    dff = w_gate.shape[1]
    tm = 256
    tf = dff // 2 if (dff // 2) % LANES == 0 else dff
    row = lambda b, t, f: (b, t, 0)
    n_lat = seq // tm
    mod = pl.BlockSpec((1, 1, d), lambda b, t, f: (jnp.where(t < n_lat, b, nb), 0, 0))
    g2 = norm_g.reshape(1, d)
    return pl.pallas_call(
        _ffn_kernel,
        grid=(nb, lt // tm, dff // tf),
        in_specs=[pl.BlockSpec((1, tm, d), row), pl.BlockSpec((1, d), lambda b, t, f: (0, 0)), mod, mod, mod,
                  pl.BlockSpec((d, tf), lambda b, t, f: (0, f)), pl.BlockSpec((d, tf), lambda b, t, f: (0, f)),
                  pl.BlockSpec((tf, d), lambda b, t, f: (f, 0))],
        out_specs=pl.BlockSpec((1, tm, d), row),
        out_shape=jax.ShapeDtypeStruct(xu.shape, F32),
        scratch_shapes=[pltpu.VMEM((tm, d), BF16), pltpu.VMEM((tm, d), F32)],
        compiler_params=_cparams("parallel", "parallel", "arbitrary"),
        name="dense_swiglu",
    )(xu, g2, scale, shift, gate, w_gate.astype(BF16), w_up.astype(BF16), w_down.astype(BF16))


def _s5_in_kernel(x_ref, g_ref, sc_ref, sh_ref, w_ref, o_ref, ob_ref):
    h = _norm_mod(x_ref[0], g_ref[...], sc_ref[0], sh_ref[0]).astype(BF16)
    u = _dot(h, w_ref[...])
    o_ref[0] = u
    ob_ref[0] = u.astype(BF16)


def _s5_in(xu, norm_g, scale, shift, w_in, seq):
    nb, lt, d = xu.shape
    tm = 256
    w = w_in.astype(BF16)
    row = lambda b, t: (b, t, 0)
    mod = _mod_spec(seq // tm, nb, d)
    g2 = norm_g.reshape(1, d)
    return pl.pallas_call(
        _s5_in_kernel,
        grid=(nb, lt // tm),
        in_specs=[pl.BlockSpec((1, tm, d), row), pl.BlockSpec((1, d), lambda b, t: (0, 0)), mod, mod,
                  pl.BlockSpec(w.shape, lambda b, t: (0, 0))],
        out_specs=[pl.BlockSpec((1, tm, w.shape[1]), row)] * 2,
        out_shape=[jax.ShapeDtypeStruct((nb, lt, w.shape[1]), F32), jax.ShapeDtypeStruct((nb, lt, w.shape[1]), BF16)],
        compiler_params=_cparams("parallel", "parallel"),
        name="s5_in",
    )(xu, g2, scale, shift, w)


def _s5_operators(a_re, a_im, log_step, b_re, b_im, c_re, c_im):
    hp = lax.Precision.HIGHEST
    t_len = S5_CHUNK
    ops = []
    for direction in range(2):
        dt = jnp.exp(log_step[direction])[:, None]
        lre, lim = a_re[direction] * dt, a_im[direction] * dt
        decay = jnp.exp(lre)
        ab_re, ab_im = decay * jnp.cos(lim), decay * jnp.sin(lim)
        den = a_re[direction] ** 2 + a_im[direction] ** 2
        f_re = ((ab_re - 1) * a_re[direction] + ab_im * a_im[direction]) / den
        f_im = (ab_im * a_re[direction] - (ab_re - 1) * a_im[direction]) / den
        bb_re = f_re[..., None] * b_re[direction] - f_im[..., None] * b_im[direction]
        bb_im = f_re[..., None] * b_im[direction] + f_im[..., None] * b_re[direction]
        cr, ci = c_re[direction], c_im[direction]
        tau = jnp.arange(t_len + 1, dtype=F32)[:, None, None]
        pw = jnp.exp(tau * lre[None])
        pw_re, pw_im = pw * jnp.cos(tau * lim[None]), pw * jnp.sin(tau * lim[None])
        ab_b_re = pw_re[..., None] * bb_re[None] - pw_im[..., None] * bb_im[None]
        ab_b_im = pw_re[..., None] * bb_im[None] + pw_im[..., None] * bb_re[None]
        kk = (jnp.einsum('gip,tgpj->tgij', cr, ab_b_re, precision=hp)
              - jnp.einsum('gip,tgpj->tgij', ci, ab_b_im, precision=hp))
        ca_re = cr[None] * pw_re[:, :, None, :] - ci[None] * pw_im[:, :, None, :]
        ca_im = cr[None] * pw_im[:, :, None, :] + ci[None] * pw_re[:, :, None, :]
        s = np.arange(t_len)
        if direction == 0:
            lag = s[None, :] - s[:, None]
            z_pow = t_len - 1 - s
            c_pow = s + 1
        else:
            lag = s[:, None] - s[None, :]
            z_pow = s
            c_pow = t_len - s
        ok = lag >= 0
        m = jnp.where(jnp.asarray(ok)[:, :, None, None, None], kk[np.clip(lag, 0, t_len)], 0.0)
        g = m.shape[2]
        m = m.transpose(2, 0, 4, 1, 3).reshape(g, t_len * S5_GROUP, t_len * S5_GROUP)
        wz_re = ab_b_re[z_pow].transpose(1, 0, 3, 2).reshape(g, t_len * S5_GROUP, S5_STATE)
        wz_im = ab_b_im[z_pow].transpose(1, 0, 3, 2).reshape(g, t_len * S5_GROUP, S5_STATE)
        wc_re = ca_re[c_pow].transpose(1, 3, 0, 2).reshape(g, S5_STATE, t_len * S5_GROUP)
        wc_im = -ca_im[c_pow].transpose(1, 3, 0, 2).reshape(g, S5_STATE, t_len * S5_GROUP)
        a_t = jnp.stack([pw_re[t_len], pw_im[t_len]], axis=1)
        ops.append((m.astype(BF16), wz_re.astype(BF16), wz_im.astype(BF16),
                    wc_re.astype(BF16), wc_im.astype(BF16), a_t))
    return [jnp.stack([ops[0][i], ops[1][i]], axis=1) for i in range(6)]


def _s5_core_kernel(u_ref, m_ref, wzr_ref, wzi_ref, wcr_ref, wci_ref, at_ref, y_ref,
                    zr_scr, zi_scr, hr_scr, hi_scr, *, nb, n_chunks, n_ctx_chunks):
    u = u_ref[0]
    n_lat = n_chunks - n_ctx_chunks
    for direction in range(2):
        zr_scr[direction] = _dot(u, wzr_ref[0, direction])
        zi_scr[direction] = _dot(u, wzi_ref[0, direction])

    a_re = [at_ref[0, d, 0:1, :] for d in range(2)]
    a_im = [at_ref[0, d, 1:2, :] for d in range(2)]

    def step(k, carry):
        cf = jnp.where(k < n_ctx_chunks, n_lat + k, k - n_ctx_chunks)
        cb = n_chunks - 1 - k
        new = []
        for direction, c in ((0, cf), (1, cb)):
            for b in range(nb):
                sr, si = carry[direction * nb + b]
                r = b * n_chunks + c
                hr_scr[direction, pl.ds(r, 1), :] = sr
                hi_scr[direction, pl.ds(r, 1), :] = si
                zr = zr_scr[direction, pl.ds(r, 1), :]
                zi = zi_scr[direction, pl.ds(r, 1), :]
                new.append((a_re[direction] * sr - a_im[direction] * si + zr,
                            a_re[direction] * si + a_im[direction] * sr + zi))
        return tuple(new)

    zero = jnp.zeros((1, S5_STATE), F32)
    lax.fori_loop(0, n_chunks, step, tuple((zero, zero) for _ in range(2 * nb)))

    y = _dot(u, m_ref[0, 0]) + _dot(u, m_ref[0, 1])
    for direction in range(2):
        y = y + _dot(hr_scr[direction].astype(BF16), wcr_ref[0, direction])
        y = y + _dot(hi_scr[direction].astype(BF16), wci_ref[0, direction])
    y_ref[0] = y


def _s5_core(u_t, ops, nb, n_chunks, n_ctx_chunks):
    g, r, w = u_t.shape
    m, wzr, wzi, wcr, wci, a_t = ops
    kern = functools.partial(_s5_core_kernel, nb=nb, n_chunks=n_chunks, n_ctx_chunks=n_ctx_chunks)
    blk = lambda a: pl.BlockSpec((1,) + a.shape[1:], lambda i: (i,) + (0,) * (a.ndim - 1))
    return pl.pallas_call(
        kern,
        grid=(g,),
        in_specs=[blk(u_t), blk(m), blk(wzr), blk(wzi), blk(wcr), blk(wci), blk(a_t)],
        out_specs=pl.BlockSpec((1, r, w), lambda i: (i, 0, 0)),
        out_shape=jax.ShapeDtypeStruct((g, r, w), F32),
        scratch_shapes=[pltpu.VMEM((2, r, S5_STATE), F32)] * 4,
        compiler_params=_cparams("parallel"),
        name="s5_core",
    )(u_t, m, wzr, wzi, wcr, wci, a_t)


def _s5_out_kernel(x_ref, u_ref, y_ref, d_ref, w_ref, gate_ref, o_ref):
    d = x_ref.shape[2]
    y = u_ref[0] * d_ref[...] + y_ref[0]
    z = _dot(jax.nn.gelu(y).astype(BF16), w_ref[...])
    o_ref[0] = x_ref[0] + gate_ref[0] * (z[:, :d] * jax.nn.sigmoid(z[:, d:]))


def _s5_out(x, u, y, d_skip, w_glu, gate):
    nb, seq, d = x.shape
    tm = 256
    w = u.shape[2]
    row = lambda b, t: (b, t, 0)
    wg = w_glu.astype(BF16)
    return pl.pallas_call(
        _s5_out_kernel,
        grid=(nb, seq // tm),
        in_specs=[pl.BlockSpec((1, tm, d), row), pl.BlockSpec((1, tm, w), row), pl.BlockSpec((1, tm, w), row),
                  pl.BlockSpec((1, w), lambda b, t: (0, 0)), pl.BlockSpec(wg.shape, lambda b, t: (0, 0)),
                  pl.BlockSpec((1, 1, d), lambda b, t: (b, 0, 0))],
        out_specs=pl.BlockSpec((1, tm, d), row),
        out_shape=jax.ShapeDtypeStruct(x.shape, F32),
        compiler_params=_cparams("parallel", "parallel"),
        name="s5_out",
    )(x, u, y, d_skip.reshape(1, w), wg, gate)


R_IDX, R_WT, R_RANK = 0, TOP_K, 2 * TOP_K


def _router_kernel(x_ref, g_ref, sc_ref, sh_ref, w_ref, tri_ref, rec_ref, cnt_ref, base_scr):
    @pl.when((pl.program_id(0) == 0) & (pl.program_id(1) == 0))
    def _():
        base_scr[...] = jnp.zeros_like(base_scr)

    t = _norm_mod(x_ref[0], g_ref[...], sc_ref[0], sh_ref[0]).astype(BF16)
    logits = _dot(t, w_ref[...])
    lane = lax.broadcasted_iota(jnp.int32, logits.shape, 1)
    big = jnp.int32(LANES)
    l1 = jnp.where(lane < N_EXPERTS, logits, -jnp.inf)
    v1 = jnp.max(l1, axis=-1, keepdims=True)
    i1 = jnp.min(jnp.where(l1 == v1, lane, big), axis=-1, keepdims=True)
    l2 = jnp.where(lane == i1, -jnp.inf, l1)
    v2 = jnp.max(l2, axis=-1, keepdims=True)
    i2 = jnp.min(jnp.where(l2 == v2, lane, big), axis=-1, keepdims=True)
    e2 = jnp.exp(v2 - v1)
    w1 = 1.0 / (1.0 + e2)
    w2 = e2 / (1.0 + e2)
    hit = jnp.where((lane == i1) | (lane == i2), 1.0, 0.0)
    before = _dot(tri_ref[...], hit.astype(BF16)) + base_scr[0:1, :]
    r1 = jnp.sum(jnp.where(lane == i1, before, 0.0), axis=-1, keepdims=True)
    r2 = jnp.sum(jnp.where(lane == i2, before, 0.0), axis=-1, keepdims=True)
    base_scr[...] = base_scr[...] + jnp.sum(hit, axis=0, keepdims=True)
    cnt_ref[...] = base_scr[...]
    rec = jnp.zeros(logits.shape, F32)
    for k, val in enumerate((i1.astype(F32), i2.astype(F32), w1, w2, r1, r2)):
        rec = jnp.where(lane == k, val, rec)
    rec_ref[0] = rec


def _router(x, norm_g, scale, shift, w_router):
    nb, seq, d = x.shape
    tm = 256
    wr = jnp.zeros((d, LANES), F32).at[:, :N_EXPERTS].set(w_router).astype(BF16)
    tri = jnp.asarray(np.tril(np.ones((tm, tm), np.float32), -1), BF16)
    row = lambda b, t: (b, t, 0)
    mod = pl.BlockSpec((1, 1, d), lambda b, t: (b, 0, 0))
    g2 = norm_g.reshape(1, d)
    return pl.pallas_call(
        _router_kernel,
        grid=(nb, seq // tm),
        in_specs=[pl.BlockSpec((1, tm, d), row), pl.BlockSpec((1, d), lambda b, t: (0, 0)), mod, mod,
                  pl.BlockSpec(wr.shape, lambda b, t: (0, 0)), pl.BlockSpec(tri.shape, lambda b, t: (0, 0))],
        out_specs=[pl.BlockSpec((1, tm, LANES), row), pl.BlockSpec((8, LANES), lambda b, t: (0, 0))],
        out_shape=[jax.ShapeDtypeStruct((nb, seq, LANES), F32), jax.ShapeDtypeStruct((8, LANES), F32)],
        scratch_shapes=[pltpu.VMEM((8, LANES), F32)],
        compiler_params=_cparams("arbitrary", "arbitrary"),
        name="router",
    )(x, g2, scale, shift, wr, tri)


def _row_copies_wait(src_ref, dst_ref, sem):
    pltpu.make_async_copy(src_ref, dst_ref, sem).wait()


def _dispatch_kernel(dest_ref, fill_ref, x_ref, g_ref, sc_ref, sh_ref, xs_ref, tbuf, zbuf, sem, zsem):
    tm = x_ref.shape[1]
    ztm = zbuf.shape[0]
    step = pl.program_id(0) * pl.num_programs(1) + pl.program_id(1)
    n_steps = pl.num_programs(0) * pl.num_programs(1)
    slot = step % 2

    def drain(s):
        for _ in range(TOP_K):
            _row_copies_wait(tbuf.at[s], xs_ref.at[pl.ds(0, tm)], sem.at[s])

    @pl.when(step == 0)
    def _():
        zbuf[...] = jnp.zeros_like(zbuf)

        def fill(start):
            def body(i, carry):
                @pl.when(fill_ref[i] > 0)
                def _():
                    cp = pltpu.make_async_copy(zbuf, xs_ref.at[pl.ds(pl.multiple_of(i * ztm, ztm), ztm)], zsem)
                    if start:
                        cp.start()
                    else:
                        cp.wait()
                return carry
            lax.fori_loop(0, fill_ref.shape[0], body, 0)

        fill(True)
        fill(False)

    @pl.when(step >= 2)
    def _():
        drain(slot)

    tbuf[slot] = _norm_mod(x_ref[0], g_ref[...], sc_ref[0], sh_ref[0])
    base = step * (tm * TOP_K)

    def body(r, carry):
        for k in range(TOP_K):
            dst = dest_ref[base + r * TOP_K + k]
            pltpu.make_async_copy(tbuf.at[slot, pl.ds(r, 1)], xs_ref.at[pl.ds(dst, 1)], sem.at[slot]).start()
        return carry

    lax.fori_loop(0, tm, body, 0, unroll=8)

    @pl.when(step == n_steps - 1)
    def _():
        drain(slot)

        @pl.when(n_steps > 1)
        def _():
            drain(1 - slot)


def _dispatch(x, norm_g, scale, shift, dest, tile_fill, n_rows, expert_tm):
    nb, seq, d = x.shape
    tm = 256
    row = lambda b, t, dr, fr: (b, t, 0)
    mod = pl.BlockSpec((1, 1, d), lambda b, t, dr, fr: (b, 0, 0))
    g2 = norm_g.reshape(1, d)
    grid_spec = pltpu.PrefetchScalarGridSpec(
        num_scalar_prefetch=2,
        grid=(nb, seq // tm),
        in_specs=[pl.BlockSpec((1, tm, d), row), pl.BlockSpec((1, d), lambda b, t, dr, fr: (0, 0)), mod, mod],
        out_specs=pl.BlockSpec(memory_space=pl.ANY),
        scratch_shapes=[pltpu.VMEM((2, tm, d), F32), pltpu.VMEM((expert_tm, d), F32),
                        pltpu.SemaphoreType.DMA((2,)), pltpu.SemaphoreType.DMA(())],
    )
    return pl.pallas_call(
        _dispatch_kernel,
        grid_spec=grid_spec,
        out_shape=jax.ShapeDtypeStruct((n_rows, d), F32),
        compiler_params=_cparams("arbitrary", "arbitrary"),
        name="dispatch",
    )(dest, tile_fill, x, g2, scale, shift)


def _expert_kernel(te_ref, tr_ref, x_ref, wg_ref, wu_ref, wd_ref, o_ref, xb_scr, acc_scr):
    i = pl.program_id(0)
    f = pl.program_id(1)
    nf = pl.num_programs(1)
    live = tr_ref[i] > 0

    @pl.when(jnp.logical_not(live) & (f == nf - 1))
    def _():
        o_ref[...] = jnp.zeros_like(o_ref)

    @pl.when(live)
    def _():
        @pl.when(f == 0)
        def _():
            xb_scr[...] = x_ref[...].astype(BF16)
            acc_scr[...] = jnp.zeros_like(acc_scr)

        x = xb_scr[...]
        a = _dot(x, wg_ref[0])
        u = _dot(x, wu_ref[0])
        acc_scr[...] += _dot((a * jax.nn.sigmoid(a) * u).astype(BF16), wd_ref[0])

        @pl.when(f == nf - 1)
        def _():
            o_ref[...] = acc_scr[...]


def _experts(xs, tile_expert, tile_rows, w_gate, w_up, w_down, tm, tf):
    n_rows, d = xs.shape
    dff = w_gate.shape[2]
    nf = dff // tf

    def fsel(i, f, te, tr):
        return jnp.where(tr[i] > 0, f, nf - 1)

    grid_spec = pltpu.PrefetchScalarGridSpec(
        num_scalar_prefetch=2,
        grid=(n_rows // tm, nf),
        in_specs=[pl.BlockSpec((tm, d), lambda i, f, te, tr: (i, 0)),
                  pl.BlockSpec((1, d, tf), lambda i, f, te, tr: (te[i], 0, fsel(i, f, te, tr))),
                  pl.BlockSpec((1, d, tf), lambda i, f, te, tr: (te[i], 0, fsel(i, f, te, tr))),
                  pl.BlockSpec((1, tf, d), lambda i, f, te, tr: (te[i], fsel(i, f, te, tr), 0))],
        out_specs=pl.BlockSpec((tm, d), lambda i, f, te, tr: (i, 0)),
        scratch_shapes=[pltpu.VMEM((tm, d), BF16), pltpu.VMEM((tm, d), F32)],
    )
    return pl.pallas_call(
        _expert_kernel,
        grid_spec=grid_spec,
        out_shape=jax.ShapeDtypeStruct((n_rows, d), F32),
        compiler_params=_cparams("arbitrary", "arbitrary"),
        name="experts",
    )(tile_expert, tile_rows, xs, w_gate, w_up, w_down)


def _combine_kernel(dest_ref, x_ref, rec_ref, gate_ref, g_ref, ys_ref, o_ref, ybuf, sem):
    tm = x_ref.shape[1]
    step = pl.program_id(0) * pl.num_programs(1) + pl.program_id(1)
    n_steps = pl.num_programs(0) * pl.num_programs(1)
    slot = step % 2

    def issue(st, sl):
        base = st * (tm * TOP_K)

        def body(r, carry):
            for k in range(TOP_K):
                src = dest_ref[base + r * TOP_K + k]
                pltpu.make_async_copy(ys_ref.at[pl.ds(src, 1)], ybuf.at[sl, k, pl.ds(r, 1)], sem.at[sl]).start()
            return carry

        lax.fori_loop(0, tm, body, 0, unroll=8)

    @pl.when(step == 0)
    def _():
        issue(step, slot)

    @pl.when(step + 1 < n_steps)
    def _():
        issue(step + 1, 1 - slot)

    for k in range(TOP_K):
        _row_copies_wait(ys_ref.at[pl.ds(0, tm)], ybuf.at[slot, k], sem.at[slot])

    rec = rec_ref[0]
    y = rec[:, R_WT:R_WT + 1] * ybuf[slot, 0] + rec[:, R_WT + 1:R_WT + 2] * ybuf[slot, 1]
    x = x_ref[0] + gate_ref[0] * y
    o_ref[0] = _rms(x) * g_ref[...]


def _combine(x, rec, gate, final_g, ys, dest):
    nb, seq, d = x.shape
    tm = 256
    row = lambda b, t, dr: (b, t, 0)
    grid_spec = pltpu.PrefetchScalarGridSpec(
        num_scalar_prefetch=1,
        grid=(nb, seq // tm),
        in_specs=[pl.BlockSpec((1, tm, d), row), pl.BlockSpec((1, tm, LANES), row),
                  pl.BlockSpec((1, 1, d), lambda b, t, dr: (b, 0, 0)), pl.BlockSpec((1, d), lambda b, t, dr: (0, 0)),
                  pl.BlockSpec(memory_space=pl.ANY)],
        out_specs=pl.BlockSpec((1, tm, d), row),
        scratch_shapes=[pltpu.VMEM((2, TOP_K, tm, d), F32), pltpu.SemaphoreType.DMA((2,))],
    )
    return pl.pallas_call(
        _combine_kernel,
        grid_spec=grid_spec,
        out_shape=jax.ShapeDtypeStruct(x.shape, F32),
        compiler_params=_cparams("arbitrary", "arbitrary"),
        name="combine_final_norm",
    )(dest, x, rec, gate, final_g.reshape(1, d), ys)


def _moe(x, norm_g, scale, shift, gate, w_router, w_gate, w_up, w_down, final_g):
    nb, seq, d = x.shape
    n_tok = nb * seq
    tm, tf = 512, 512
    rec, cnt = _router(x, norm_g, scale, shift, w_router)
    counts = cnt[0, :N_EXPERTS].astype(jnp.int32)
    padded = ((counts + tm - 1) // tm) * tm
    pend = jnp.cumsum(padded)
    pstart = pend - padded
    idx = rec[:, :, R_IDX:R_IDX + TOP_K].astype(jnp.int32)
    rank = rec[:, :, R_RANK:R_RANK + TOP_K].astype(jnp.int32)
    seg = jnp.sum(jnp.where(idx[..., None] == jnp.arange(N_EXPERTS), pstart, 0), axis=-1)
    dest = (seg + rank).reshape(n_tok * TOP_K)
    n_rows = n_tok * TOP_K + N_EXPERTS * tm
    tile_start = jnp.arange(n_rows // tm, dtype=jnp.int32) * tm
    tile_expert = jnp.sum((tile_start[:, None] >= pend[None, :]).astype(jnp.int32), axis=1)
    live = tile_expert < N_EXPERTS
    last_live = jnp.max(jnp.where(live, tile_expert, 0))
    tile_expert = jnp.where(live, tile_expert, last_live)
    seg_rows = jnp.sum(jnp.where(tile_expert[:, None] == jnp.arange(N_EXPERTS), (pstart + counts)[None, :], 0), axis=1)
    tile_rows = jnp.where(live, jnp.clip(seg_rows - tile_start, 0, tm), 0).astype(jnp.int32)
    tile_fill = (tile_rows < tm).astype(jnp.int32)

    xs = _dispatch(x, norm_g, scale, shift, dest, tile_fill, n_rows, tm)
    ys = _experts(xs, tile_expert.astype(jnp.int32), tile_rows,
                  w_gate.astype(BF16), w_up.astype(BF16), w_down.astype(BF16), tm, tf)
    return _combine(x, rec, gate, final_g, ys, dest)


def kernel(x, c, ctx, c_ctx, mod_w, mod_b, norm1_g, norm2_g, ev_w_in, ev_q_norm_g, ev_w_qb, ev_kv_norm_g, ev_w_kvb, ev_na_rpb, ev_w_out, ev_ffn_w_gate, ev_ffn_w_up, ev_ffn_w_down, od_w_in, od_a_re, od_a_im, od_log_step, od_b_re, od_b_im, od_c_re, od_c_im, od_d, od_w_glu, moe_w_router, moe_w_gate, moe_w_up, moe_w_down, final_g):
    nb, seq, d = x.shape
    ctx_len = ctx.shape[1]
    assert mod_w.shape[0] == 2 and nb < 8
    assert seq % (NA_ROWS_PER_BLOCK * GRID_W) == 0 and ctx_len == NA_ROWS_PER_BLOCK * GRID_W

    cond = jnp.zeros((8, d), F32).at[:nb].set(c).at[nb].set(c_ctx)
    mods = _adaln(cond, mod_w, mod_b)

    def mod_parts(layer):
        return [mods[layer, :, i * d:(i + 1) * d].reshape(8, 1, d) for i in range(N_MOD)]

    xu = jnp.concatenate([x, ctx], axis=1)

    sh1, sc1, g1, sh2, sc2, g2 = mod_parts(0)
    cos, sin = _rope_tables(seq, ctx_len)
    weights = _even_weights(ev_w_in[0], ev_w_qb[0], ev_w_kvb[0])
    q, k, v, nq, nk, nv = _even_project(xu, norm1_g[0], sc1, sh1, weights, ev_q_norm_g[0], ev_kv_norm_g[0],
                                        cos, sin, seq)
    mla = _mla_attention(q, k, v, seq, ctx_len)
    bias = _na_bias_tables(ev_na_rpb[0], seq // GRID_W)
    na = _na_attention(nq, nk, nv, bias, seq, ctx_len)
    xu = _out_proj(xu, mla, na, ev_w_out[0], g1, seq)
    xu = _ffn(xu, norm2_g[0], sc2, sh2, g2, ev_ffn_w_gate[0], ev_ffn_w_up[0], ev_ffn_w_down[0], seq)

    sh1, sc1, g1, sh2, sc2, g2 = mod_parts(1)
    u, ub = _s5_in(xu, norm1_g[1], sc1, sh1, od_w_in[0], seq)
    lt = seq + ctx_len
    n_chunks = lt // S5_CHUNK
    n_groups = u.shape[2] // S5_GROUP
    u_t = ub.reshape(nb, n_chunks, S5_CHUNK, n_groups, S5_GROUP)
    u_t = u_t.transpose(3, 0, 1, 2, 4).reshape(n_groups, nb * n_chunks, S5_CHUNK * S5_GROUP)
    ops = _s5_operators(od_a_re[0], od_a_im[0], od_log_step[0], od_b_re[0], od_b_im[0], od_c_re[0], od_c_im[0])
    y_t = _s5_core(u_t, ops, nb, n_chunks, ctx_len // S5_CHUNK)
    y = y_t.reshape(n_groups, nb, n_chunks, S5_CHUNK, S5_GROUP)[:, :, :seq // S5_CHUNK]
    y = y.transpose(1, 2, 3, 0, 4).reshape(nb, seq, n_groups * S5_GROUP)
    xl = xu[:, :seq]
    xl = _s5_out(xl, u, y, od_d[0], od_w_glu[0], g1[:nb])
    return _moe(xl, norm2_g[1], sc2[:nb], sh2[:nb], g2[:nb], moe_w_router[0], moe_w_gate[0], moe_w_up[0],
                moe_w_down[0], final_g)
```

```python
import functools
import math

import numpy as np
import jax
import jax.numpy as jnp
from jax import lax
from jax.experimental import pallas as pl
from jax.experimental.pallas import tpu as pltpu

F32 = jnp.float32
BF16 = jnp.bfloat16

LANES = 128
VMEM_LIMIT_BYTES = 52 * 1024 * 1024

NORM_EPS = 1e-6
ROPE_BASE = 10000.0
GRID_W = 64
N_MOD = 6

MLA_HEADS = 8
MLA_NOPE = 64
MLA_ROPE = 32
MLA_V = 64
Q_LORA = 384
KV_LORA = 256
MLA_SCALE = (MLA_NOPE + MLA_ROPE) ** -0.5
MLA_EXP2_SCALE = MLA_SCALE * math.log2(math.e)

NA_HEADS = 8
NA_HEAD_DIM = 64
NA_WIN_H = 8
NA_WIN_W = 16
NA_SCALE = NA_HEAD_DIM ** -0.5
NA_ROWS_PER_BLOCK = 4
NA_KEY_ROWS = NA_ROWS_PER_BLOCK + NA_WIN_H - 1

S5_GROUP = 16
S5_STATE = 64
S5_CHUNK = 16
S5_LANE_GROUPS = LANES // S5_GROUP

N_EXPERTS = 8
TOP_K = 2

NEG_BIG = -1e30


def _cparams(*sem):
    return pltpu.CompilerParams(dimension_semantics=sem, vmem_limit_bytes=VMEM_LIMIT_BYTES)


def _rms(x):
    return x * lax.rsqrt(jnp.mean(x * x, axis=-1, keepdims=True) + NORM_EPS)


def _norm_mod(x, g, scale, shift):
    return (_rms(x) * g) * (1 + scale) + shift


def _dot(a, b):
    return jnp.dot(a, b, preferred_element_type=F32)


def _dot_nt(a, b):
    return lax.dot_general(a, b, (((1,), (1,)), ((), ())), preferred_element_type=F32)


def _adaln_kernel(c_ref, w_ref, b_ref, o_ref):
    c = c_ref[...]
    s = (c * jax.nn.sigmoid(c)).astype(BF16)
    o_ref[0] = _dot(s, w_ref[0].astype(BF16)) + b_ref[0]


def _adaln(cond, mod_w, mod_b):
    nl, d, n = mod_w.shape
    tn = 1536
    return pl.pallas_call(
        _adaln_kernel,
        grid=(nl, n // tn),
        in_specs=[pl.BlockSpec((8, d), lambda l, j: (0, 0)),
                  pl.BlockSpec((1, d, tn), lambda l, j: (l, 0, j)),
                  pl.BlockSpec((1, 1, tn), lambda l, j: (l, 0, j))],
        out_specs=pl.BlockSpec((1, 8, tn), lambda l, j: (l, 0, j)),
        out_shape=jax.ShapeDtypeStruct((nl, 8, n), F32),
        compiler_params=_cparams("parallel", "parallel"),
        name="adaln",
    )(cond, mod_w, mod_b.reshape(nl, 1, n))


def _even_proj_kernel(x_ref, g_ref, sc_ref, sh_ref, win_ref, qg_ref, kvg_ref, wq_ref, wqr_ref, wkk_ref,
                      wkv_ref, cos_ref, sin_ref, q_ref, k_ref, v_ref, nq_ref, nk_ref, nv_ref):
    h = _norm_mod(x_ref[0], g_ref[...], sc_ref[0], sh_ref[0]).astype(BF16)
    p = _dot(h, win_ref[...])
    c0, c1, c2, c3 = Q_LORA, Q_LORA + KV_LORA, Q_LORA + KV_LORA + LANES, Q_LORA + KV_LORA + 2 * LANES
    cqn = (_rms(p[:, :c0]) * qg_ref[...]).astype(BF16)
    ckvn = (_rms(p[:, c0:c1]) * kvg_ref[...]).astype(BF16)
    cos = cos_ref[...]
    sin = sin_ref[...]
    kr = p[:, c1:c2] * cos + p[:, c2:c3] * sin
    qa = _dot(cqn, wq_ref[...])
    qb = _dot(cqn, wqr_ref[...])
    kk = _dot(ckvn, wkk_ref[...])
    for hd in range(MLA_HEADS):
        sl = slice(hd * LANES, (hd + 1) * LANES)
        q_ref[0, :, sl] = (qa[:, sl] * cos + qb[:, sl] * sin).astype(BF16)
        k_ref[0, :, sl] = (kk[:, sl] + kr).astype(BF16)
    v_ref[0] = _dot(ckvn, wkv_ref[...]).astype(BF16)
    w = NA_HEADS * NA_HEAD_DIM
    nq_ref[0] = (p[:, c3:c3 + w] * NA_SCALE).astype(BF16)
    nk_ref[0] = p[:, c3 + w:c3 + 2 * w].astype(BF16)
    nv_ref[0] = p[:, c3 + 2 * w:c3 + 3 * w].astype(BF16)


def _rot_half_cols(w):
    q = MLA_ROPE // 4
    return jnp.concatenate([-w[:, q:2 * q], w[:, :q], -w[:, 3 * q:], w[:, 2 * q:3 * q]], axis=1)


def _rope_tables(seq, ctx_len):
    q = MLA_ROPE // 4
    t = np.arange(seq)
    inv_freq = ROPE_BASE ** (-jnp.arange(q, dtype=F32) / q)
    ang_r = jnp.asarray(t // GRID_W, F32)[:, None] * inv_freq[None, :]
    ang_c = jnp.asarray(t % GRID_W, F32)[:, None] * inv_freq[None, :]
    cos32 = jnp.concatenate([jnp.cos(ang_r)] * 2 + [jnp.cos(ang_c)] * 2, axis=1)
    sin32 = jnp.concatenate([jnp.sin(ang_r)] * 2 + [jnp.sin(ang_c)] * 2, axis=1)
    cos = jnp.ones((seq + ctx_len, LANES), F32).at[:seq, MLA_NOPE:MLA_NOPE + MLA_ROPE].set(cos32)
    sin = jnp.zeros((seq + ctx_len, LANES), F32).at[:seq, MLA_NOPE:MLA_NOPE + MLA_ROPE].set(sin32)
    return cos, sin


def _even_weights(w_in, w_qb, w_kvb):
    d = w_in.shape[0]
    c1 = Q_LORA + KV_LORA
    wkr = w_in[:, c1:c1 + MLA_ROPE]
    pad = lambda w: jnp.zeros((d, LANES), F32).at[:, MLA_NOPE:MLA_NOPE + MLA_ROPE].set(w)
    win = jnp.concatenate([w_in[:, :c1], pad(wkr), pad(_rot_half_cols(wkr)), w_in[:, c1 + MLA_ROPE:]], axis=1)
    hq = MLA_NOPE + MLA_ROPE
    wq = w_qb.reshape(Q_LORA, MLA_HEADS, hq)
    zq = jnp.zeros((Q_LORA, MLA_HEADS, LANES - hq), F32)
    wq_main = jnp.concatenate([wq, zq], axis=2).reshape(Q_LORA, MLA_HEADS * LANES)
    rot = jnp.stack([_rot_half_cols(wq[:, h, MLA_NOPE:]) for h in range(MLA_HEADS)], axis=1)
    wq_rot = jnp.concatenate([jnp.zeros((Q_LORA, MLA_HEADS, MLA_NOPE), F32), rot, zq], axis=2)
    wq_rot = wq_rot.reshape(Q_LORA, MLA_HEADS * LANES)
    wkv = w_kvb.reshape(KV_LORA, MLA_HEADS, MLA_NOPE + MLA_V)
    wkk = jnp.concatenate([wkv[:, :, :MLA_NOPE], jnp.zeros((KV_LORA, MLA_HEADS, LANES - MLA_NOPE), F32)], axis=2)
    wkk = wkk.reshape(KV_LORA, MLA_HEADS * LANES)
    wv = wkv[:, :, MLA_NOPE:].reshape(KV_LORA, MLA_HEADS * MLA_V)
    return tuple(a.astype(BF16) for a in (win, wq_main, wq_rot, wkk, wv))


def _mod_spec(n_lat_tiles, nb, d):
    return pl.BlockSpec((1, 1, d), lambda b, t: (jnp.where(t < n_lat_tiles, b, nb), 0, 0))


def _even_project(xu, norm_g, scale, shift, weights, q_norm_g, kv_norm_g, cos, sin, seq):
    nb, lt, d = xu.shape
    tm = 256
    win, wq, wqr, wkk, wv = weights
    n_lat = seq // tm
    row = lambda b, t: (b, t, 0)
    full = lambda a: pl.BlockSpec(a.shape, lambda b, t: (0,) * a.ndim)
    mod = _mod_spec(n_lat, nb, d)
    g2 = norm_g.reshape(1, d)
    qg2 = q_norm_g.reshape(1, Q_LORA)
    kvg2 = kv_norm_g.reshape(1, KV_LORA)
    wide = MLA_HEADS * LANES
    half = MLA_HEADS * MLA_V
    outs = [jax.ShapeDtypeStruct((nb, lt, wide), BF16), jax.ShapeDtypeStruct((nb, lt, wide), BF16)] + \
           [jax.ShapeDtypeStruct((nb, lt, half), BF16)] * 4
    return pl.pallas_call(
        _even_proj_kernel,
        grid=(nb, lt // tm),
        in_specs=[pl.BlockSpec((1, tm, d), row), full(g2), mod, mod, full(win), full(qg2), full(kvg2),
                  full(wq), full(wqr), full(wkk), full(wv),
                  pl.BlockSpec((tm, LANES), lambda b, t: (t, 0)), pl.BlockSpec((tm, LANES), lambda b, t: (t, 0))],
        out_specs=[pl.BlockSpec((1, tm, wide), row), pl.BlockSpec((1, tm, wide), row)] +
                  [pl.BlockSpec((1, tm, half), row)] * 4,
        out_shape=outs,
        compiler_params=_cparams("parallel", "parallel"),
        name="even_project",
    )(xu, g2, scale, shift, win, qg2, kvg2, wq, wqr, wkk, wv, cos, sin)


def _mla_pair_update(scores, vc, carry):
    out = []
    for s, (m, l, acc) in zip(scores, carry):
        m_new = jnp.maximum(m, jnp.max(s, axis=-1, keepdims=True))
        alpha = jnp.exp2((m - m_new) * MLA_EXP2_SCALE)
        p = jnp.exp2((s - m_new) * MLA_EXP2_SCALE)
        l = alpha * l + jnp.sum(p, axis=-1, keepdims=True)
        acc = alpha * acc + _dot(p.astype(BF16), vc)
        out.append((m_new, l, acc))
    return tuple(out)


def _mla_kernel(q_ref, k_ref, v_ref, o_ref, *, seq, ctx_len, tk):
    tq = q_ref.shape[1]
    n_lat_tiles = seq // tq
    t = pl.program_id(1)
    lane = lax.broadcasted_iota(jnp.int32, (tq, LANES), 1)

    def run(with_latent):
        for pair in range(MLA_HEADS // 2):
            vsl = slice(pair * LANES, (pair + 1) * LANES)
            ksls = [slice((2 * pair + sub) * LANES, (2 * pair + sub + 1) * LANES) for sub in range(2)]
            qs = [q_ref[0, :, ksl] for ksl in ksls]

            def scores(off, width):
                return tuple(_dot_nt(q, k_ref[0, pl.ds(off, width), ksl]) for q, ksl in zip(qs, ksls))

            init = (jnp.full((tq, 1), NEG_BIG, F32), jnp.zeros((tq, 1), F32), jnp.zeros((tq, LANES), F32))
            carry = (init, init)
            if with_latent:
                n_chunks = seq // tk

                def body(i, c):
                    cur, state = c
                    nxt = scores(pl.multiple_of((i + 1) * tk, tk), tk)
                    state = _mla_pair_update(cur, v_ref[0, pl.ds(pl.multiple_of(i * tk, tk), tk), vsl], state)
                    return nxt, state

                cur, carry = lax.fori_loop(0, n_chunks - 1, body, (scores(0, tk), carry))
                ctx_scores = scores(seq, ctx_len)
                carry = _mla_pair_update(cur, v_ref[0, (n_chunks - 1) * tk:n_chunks * tk, vsl], carry)
            else:
                ctx_scores = scores(seq, ctx_len)
            carry = _mla_pair_update(ctx_scores, v_ref[0, seq:seq + ctx_len, vsl], carry)
            outs = [acc / l for (_, l, acc) in carry]
            o_ref[0, :, vsl] = jnp.where(lane < MLA_V, outs[0], outs[1]).astype(BF16)

    @pl.when(t < n_lat_tiles)
    def _():
        run(True)

    @pl.when(t >= n_lat_tiles)
    def _():
        run(False)


def _mla_attention(q, k, v, seq, ctx_len):
    nb, lt, wide = q.shape
    half = v.shape[2]
    tq = 256
    kern = functools.partial(_mla_kernel, seq=seq, ctx_len=ctx_len, tk=512)
    return pl.pallas_call(
        kern,
        grid=(nb, lt // tq),
        in_specs=[pl.BlockSpec((1, tq, wide), lambda b, t: (b, t, 0)),
                  pl.BlockSpec((1, lt, wide), lambda b, t: (b, 0, 0)),
                  pl.BlockSpec((1, lt, half), lambda b, t: (b, 0, 0))],
        out_specs=pl.BlockSpec((1, tq, half), lambda b, t: (b, t, 0)),
        out_shape=jax.ShapeDtypeStruct((nb, lt, half), BF16),
        compiler_params=_cparams("parallel", "parallel"),
        name="mla_attention",
    )(q, k, v)


def _na_bias_tables(rpb, rows):
    kh = min(NA_WIN_H, rows)
    last_r0 = rows - NA_ROWS_PER_BLOCK
    tabs = []
    for r0 in (0, 2 * NA_ROWS_PER_BLOCK, last_r0):
        kstart = int(np.clip(r0 - kh // 2, 0, rows - NA_KEY_ROWS))
        r = r0 + np.arange(NA_ROWS_PER_BLOCK)
        rs = np.clip(r - kh // 2, 0, rows - kh)
        kr = kstart + np.arange(NA_KEY_ROWS)
        row_ok = (kr[None, :] >= rs[:, None]) & (kr[None, :] < rs[:, None] + kh)
        row_off = np.clip(kr[None, :] - r[:, None] + (NA_WIN_H - 1), 0, 2 * NA_WIN_H - 2)
        c = np.arange(GRID_W)
        cs = np.clip(c - NA_WIN_W // 2, 0, GRID_W - NA_WIN_W)
        col_ok = (c[None, :] >= cs[:, None]) & (c[None, :] < cs[:, None] + NA_WIN_W)
        col_off = np.clip(c[None, :] - c[:, None] + (NA_WIN_W - 1), 0, 2 * NA_WIN_W - 2)
        b = rpb[:, row_off][..., col_off]
        ok = row_ok[:, :, None, None] & col_ok[None, None, :, :]
        b = jnp.where(jnp.asarray(ok)[None], b, NEG_BIG).transpose(0, 1, 3, 2, 4)
        tabs.append(b.reshape(rpb.shape[0], NA_ROWS_PER_BLOCK * GRID_W, NA_KEY_ROWS * GRID_W))
    return jnp.stack(tabs)


def _na_kernel(q_ref, k_ref, v_ref, bias_ref, o_ref, *, seq, ctx_len):
    tq = q_ref.shape[1]
    rows = seq // GRID_W
    n_lat_tiles = seq // tq
    nkeys = NA_KEY_ROWS * GRID_W
    t = pl.program_id(1)
    lane = lax.broadcasted_iota(jnp.int32, (tq, LANES), 1)
    kstart = jnp.clip(t * NA_ROWS_PER_BLOCK - NA_WIN_H // 2, 0, rows - NA_KEY_ROWS)
    koff = pl.multiple_of(kstart * GRID_W, GRID_W)

    def run(with_window):
        for pair in range(NA_HEADS // 2):
            sl = slice(pair * LANES, (pair + 1) * LANES)
            q2 = q_ref[0, :, sl].astype(F32)
            kc = k_ref[0, seq:seq + ctx_len, sl]
            vc = v_ref[0, seq:seq + ctx_len, sl]
            if with_window:
                kw = k_ref[0, pl.ds(koff, nkeys), sl]
                vw = v_ref[0, pl.ds(koff, nkeys), sl]
            outs = []
            for sub in range(2):
                keep = (lane < NA_HEAD_DIM) if sub == 0 else (lane >= NA_HEAD_DIM)
                q = jnp.where(keep, q2, 0.0).astype(BF16)
                s_c = _dot_nt(q, kc)
                m = jnp.max(s_c, axis=-1, keepdims=True)
                if with_window:
                    s_w = _dot_nt(q, kw) + bias_ref[0, 2 * pair + sub]
                    m = jnp.maximum(m, jnp.max(s_w, axis=-1, keepdims=True))
                p_c = jnp.exp(s_c - m)
                l = jnp.sum(p_c, axis=-1, keepdims=True)
                acc = _dot(p_c.astype(BF16), vc)
                if with_window:
                    p_w = jnp.exp(s_w - m)
                    l = l + jnp.sum(p_w, axis=-1, keepdims=True)
                    acc = acc + _dot(p_w.astype(BF16), vw)
                outs.append(acc / l)
            o_ref[0, :, sl] = jnp.where(lane < NA_HEAD_DIM, outs[0], outs[1]).astype(BF16)

    @pl.when(t < n_lat_tiles)
    def _():
        run(True)

    @pl.when(t >= n_lat_tiles)
    def _():
        run(False)


def _na_attention(q, k, v, bias, seq, ctx_len):
    nb, lt, w = q.shape
    tq = NA_ROWS_PER_BLOCK * GRID_W
    n_lat = seq // tq
    kern = functools.partial(_na_kernel, seq=seq, ctx_len=ctx_len)

    def variant(b, t):
        return (jnp.where(t == 0, 0, jnp.where(t >= n_lat - 1, 2, 1)), 0, 0, 0)

    return pl.pallas_call(
        kern,
        grid=(nb, lt // tq),
        in_specs=[pl.BlockSpec((1, tq, w), lambda b, t: (b, t, 0)),
                  pl.BlockSpec((1, lt, w), lambda b, t: (b, 0, 0)),
                  pl.BlockSpec((1, lt, w), lambda b, t: (b, 0, 0)),
                  pl.BlockSpec((1,) + bias.shape[1:], variant)],
        out_specs=pl.BlockSpec((1, tq, w), lambda b, t: (b, t, 0)),
        out_shape=jax.ShapeDtypeStruct((nb, lt, w), BF16),
        compiler_params=_cparams("parallel", "parallel"),
        name="na_attention",
    )(q, k, v, bias)


def _out_proj_kernel(x_ref, a_ref, b_ref, wa_ref, wb_ref, gate_ref, o_ref):
    y = _dot(a_ref[0], wa_ref[...]) + _dot(b_ref[0], wb_ref[...])
    o_ref[0] = x_ref[0] + gate_ref[0] * y


def _out_proj(xu, a, b, w_out, gate, seq):
    nb, lt, d = xu.shape
    tm = 256
    ka = a.shape[2]
    wa = w_out[:ka].astype(BF16)
    wb = w_out[ka:].astype(BF16)
    row = lambda bb, t: (bb, t, 0)
    full = lambda arr: pl.BlockSpec(arr.shape, lambda bb, t: (0,) * arr.ndim)
    return pl.pallas_call(
        _out_proj_kernel,
        grid=(nb, lt // tm),
        in_specs=[pl.BlockSpec((1, tm, d), row), pl.BlockSpec((1, tm, ka), row),
                  pl.BlockSpec((1, tm, b.shape[2]), row), full(wa), full(wb), _mod_spec(seq // tm, nb, d)],
        out_specs=pl.BlockSpec((1, tm, d), row),
        out_shape=jax.ShapeDtypeStruct(xu.shape, F32),
        compiler_params=_cparams("parallel", "parallel"),
        name="out_proj",
    )(xu, a, b, wa, wb, gate)


def _ffn_kernel(x_ref, g_ref, sc_ref, sh_ref, gate_ref, wg_ref, wu_ref, wd_ref, o_ref, h_scr, acc_scr):
    f = pl.program_id(2)

    @pl.when(f == 0)
    def _():
        h_scr[...] = _norm_mod(x_ref[0], g_ref[...], sc_ref[0], sh_ref[0]).astype(BF16)
        acc_scr[...] = jnp.zeros_like(acc_scr)

    h = h_scr[...]
    a = _dot(h, wg_ref[...])
    u = _dot(h, wu_ref[...])
    acc_scr[...] += _dot((a * jax.nn.sigmoid(a) * u).astype(BF16), wd_ref[...])

    @pl.when(f == pl.num_programs(2) - 1)
    def _():
        o_ref[0] = x_ref[0] + gate_ref[0] * acc_scr[...]


def _ffn(xu, norm_g, scale, shift, gate, w_gate, w_up, w_down, seq):
    nb, lt, d = xu.shape
    dff = w_gate.shape[1]
    tm = 256
    tf = dff
    row = lambda b, t, f: (b, t, 0)
    n_lat = seq // tm
    mod = pl.BlockSpec((1, 1, d), lambda b, t, f: (jnp.where(t < n_lat, b, nb), 0, 0))
    g2 = norm_g.reshape(1, d)
    once = pl.Buffered(1)
    return pl.pallas_call(
        _ffn_kernel,
        grid=(nb, lt // tm, dff // tf),
        in_specs=[pl.BlockSpec((1, tm, d), row), pl.BlockSpec((1, d), lambda b, t, f: (0, 0)), mod, mod, mod,
                  pl.BlockSpec((d, tf), lambda b, t, f: (0, f), pipeline_mode=once),
                  pl.BlockSpec((d, tf), lambda b, t, f: (0, f), pipeline_mode=once),
                  pl.BlockSpec((tf, d), lambda b, t, f: (f, 0), pipeline_mode=once)],
        out_specs=pl.BlockSpec((1, tm, d), row),
        out_shape=jax.ShapeDtypeStruct(xu.shape, F32),
        scratch_shapes=[pltpu.VMEM((tm, d), BF16), pltpu.VMEM((tm, d), F32)],
        compiler_params=_cparams("parallel", "parallel", "arbitrary"),
        name="dense_swiglu",
    )(xu, g2, scale, shift, gate, w_gate.astype(BF16), w_up.astype(BF16), w_down.astype(BF16))


def _s5_in_kernel(x_ref, g_ref, sc_ref, sh_ref, w_ref, o_ref):
    h = _norm_mod(x_ref[0], g_ref[...], sc_ref[0], sh_ref[0]).astype(BF16)
    u = _dot(h, w_ref[...])
    for k in range(o_ref.shape[0]):
        o_ref[k, 0] = u[:, k * LANES:(k + 1) * LANES]


def _s5_in(xu, norm_g, scale, shift, w_in, seq):
    nb, lt, d = xu.shape
    tm = 256
    w = w_in.astype(BF16)
    row = lambda b, t: (b, t, 0)
    mod = _mod_spec(seq // tm, nb, d)
    g2 = norm_g.reshape(1, d)
    return pl.pallas_call(
        _s5_in_kernel,
        grid=(nb, lt // tm),
        in_specs=[pl.BlockSpec((1, tm, d), row), pl.BlockSpec((1, d), lambda b, t: (0, 0)), mod, mod,
                  pl.BlockSpec(w.shape, lambda b, t: (0, 0))],
        out_specs=pl.BlockSpec((w.shape[1] // LANES, 1, tm, LANES), lambda b, t: (0, b, t, 0)),
        out_shape=jax.ShapeDtypeStruct((w.shape[1] // LANES, nb, lt, LANES), F32),
        compiler_params=_cparams("parallel", "parallel"),
        name="s5_in",
    )(xu, g2, scale, shift, w)


def _s5_operators(a_re, a_im, log_step, b_re, b_im, c_re, c_im):
    hp = lax.Precision.HIGHEST
    t_len = S5_CHUNK
    ops = []
    for direction in range(2):
        dt = jnp.exp(log_step[direction])[:, None]
        lre, lim = a_re[direction] * dt, a_im[direction] * dt
        decay = jnp.exp(lre)
        ab_re, ab_im = decay * jnp.cos(lim), decay * jnp.sin(lim)
        den = a_re[direction] ** 2 + a_im[direction] ** 2
        f_re = ((ab_re - 1) * a_re[direction] + ab_im * a_im[direction]) / den
        f_im = (ab_im * a_re[direction] - (ab_re - 1) * a_im[direction]) / den
        bb_re = f_re[..., None] * b_re[direction] - f_im[..., None] * b_im[direction]
        bb_im = f_re[..., None] * b_im[direction] + f_im[..., None] * b_re[direction]
        cr, ci = c_re[direction], c_im[direction]
        tau = jnp.arange(t_len + 1, dtype=F32)[:, None, None]
        pw = jnp.exp(tau * lre[None])
        pw_re, pw_im = pw * jnp.cos(tau * lim[None]), pw * jnp.sin(tau * lim[None])
        ab_b_re = pw_re[..., None] * bb_re[None] - pw_im[..., None] * bb_im[None]
        ab_b_im = pw_re[..., None] * bb_im[None] + pw_im[..., None] * bb_re[None]
        kk = (jnp.einsum('gip,tgpj->tgij', cr, ab_b_re, precision=hp)
              - jnp.einsum('gip,tgpj->tgij', ci, ab_b_im, precision=hp))
        ca_re = cr[None] * pw_re[:, :, None, :] - ci[None] * pw_im[:, :, None, :]
        ca_im = cr[None] * pw_im[:, :, None, :] + ci[None] * pw_re[:, :, None, :]
        s = np.arange(t_len)
        if direction == 0:
            lag = s[None, :] - s[:, None]
            z_pow = t_len - 1 - s
            c_pow = s + 1
        else:
            lag = s[:, None] - s[None, :]
            z_pow = s
            c_pow = t_len - s
        ok = lag >= 0
        m = jnp.where(jnp.asarray(ok)[:, :, None, None, None], kk[np.clip(lag, 0, t_len)], 0.0)
        ops.append((m, [ab_b_re[z_pow], ab_b_im[z_pow]],
                    [ca_re[c_pow], -ca_im[c_pow]],
                    [pw_re[t_len], pw_im[t_len]]))
    lg = S5_LANE_GROUPS
    nblk = a_re.shape[1] // lg
    eye = jnp.eye(lg, dtype=BF16)
    blocked = lambda arr, axis: arr.astype(BF16).reshape(arr.shape[:axis] + (nblk, lg) + arr.shape[axis + 1:])
    width = t_len * lg * S5_GROUP
    m_sum = blocked(ops[0][0] + ops[1][0], 2)
    w_intra = jnp.einsum('ab,stnbij->nsajtbi', eye, m_sum).reshape(nblk, width, width)
    wz4 = blocked(jnp.stack(ops[0][1] + ops[1][1]), 2)
    w_z = jnp.einsum('ab,ksnbpj->nsajkbp', eye, wz4).reshape(nblk, width, 4 * lg * S5_STATE)
    wc4 = blocked(jnp.stack(ops[0][2] + ops[1][2]), 2)
    w_c = jnp.einsum('ab,ktnbip->nkaptbi', eye, wc4).reshape(nblk, 4 * lg * S5_STATE, width)
    a_t = jnp.stack(ops[0][3] + ops[1][3]).reshape(4, nblk, lg * S5_STATE).transpose(1, 0, 2)
    return w_intra, w_z, w_c, a_t


def _s5_core_kernel(a_ref, wi_ref, wz_ref, wc_ref, at_ref, y_ref, z_scr, h_scr, *, n_ctx_chunks):
    n_chunks = z_scr.shape[0]
    n_lat = n_chunks - n_ctx_chunks
    w = z_scr.shape[1] // 4
    a = a_ref[0, 0].astype(BF16)
    z_scr[...] = _dot(a, wz_ref[0])
    ar_f, ai_f, ar_b, ai_b = [at_ref[0, k:k + 1, :] for k in range(4)]

    def step(k, carry):
        (fr, fi), (br, bi) = carry
        cf = jnp.where(k < n_ctx_chunks, n_lat + k, k - n_ctx_chunks)
        cb = n_chunks - 1 - k
        rf, rb = pl.ds(cf, 1), pl.ds(cb, 1)
        h_scr[rf, 0:w] = fr
        h_scr[rf, w:2 * w] = fi
        h_scr[rb, 2 * w:3 * w] = br
        h_scr[rb, 3 * w:4 * w] = bi
        fwd = (ar_f * fr - ai_f * fi + z_scr[rf, 0:w], ar_f * fi + ai_f * fr + z_scr[rf, w:2 * w])
        bwd = (ar_b * br - ai_b * bi + z_scr[rb, 2 * w:3 * w], ar_b * bi + ai_b * br + z_scr[rb, 3 * w:4 * w])
        return fwd, bwd

    zero = jnp.zeros((1, w), F32)
    lax.fori_loop(0, n_chunks, step, ((zero, zero), (zero, zero)))
    y_ref[0, 0] = _dot(a, wi_ref[0]) + _dot(h_scr[...].astype(BF16), wc_ref[0])


def _s5_core(u_blocks, ops, n_ctx_chunks):
    nblk, nb, lt, lanes = u_blocks.shape
    n_chunks = lt // S5_CHUNK
    width = S5_CHUNK * lanes
    w_intra, w_z, w_c, a_t = ops
    kern = functools.partial(_s5_core_kernel, n_ctx_chunks=n_ctx_chunks)
    once = lambda a: pl.BlockSpec((1,) + a.shape[1:], lambda g, b: (g, 0, 0), pipeline_mode=pl.Buffered(1))
    tok = pl.BlockSpec((1, 1, n_chunks, width), lambda g, b: (g, b, 0, 0))
    y = pl.pallas_call(
        kern,
        grid=(nblk, nb),
        in_specs=[tok, once(w_intra), once(w_z), once(w_c),
                  pl.BlockSpec((1,) + a_t.shape[1:], lambda g, b: (g, 0, 0))],
        out_specs=tok,
        out_shape=jax.ShapeDtypeStruct((nblk, nb, n_chunks, width), F32),
        scratch_shapes=[pltpu.VMEM((n_chunks, w_z.shape[2]), F32), pltpu.VMEM((n_chunks, w_c.shape[1]), F32)],
        compiler_params=_cparams("parallel", "parallel"),
        name="s5_core",
    )(u_blocks.reshape(nblk, nb, n_chunks, width), w_intra, w_z, w_c, a_t)
    return y.reshape(u_blocks.shape)


def _s5_out_kernel(x_ref, u_ref, y_ref, d_ref, w_ref, gate_ref, o_ref):
    d = x_ref.shape[2]
    u = jnp.concatenate([u_ref[k, 0] for k in range(u_ref.shape[0])], axis=1)
    y = jnp.concatenate([y_ref[k, 0] for k in range(y_ref.shape[0])], axis=1)
    y = u * d_ref[...] + y
    z = _dot(jax.nn.gelu(y).astype(BF16), w_ref[...])
    o_ref[0] = x_ref[0] + gate_ref[0] * (z[:, :d] * jax.nn.sigmoid(z[:, d:]))


def _s5_out(x, u_blocks, y_blocks, d_skip, w_glu, gate):
    nb, seq, d = x.shape
    tm = 256
    nblk, _, _, lanes = u_blocks.shape
    w = nblk * lanes
    row = lambda b, t: (b, t, 0)
    blk = pl.BlockSpec((nblk, 1, tm, lanes), lambda b, t: (0, b, t, 0))
    wg = w_glu.astype(BF16)
    return pl.pallas_call(
        _s5_out_kernel,
        grid=(nb, seq // tm),
        in_specs=[pl.BlockSpec((1, tm, d), row), blk, blk,
                  pl.BlockSpec((1, w), lambda b, t: (0, 0)), pl.BlockSpec(wg.shape, lambda b, t: (0, 0)),
                  pl.BlockSpec((1, 1, d), lambda b, t: (b, 0, 0))],
        out_specs=pl.BlockSpec((1, tm, d), row),
        out_shape=jax.ShapeDtypeStruct(x.shape, F32),
        compiler_params=_cparams("parallel", "parallel"),
        name="s5_out",
    )(x, u_blocks, y_blocks, d_skip.reshape(1, w), wg, gate)


R_IDX, R_WT, R_RANK = 0, TOP_K, 2 * TOP_K


def _router_kernel(x_ref, g_ref, sc_ref, sh_ref, w_ref, tri_ref, rec_ref, cnt_ref, base_scr):
    @pl.when((pl.program_id(0) == 0) & (pl.program_id(1) == 0))
    def _():
        base_scr[...] = jnp.zeros_like(base_scr)

    t = _norm_mod(x_ref[0], g_ref[...], sc_ref[0], sh_ref[0]).astype(BF16)
    logits = _dot(t, w_ref[...])
    lane = lax.broadcasted_iota(jnp.int32, logits.shape, 1)
    big = jnp.int32(LANES)
    l1 = jnp.where(lane < N_EXPERTS, logits, -jnp.inf)
    v1 = jnp.max(l1, axis=-1, keepdims=True)
    i1 = jnp.min(jnp.where(l1 == v1, lane, big), axis=-1, keepdims=True)
    l2 = jnp.where(lane == i1, -jnp.inf, l1)
    v2 = jnp.max(l2, axis=-1, keepdims=True)
    i2 = jnp.min(jnp.where(l2 == v2, lane, big), axis=-1, keepdims=True)
    e2 = jnp.exp(v2 - v1)
    w1 = 1.0 / (1.0 + e2)
    w2 = e2 / (1.0 + e2)
    hit = jnp.where((lane == i1) | (lane == i2), 1.0, 0.0)
    before = _dot(tri_ref[...], hit.astype(BF16)) + base_scr[0:1, :]
    r1 = jnp.sum(jnp.where(lane == i1, before, 0.0), axis=-1, keepdims=True)
    r2 = jnp.sum(jnp.where(lane == i2, before, 0.0), axis=-1, keepdims=True)
    base_scr[...] = base_scr[...] + jnp.sum(hit, axis=0, keepdims=True)
    cnt_ref[...] = base_scr[...]
    rec = jnp.zeros(logits.shape, F32)
    for k, val in enumerate((i1.astype(F32), i2.astype(F32), w1, w2, r1, r2)):
        rec = jnp.where(lane == k, val, rec)
    rec_ref[0] = rec


def _router(x, norm_g, scale, shift, w_router):
    nb, seq, d = x.shape
    tm = 256
    wr = jnp.zeros((d, LANES), F32).at[:, :N_EXPERTS].set(w_router).astype(BF16)
    tri = jnp.asarray(np.tril(np.ones((tm, tm), np.float32), -1), BF16)
    row = lambda b, t: (b, t, 0)
    mod = pl.BlockSpec((1, 1, d), lambda b, t: (b, 0, 0))
    g2 = norm_g.reshape(1, d)
    return pl.pallas_call(
        _router_kernel,
        grid=(nb, seq // tm),
        in_specs=[pl.BlockSpec((1, tm, d), row), pl.BlockSpec((1, d), lambda b, t: (0, 0)), mod, mod,
                  pl.BlockSpec(wr.shape, lambda b, t: (0, 0)), pl.BlockSpec(tri.shape, lambda b, t: (0, 0))],
        out_specs=[pl.BlockSpec((1, tm, LANES), row), pl.BlockSpec((8, LANES), lambda b, t: (0, 0))],
        out_shape=[jax.ShapeDtypeStruct((nb, seq, LANES), F32), jax.ShapeDtypeStruct((8, LANES), F32)],
        scratch_shapes=[pltpu.VMEM((8, LANES), F32)],
        compiler_params=_cparams("arbitrary", "arbitrary"),
        name="router",
    )(x, g2, scale, shift, wr, tri)


def _row_copies_wait(src_ref, dst_ref, sem):
    pltpu.make_async_copy(src_ref, dst_ref, sem).wait()


def _dispatch_kernel(dest_ref, fill_ref, x_ref, g_ref, sc_ref, sh_ref, xs_ref, tbuf, zbuf, sem, zsem):
    tm = x_ref.shape[1]
    ztm = zbuf.shape[0]
    step = pl.program_id(0) * pl.num_programs(1) + pl.program_id(1)
    n_steps = pl.num_programs(0) * pl.num_programs(1)
    slot = step % 2

    def drain(s):
        for _ in range(TOP_K):
            _row_copies_wait(tbuf.at[s], xs_ref.at[pl.ds(0, tm)], sem.at[s])

    @pl.when(step == 0)
    def _():
        zbuf[...] = jnp.zeros_like(zbuf)

        def fill(start):
            def body(i, carry):
                @pl.when(fill_ref[i] > 0)
                def _():
                    cp = pltpu.make_async_copy(zbuf, xs_ref.at[pl.ds(pl.multiple_of(i * ztm, ztm), ztm)], zsem)
                    if start:
                        cp.start()
                    else:
                        cp.wait()
                return carry
            lax.fori_loop(0, fill_ref.shape[0], body, 0)

        fill(True)
        fill(False)

    @pl.when(step >= 2)
    def _():
        drain(slot)

    tbuf[slot] = _norm_mod(x_ref[0], g_ref[...], sc_ref[0], sh_ref[0])
    base = step * (tm * TOP_K)

    def body(r, carry):
        for k in range(TOP_K):
            dst = dest_ref[base + r * TOP_K + k]
            pltpu.make_async_copy(tbuf.at[slot, pl.ds(r, 1)], xs_ref.at[pl.ds(dst, 1)], sem.at[slot]).start()
        return carry

    lax.fori_loop(0, tm, body, 0, unroll=8)

    @pl.when(step == n_steps - 1)
    def _():
        drain(slot)

        @pl.when(n_steps > 1)
        def _():
            drain(1 - slot)


def _dispatch(x, norm_g, scale, shift, dest, tile_fill, n_rows, expert_tm):
    nb, seq, d = x.shape
    tm = 256
    row = lambda b, t, dr, fr: (b, t, 0)
    mod = pl.BlockSpec((1, 1, d), lambda b, t, dr, fr: (b, 0, 0))
    g2 = norm_g.reshape(1, d)
    grid_spec = pltpu.PrefetchScalarGridSpec(
        num_scalar_prefetch=2,
        grid=(nb, seq // tm),
        in_specs=[pl.BlockSpec((1, tm, d), row), pl.BlockSpec((1, d), lambda b, t, dr, fr: (0, 0)), mod, mod],
        out_specs=pl.BlockSpec(memory_space=pl.ANY),
        scratch_shapes=[pltpu.VMEM((2, tm, d), F32), pltpu.VMEM((expert_tm, d), F32),
                        pltpu.SemaphoreType.DMA((2,)), pltpu.SemaphoreType.DMA(())],
    )
    return pl.pallas_call(
        _dispatch_kernel,
        grid_spec=grid_spec,
        out_shape=jax.ShapeDtypeStruct((n_rows, d), F32),
        compiler_params=_cparams("arbitrary", "arbitrary"),
        name="dispatch",
    )(dest, tile_fill, x, g2, scale, shift)


def _expert_kernel(te_ref, tr_ref, x_ref, wg_ref, wu_ref, wd_ref, o_ref, xb_scr, acc_scr):
    i = pl.program_id(0)
    f = pl.program_id(1)
    nf = pl.num_programs(1)
    live = tr_ref[i] > 0

    @pl.when(jnp.logical_not(live) & (f == nf - 1))
    def _():
        o_ref[...] = jnp.zeros_like(o_ref)

    @pl.when(live)
    def _():
        @pl.when(f == 0)
        def _():
            xb_scr[...] = x_ref[...].astype(BF16)
            acc_scr[...] = jnp.zeros_like(acc_scr)

        x = xb_scr[...]
        a = _dot(x, wg_ref[0])
        u = _dot(x, wu_ref[0])
        acc_scr[...] += _dot((a * jax.nn.sigmoid(a) * u).astype(BF16), wd_ref[0])

        @pl.when(f == nf - 1)
        def _():
            o_ref[...] = acc_scr[...]


def _experts(xs, tile_expert, tile_rows, w_gate, w_up, w_down, tm, tf):
    n_rows, d = xs.shape
    dff = w_gate.shape[2]
    nf = dff // tf

    def fsel(i, f, te, tr):
        return jnp.where(tr[i] > 0, f, nf - 1)

    grid_spec = pltpu.PrefetchScalarGridSpec(
        num_scalar_prefetch=2,
        grid=(n_rows // tm, nf),
        in_specs=[pl.BlockSpec((tm, d), lambda i, f, te, tr: (i, 0)),
                  pl.BlockSpec((1, d, tf), lambda i, f, te, tr: (te[i], 0, fsel(i, f, te, tr))),
                  pl.BlockSpec((1, d, tf), lambda i, f, te, tr: (te[i], 0, fsel(i, f, te, tr))),
                  pl.BlockSpec((1, tf, d), lambda i, f, te, tr: (te[i], fsel(i, f, te, tr), 0))],
        out_specs=pl.BlockSpec((tm, d), lambda i, f, te, tr: (i, 0)),
        scratch_shapes=[pltpu.VMEM((tm, d), BF16), pltpu.VMEM((tm, d), F32)],
    )
    return pl.pallas_call(
        _expert_kernel,
        grid_spec=grid_spec,
        out_shape=jax.ShapeDtypeStruct((n_rows, d), F32),
        compiler_params=_cparams("arbitrary", "arbitrary"),
        name="experts",
    )(tile_expert, tile_rows, xs, w_gate, w_up, w_down)


def _combine_kernel(dest_ref, x_ref, rec_ref, gate_ref, g_ref, ys_ref, o_ref, ybuf, sem):
    tm = x_ref.shape[1]
    step = pl.program_id(0) * pl.num_programs(1) + pl.program_id(1)
    n_steps = pl.num_programs(0) * pl.num_programs(1)
    slot = step % 2

    def issue(st, sl):
        base = st * (tm * TOP_K)

        def body(r, carry):
            for k in range(TOP_K):
                src = dest_ref[base + r * TOP_K + k]
                pltpu.make_async_copy(ys_ref.at[pl.ds(src, 1)], ybuf.at[sl, k, pl.ds(r, 1)], sem.at[sl]).start()
            return carry

        lax.fori_loop(0, tm, body, 0, unroll=8)

    @pl.when(step == 0)
    def _():
        issue(step, slot)

    @pl.when(step + 1 < n_steps)
    def _():
        issue(step + 1, 1 - slot)

    for k in range(TOP_K):
        _row_copies_wait(ys_ref.at[pl.ds(0, tm)], ybuf.at[slot, k], sem.at[slot])

    rec = rec_ref[0]
    y = rec[:, R_WT:R_WT + 1] * ybuf[slot, 0] + rec[:, R_WT + 1:R_WT + 2] * ybuf[slot, 1]
    x = x_ref[0] + gate_ref[0] * y
    o_ref[0] = _rms(x) * g_ref[...]


def _combine(x, rec, gate, final_g, ys, dest):
    nb, seq, d = x.shape
    tm = 256
    row = lambda b, t, dr: (b, t, 0)
    grid_spec = pltpu.PrefetchScalarGridSpec(
        num_scalar_prefetch=1,
        grid=(nb, seq // tm),
        in_specs=[pl.BlockSpec((1, tm, d), row), pl.BlockSpec((1, tm, LANES), row),
                  pl.BlockSpec((1, 1, d), lambda b, t, dr: (b, 0, 0)), pl.BlockSpec((1, d), lambda b, t, dr: (0, 0)),
                  pl.BlockSpec(memory_space=pl.ANY)],
        out_specs=pl.BlockSpec((1, tm, d), row),
        scratch_shapes=[pltpu.VMEM((2, TOP_K, tm, d), F32), pltpu.SemaphoreType.DMA((2,))],
    )
    return pl.pallas_call(
        _combine_kernel,
        grid_spec=grid_spec,
        out_shape=jax.ShapeDtypeStruct(x.shape, F32),
        compiler_params=_cparams("arbitrary", "arbitrary"),
        name="combine_final_norm",
    )(dest, x, rec, gate, final_g.reshape(1, d), ys)


def _moe(x, norm_g, scale, shift, gate, w_router, w_gate, w_up, w_down, final_g):
    nb, seq, d = x.shape
    n_tok = nb * seq
    tm, tf = 512, w_gate.shape[2] // 2
    rec, cnt = _router(x, norm_g, scale, shift, w_router)
    counts = cnt[0, :N_EXPERTS].astype(jnp.int32)
    padded = ((counts + tm - 1) // tm) * tm
    pend = jnp.cumsum(padded)
    pstart = pend - padded
    idx = rec[:, :, R_IDX:R_IDX + TOP_K].astype(jnp.int32)
    rank = rec[:, :, R_RANK:R_RANK + TOP_K].astype(jnp.int32)
    seg = jnp.sum(jnp.where(idx[..., None] == jnp.arange(N_EXPERTS), pstart, 0), axis=-1)
    dest = (seg + rank).reshape(n_tok * TOP_K)
    n_rows = n_tok * TOP_K + N_EXPERTS * tm
    tile_start = jnp.arange(n_rows // tm, dtype=jnp.int32) * tm
    tile_expert = jnp.sum((tile_start[:, None] >= pend[None, :]).astype(jnp.int32), axis=1)
    live = tile_expert < N_EXPERTS
    last_live = jnp.max(jnp.where(live, tile_expert, 0))
    tile_expert = jnp.where(live, tile_expert, last_live)
    seg_rows = jnp.sum(jnp.where(tile_expert[:, None] == jnp.arange(N_EXPERTS), (pstart + counts)[None, :], 0), axis=1)
    tile_rows = jnp.where(live, jnp.clip(seg_rows - tile_start, 0, tm), 0).astype(jnp.int32)
    tile_fill = (tile_rows < tm).astype(jnp.int32)

    xs = _dispatch(x, norm_g, scale, shift, dest, tile_fill, n_rows, tm)
    ys = _experts(xs, tile_expert.astype(jnp.int32), tile_rows,
                  w_gate.astype(BF16), w_up.astype(BF16), w_down.astype(BF16), tm, tf)
    return _combine(x, rec, gate, final_g, ys, dest)


def kernel(x, c, ctx, c_ctx, mod_w, mod_b, norm1_g, norm2_g, ev_w_in, ev_q_norm_g, ev_w_qb, ev_kv_norm_g, ev_w_kvb, ev_na_rpb, ev_w_out, ev_ffn_w_gate, ev_ffn_w_up, ev_ffn_w_down, od_w_in, od_a_re, od_a_im, od_log_step, od_b_re, od_b_im, od_c_re, od_c_im, od_d, od_w_glu, moe_w_router, moe_w_gate, moe_w_up, moe_w_down, final_g):
    nb, seq, d = x.shape
    ctx_len = ctx.shape[1]
    assert mod_w.shape[0] == 2 and nb < 8
    assert seq % (NA_ROWS_PER_BLOCK * GRID_W) == 0 and ctx_len == NA_ROWS_PER_BLOCK * GRID_W

    cond = jnp.zeros((8, d), F32).at[:nb].set(c).at[nb].set(c_ctx)
    mods = _adaln(cond, mod_w, mod_b)

    def mod_parts(layer):
        return [mods[layer, :, i * d:(i + 1) * d].reshape(8, 1, d) for i in range(N_MOD)]

    xu = jnp.concatenate([x, ctx], axis=1)

    sh1, sc1, g1, sh2, sc2, g2 = mod_parts(0)
    cos, sin = _rope_tables(seq, ctx_len)
    weights = _even_weights(ev_w_in[0], ev_w_qb[0], ev_w_kvb[0])
    q, k, v, nq, nk, nv = _even_project(xu, norm1_g[0], sc1, sh1, weights, ev_q_norm_g[0], ev_kv_norm_g[0],
                                        cos, sin, seq)
    mla = _mla_attention(q, k, v, seq, ctx_len)
    bias = _na_bias_tables(ev_na_rpb[0], seq // GRID_W)
    na = _na_attention(nq, nk, nv, bias, seq, ctx_len)
    xu = _out_proj(xu, mla, na, ev_w_out[0], g1, seq)
    xu = _ffn(xu, norm2_g[0], sc2, sh2, g2, ev_ffn_w_gate[0], ev_ffn_w_up[0], ev_ffn_w_down[0], seq)

    sh1, sc1, g1, sh2, sc2, g2 = mod_parts(1)
    u = _s5_in(xu, norm1_g[1], sc1, sh1, od_w_in[0], seq)
    ops = _s5_operators(od_a_re[0], od_a_im[0], od_log_step[0], od_b_re[0], od_b_im[0], od_c_re[0], od_c_im[0])
    y = _s5_core(u, ops, ctx_len // S5_CHUNK)
    xl = xu[:, :seq]
    xl = _s5_out(xl, u, y, od_d[0], od_w_glu[0], g1[:nb])
    return _moe(xl, norm2_g[1], sc2[:nb], sh2[:nb], g2[:nb], moe_w_router[0], moe_w_gate[0], moe_w_up[0],
                moe_w_down[0], final_g)
```

```python
import functools
import math

import numpy as np
import jax
import jax.numpy as jnp
from jax import lax
from jax.experimental import pallas as pl
from jax.experimental.pallas import tpu as pltpu

F32 = jnp.float32
BF16 = jnp.bfloat16

LANES = 128
VMEM_LIMIT_BYTES = 52 * 1024 * 1024

NORM_EPS = 1e-6
ROPE_BASE = 10000.0
GRID_W = 64
N_MOD = 6

MLA_HEADS = 8
MLA_NOPE = 64
MLA_ROPE = 32
MLA_V = 64
Q_LORA = 384
KV_LORA = 256
MLA_SCALE = (MLA_NOPE + MLA_ROPE) ** -0.5
MLA_EXP2_SCALE = MLA_SCALE * math.log2(math.e)

NA_HEADS = 8
NA_HEAD_DIM = 64
NA_WIN_H = 8
NA_WIN_W = 16
NA_SCALE = NA_HEAD_DIM ** -0.5
NA_ROWS_PER_BLOCK = 4
NA_KEY_ROWS = NA_ROWS_PER_BLOCK + NA_WIN_H - 1

S5_GROUP = 16
S5_STATE = 64
S5_CHUNK = 16
S5_LANE_GROUPS = LANES // S5_GROUP

N_EXPERTS = 8
TOP_K = 2

NEG_BIG = -1e30


def _cparams(*sem):
    return pltpu.CompilerParams(dimension_semantics=sem, vmem_limit_bytes=VMEM_LIMIT_BYTES)


def _rms(x):
    return x * lax.rsqrt(jnp.mean(x * x, axis=-1, keepdims=True) + NORM_EPS)


def _norm_mod(x, g, scale, shift):
    return (_rms(x) * g) * (1 + scale) + shift


def _dot(a, b):
    return jnp.dot(a, b, preferred_element_type=F32)


def _dot_nt(a, b):
    return lax.dot_general(a, b, (((1,), (1,)), ((), ())), preferred_element_type=F32)


def _adaln_kernel(c_ref, w_ref, b_ref, o_ref):
    c = c_ref[...]
    s = (c * jax.nn.sigmoid(c)).astype(BF16)
    o_ref[0] = _dot(s, w_ref[0].astype(BF16)) + b_ref[0]


def _adaln(cond, mod_w, mod_b):
    nl, d, n = mod_w.shape
    tn = 1536
    return pl.pallas_call(
        _adaln_kernel,
        grid=(nl, n // tn),
        in_specs=[pl.BlockSpec((8, d), lambda l, j: (0, 0)),
                  pl.BlockSpec((1, d, tn), lambda l, j: (l, 0, j)),
                  pl.BlockSpec((1, 1, tn), lambda l, j: (l, 0, j))],
        out_specs=pl.BlockSpec((1, 8, tn), lambda l, j: (l, 0, j)),
        out_shape=jax.ShapeDtypeStruct((nl, 8, n), F32),
        compiler_params=_cparams("parallel", "parallel"),
        name="adaln",
    )(cond, mod_w, mod_b.reshape(nl, 1, n))


def _even_proj_kernel(x_ref, g_ref, sc_ref, sh_ref, win_ref, qg_ref, kvg_ref, wq_ref, wqr_ref, wkk_ref,
                      wkv_ref, cos_ref, sin_ref, q_ref, k_ref, v_ref, nq_ref, nk_ref, nv_ref):
    h = _norm_mod(x_ref[0], g_ref[...], sc_ref[0], sh_ref[0]).astype(BF16)
    p = _dot(h, win_ref[...])
    c0, c1, c2, c3 = Q_LORA, Q_LORA + KV_LORA, Q_LORA + KV_LORA + LANES, Q_LORA + KV_LORA + 2 * LANES
    cqn = (_rms(p[:, :c0]) * qg_ref[...]).astype(BF16)
    ckvn = (_rms(p[:, c0:c1]) * kvg_ref[...]).astype(BF16)
    cos = cos_ref[...]
    sin = sin_ref[...]
    kr = p[:, c1:c2] * cos + p[:, c2:c3] * sin
    qa = _dot(cqn, wq_ref[...])
    qb = _dot(cqn, wqr_ref[...])
    kk = _dot(ckvn, wkk_ref[...])
    for hd in range(MLA_HEADS):
        sl = slice(hd * LANES, (hd + 1) * LANES)
        q_ref[0, :, sl] = (qa[:, sl] * cos + qb[:, sl] * sin).astype(BF16)
        k_ref[0, :, sl] = (kk[:, sl] + kr).astype(BF16)
    vlane = lax.broadcasted_iota(jnp.int32, (1, MLA_HEADS * LANES), 1) & (LANES - 1)
    v_ref[0] = (_dot(ckvn, wkv_ref[...]) + jnp.where(vlane == MLA_V, 1.0, 0.0)).astype(BF16)
    w = NA_HEADS * NA_HEAD_DIM
    nq_ref[0] = (p[:, c3:c3 + w] * NA_SCALE).astype(BF16)
    nk_ref[0] = p[:, c3 + w:c3 + 2 * w].astype(BF16)
    nv_ref[0] = p[:, c3 + 2 * w:c3 + 3 * w].astype(BF16)


def _rot_half_cols(w):
    q = MLA_ROPE // 4
    return jnp.concatenate([-w[:, q:2 * q], w[:, :q], -w[:, 3 * q:], w[:, 2 * q:3 * q]], axis=1)


def _rope_tables(seq, ctx_len):
    q = MLA_ROPE // 4
    t = np.arange(seq)
    inv_freq = ROPE_BASE ** (-jnp.arange(q, dtype=F32) / q)
    ang_r = jnp.asarray(t // GRID_W, F32)[:, None] * inv_freq[None, :]
    ang_c = jnp.asarray(t % GRID_W, F32)[:, None] * inv_freq[None, :]
    cos32 = jnp.concatenate([jnp.cos(ang_r)] * 2 + [jnp.cos(ang_c)] * 2, axis=1)
    sin32 = jnp.concatenate([jnp.sin(ang_r)] * 2 + [jnp.sin(ang_c)] * 2, axis=1)
    cos = jnp.ones((seq + ctx_len, LANES), F32).at[:seq, MLA_NOPE:MLA_NOPE + MLA_ROPE].set(cos32)
    sin = jnp.zeros((seq + ctx_len, LANES), F32).at[:seq, MLA_NOPE:MLA_NOPE + MLA_ROPE].set(sin32)
    return cos, sin


def _even_weights(w_in, w_qb, w_kvb):
    d = w_in.shape[0]
    c1 = Q_LORA + KV_LORA
    wkr = w_in[:, c1:c1 + MLA_ROPE]
    pad = lambda w: jnp.zeros((d, LANES), F32).at[:, MLA_NOPE:MLA_NOPE + MLA_ROPE].set(w)
    win = jnp.concatenate([w_in[:, :c1], pad(wkr), pad(_rot_half_cols(wkr)), w_in[:, c1 + MLA_ROPE:]], axis=1)
    hq = MLA_NOPE + MLA_ROPE
    wq = w_qb.reshape(Q_LORA, MLA_HEADS, hq)
    zq = jnp.zeros((Q_LORA, MLA_HEADS, LANES - hq), F32)
    wq_main = jnp.concatenate([wq, zq], axis=2).reshape(Q_LORA, MLA_HEADS * LANES)
    rot = jnp.stack([_rot_half_cols(wq[:, h, MLA_NOPE:]) for h in range(MLA_HEADS)], axis=1)
    wq_rot = jnp.concatenate([jnp.zeros((Q_LORA, MLA_HEADS, MLA_NOPE), F32), rot, zq], axis=2)
    wq_rot = wq_rot.reshape(Q_LORA, MLA_HEADS * LANES)
    wkv = w_kvb.reshape(KV_LORA, MLA_HEADS, MLA_NOPE + MLA_V)
    wkk = jnp.concatenate([wkv[:, :, :MLA_NOPE], jnp.zeros((KV_LORA, MLA_HEADS, LANES - MLA_NOPE), F32)], axis=2)
    wkk = wkk.reshape(KV_LORA, MLA_HEADS * LANES)
    wv = jnp.concatenate([wkv[:, :, MLA_NOPE:], jnp.zeros((KV_LORA, MLA_HEADS, LANES - MLA_V), F32)], axis=2)
    wv = wv.reshape(KV_LORA, MLA_HEADS * LANES)
    return tuple(a.astype(BF16) for a in (win, wq_main, wq_rot, wkk, wv))


def _mod_spec(n_lat_tiles, nb, d):
    return pl.BlockSpec((1, 1, d), lambda b, t: (jnp.where(t < n_lat_tiles, b, nb), 0, 0))


def _even_project(xu, norm_g, scale, shift, weights, q_norm_g, kv_norm_g, cos, sin, seq):
    nb, lt, d = xu.shape
    tm = 256
    win, wq, wqr, wkk, wv = weights
    n_lat = seq // tm
    row = lambda b, t: (b, t, 0)
    full = lambda a: pl.BlockSpec(a.shape, lambda b, t: (0,) * a.ndim)
    mod = _mod_spec(n_lat, nb, d)
    g2 = norm_g.reshape(1, d)
    qg2 = q_norm_g.reshape(1, Q_LORA)
    kvg2 = kv_norm_g.reshape(1, KV_LORA)
    wide = MLA_HEADS * LANES
    half = MLA_HEADS * MLA_V
    outs = [jax.ShapeDtypeStruct((nb, lt, wide), BF16), jax.ShapeDtypeStruct((nb, lt, wide), BF16)] + \
           [jax.ShapeDtypeStruct((nb, lt, wide), BF16)] + [jax.ShapeDtypeStruct((nb, lt, half), BF16)] * 3
    return pl.pallas_call(
        _even_proj_kernel,
        grid=(nb, lt // tm),
        in_specs=[pl.BlockSpec((1, tm, d), row), full(g2), mod, mod, full(win), full(qg2), full(kvg2),
                  full(wq), full(wqr), full(wkk), full(wv),
                  pl.BlockSpec((tm, LANES), lambda b, t: (t, 0)), pl.BlockSpec((tm, LANES), lambda b, t: (t, 0))],
        out_specs=[pl.BlockSpec((1, tm, wide), row), pl.BlockSpec((1, tm, wide), row)] +
                  [pl.BlockSpec((1, tm, wide), row)] + [pl.BlockSpec((1, tm, half), row)] * 3,
        out_shape=outs,
        compiler_params=_cparams("parallel", "parallel"),
        name="even_project",
    )(xu, g2, scale, shift, win, qg2, kvg2, wq, wqr, wkk, wv, cos, sin)


def _mla_pair_update(scores, vcs, carry):
    out = []
    for s, vc, (m, acc) in zip(scores, vcs, carry):
        m_new = jnp.maximum(m, jnp.max(s, axis=-1, keepdims=True))
        alpha = jnp.exp2((m - m_new) * MLA_EXP2_SCALE)
        p = jnp.exp2((s - m_new) * MLA_EXP2_SCALE)
        acc = alpha * acc + _dot(p.astype(BF16), vc)
        out.append((m_new, acc))
    return tuple(out)


def _mla_kernel(q_ref, k_ref, v_ref, o_ref, *, seq, ctx_len, tk):
    tq = q_ref.shape[1]
    n_lat_tiles = seq // tq
    t = pl.program_id(1)
    lane = lax.broadcasted_iota(jnp.int32, (tq, LANES), 1)

    def run(with_latent):
        for pair in range(MLA_HEADS // 2):
            sls = [slice((2 * pair + sub) * LANES, (2 * pair + sub + 1) * LANES) for sub in range(2)]
            qs = [q_ref[0, :, sl] for sl in sls]

            def scores(off, width):
                return tuple(_dot_nt(q, k_ref[0, pl.ds(off, width), sl]) for q, sl in zip(qs, sls))

            def values(off, width):
                return [v_ref[0, pl.ds(off, width), sl] for sl in sls]

            init = (jnp.full((tq, 1), NEG_BIG, F32), jnp.zeros((tq, LANES), F32))
            carry = (init, init)
            if with_latent:
                n_chunks = seq // tk

                def body(i, c):
                    cur, state = c
                    nxt = scores(pl.multiple_of((i + 1) * tk, tk), tk)
                    state = _mla_pair_update(cur, values(pl.multiple_of(i * tk, tk), tk), state)
                    return nxt, state

                cur, carry = lax.fori_loop(0, n_chunks - 1, body, (scores(0, tk), carry))
                ctx_scores = scores(seq, ctx_len)
                carry = _mla_pair_update(cur, values((n_chunks - 1) * tk, tk), carry)
            else:
                ctx_scores = scores(seq, ctx_len)
            carry = _mla_pair_update(ctx_scores, values(seq, ctx_len), carry)
            outs = [acc / acc[:, MLA_V:MLA_V + 1] for (_, acc) in carry]
            second = pltpu.roll(outs[1], MLA_V, 1)
            o_ref[0, :, pair * LANES:(pair + 1) * LANES] = jnp.where(lane < MLA_V, outs[0], second).astype(BF16)

    @pl.when(t < n_lat_tiles)
    def _():
        run(True)

    @pl.when(t >= n_lat_tiles)
    def _():
        run(False)


def _mla_attention(q, k, v, seq, ctx_len):
    nb, lt, wide = q.shape
    half = MLA_HEADS * MLA_V
    tq, tk = 256, 512
    kern = functools.partial(_mla_kernel, seq=seq, ctx_len=ctx_len, tk=tk)
    return pl.pallas_call(
        kern,
        grid=(nb, lt // tq),
        in_specs=[pl.BlockSpec((1, tq, wide), lambda b, t: (b, t, 0)),
                  pl.BlockSpec((1, lt, wide), lambda b, t: (b, 0, 0)),
                  pl.BlockSpec((1, lt, wide), lambda b, t: (b, 0, 0))],
        out_specs=pl.BlockSpec((1, tq, half), lambda b, t: (b, t, 0)),
        out_shape=jax.ShapeDtypeStruct((nb, lt, half), BF16),
        compiler_params=_cparams("parallel", "parallel"),
        name="mla_attention",
    )(q, k, v)


def _na_bias_tables(rpb, rows):
    kh = min(NA_WIN_H, rows)
    last_r0 = rows - NA_ROWS_PER_BLOCK
    tabs = []
    for r0 in (0, 2 * NA_ROWS_PER_BLOCK, last_r0):
        kstart = int(np.clip(r0 - kh // 2, 0, rows - NA_KEY_ROWS))
        r = r0 + np.arange(NA_ROWS_PER_BLOCK)
        rs = np.clip(r - kh // 2, 0, rows - kh)
        kr = kstart + np.arange(NA_KEY_ROWS)
        row_ok = (kr[None, :] >= rs[:, None]) & (kr[None, :] < rs[:, None] + kh)
        row_off = np.clip(kr[None, :] - r[:, None] + (NA_WIN_H - 1), 0, 2 * NA_WIN_H - 2)
        c = np.arange(GRID_W)
        cs = np.clip(c - NA_WIN_W // 2, 0, GRID_W - NA_WIN_W)
        col_ok = (c[None, :] >= cs[:, None]) & (c[None, :] < cs[:, None] + NA_WIN_W)
        col_off = np.clip(c[None, :] - c[:, None] + (NA_WIN_W - 1), 0, 2 * NA_WIN_W - 2)
        sel_r = jnp.asarray(row_off[..., None] == np.arange(2 * NA_WIN_H - 1), F32)
        sel_c = jnp.asarray(col_off[..., None] == np.arange(2 * NA_WIN_W - 1), F32)
        b = jnp.einsum('aeu,huv,cdv->haced', sel_r, rpb, sel_c, precision=lax.Precision.HIGHEST)
        ok = row_ok[:, None, :, None] & col_ok[None, :, None, :]
        b = jnp.where(jnp.asarray(ok)[None], b, NEG_BIG)
        tabs.append(b.reshape(rpb.shape[0], NA_ROWS_PER_BLOCK * GRID_W, NA_KEY_ROWS * GRID_W))
    return jnp.stack(tabs)


def _na_kernel(q_ref, k_ref, v_ref, bias_ref, o_ref, *, seq, ctx_len):
    tq = q_ref.shape[1]
    rows = seq // GRID_W
    n_lat_tiles = seq // tq
    nkeys = NA_KEY_ROWS * GRID_W
    t = pl.program_id(1)
    lane = lax.broadcasted_iota(jnp.int32, (tq, LANES), 1)
    kstart = jnp.clip(t * NA_ROWS_PER_BLOCK - NA_WIN_H // 2, 0, rows - NA_KEY_ROWS)
    koff = pl.multiple_of(kstart * GRID_W, GRID_W)

    def run(with_window):
        for pair in range(NA_HEADS // 2):
            sl = slice(pair * LANES, (pair + 1) * LANES)
            q2 = q_ref[0, :, sl].astype(F32)
            kc = k_ref[0, seq:seq + ctx_len, sl]
            vc = v_ref[0, seq:seq + ctx_len, sl]
            if with_window:
                kw = k_ref[0, pl.ds(koff, nkeys), sl]
                vw = v_ref[0, pl.ds(koff, nkeys), sl]
            outs = []
            for sub in range(2):
                keep = (lane < NA_HEAD_DIM) if sub == 0 else (lane >= NA_HEAD_DIM)
                q = jnp.where(keep, q2, 0.0).astype(BF16)
                s_c = _dot_nt(q, kc)
                m = jnp.max(s_c, axis=-1, keepdims=True)
                if with_window:
                    s_w = _dot_nt(q, kw) + bias_ref[0, 2 * pair + sub]
                    m = jnp.maximum(m, jnp.max(s_w, axis=-1, keepdims=True))
                p_c = jnp.exp(s_c - m)
                l = jnp.sum(p_c, axis=-1, keepdims=True)
                acc = _dot(p_c.astype(BF16), vc)
                if with_window:
                    p_w = jnp.exp(s_w - m)
                    l = l + jnp.sum(p_w, axis=-1, keepdims=True)
                    acc = acc + _dot(p_w.astype(BF16), vw)
                outs.append(acc / l)
            o_ref[0, :, sl] = jnp.where(lane < NA_HEAD_DIM, outs[0], outs[1]).astype(BF16)

    @pl.when(t < n_lat_tiles)
    def _():
        run(True)

    @pl.when(t >= n_lat_tiles)
    def _():
        run(False)


def _na_attention(q, k, v, bias, seq, ctx_len):
    nb, lt, w = q.shape
    tq = NA_ROWS_PER_BLOCK * GRID_W
    n_lat = seq // tq
    kern = functools.partial(_na_kernel, seq=seq, ctx_len=ctx_len)

    def variant(b, t):
        return (jnp.where(t == 0, 0, jnp.where(t >= n_lat - 1, 2, 1)), 0, 0, 0)

    return pl.pallas_call(
        kern,
        grid=(nb, lt // tq),
        in_specs=[pl.BlockSpec((1, tq, w), lambda b, t: (b, t, 0)),
                  pl.BlockSpec((1, lt, w), lambda b, t: (b, 0, 0)),
                  pl.BlockSpec((1, lt, w), lambda b, t: (b, 0, 0)),
                  pl.BlockSpec((1,) + bias.shape[1:], variant)],
        out_specs=pl.BlockSpec((1, tq, w), lambda b, t: (b, t, 0)),
        out_shape=jax.ShapeDtypeStruct((nb, lt, w), BF16),
        compiler_params=_cparams("parallel", "parallel"),
        name="na_attention",
    )(q, k, v, bias)


def _out_proj_kernel(x_ref, a_ref, b_ref, wa_ref, wb_ref, gate_ref, o_ref):
    y = _dot(a_ref[0], wa_ref[...]) + _dot(b_ref[0], wb_ref[...])
    o_ref[0] = x_ref[0] + gate_ref[0] * y


def _out_proj(xu, a, b, w_out, gate, seq):
    nb, lt, d = xu.shape
    tm = 256
    ka = a.shape[2]
    wa = w_out[:ka].astype(BF16)
    wb = w_out[ka:].astype(BF16)
    row = lambda bb, t: (bb, t, 0)
    full = lambda arr: pl.BlockSpec(arr.shape, lambda bb, t: (0,) * arr.ndim)
    return pl.pallas_call(
        _out_proj_kernel,
        grid=(nb, lt // tm),
        in_specs=[pl.BlockSpec((1, tm, d), row), pl.BlockSpec((1, tm, ka), row),
                  pl.BlockSpec((1, tm, b.shape[2]), row), full(wa), full(wb), _mod_spec(seq // tm, nb, d)],
        out_specs=pl.BlockSpec((1, tm, d), row),
        out_shape=jax.ShapeDtypeStruct(xu.shape, F32),
        compiler_params=_cparams("parallel", "parallel"),
        name="out_proj",
    )(xu, a, b, wa, wb, gate)


def _ffn_kernel(x_ref, g_ref, sc_ref, sh_ref, gate_ref, wg_ref, wu_ref, wd_ref, o_ref, h_scr, acc_scr):
    f = pl.program_id(2)

    @pl.when(f == 0)
    def _():
        h_scr[...] = _norm_mod(x_ref[0], g_ref[...], sc_ref[0], sh_ref[0]).astype(BF16)
        acc_scr[...] = jnp.zeros_like(acc_scr)

    h = h_scr[...]
    a = _dot(h, wg_ref[...])
    u = _dot(h, wu_ref[...])
    acc_scr[...] += _dot((a * jax.nn.sigmoid(a) * u).astype(BF16), wd_ref[...])

    @pl.when(f == pl.num_programs(2) - 1)
    def _():
        o_ref[0] = x_ref[0] + gate_ref[0] * acc_scr[...]


def _ffn(xu, norm_g, scale, shift, gate, w_gate, w_up, w_down, seq):
    nb, lt, d = xu.shape
    dff = w_gate.shape[1]
    tm = 256
    tf = dff
    row = lambda b, t, f: (b, t, 0)
    n_lat = seq // tm
    mod = pl.BlockSpec((1, 1, d), lambda b, t, f: (jnp.where(t < n_lat, b, nb), 0, 0))
    g2 = norm_g.reshape(1, d)
    once = pl.Buffered(1)
    return pl.pallas_call(
        _ffn_kernel,
        grid=(nb, lt // tm, dff // tf),
        in_specs=[pl.BlockSpec((1, tm, d), row), pl.BlockSpec((1, d), lambda b, t, f: (0, 0)), mod, mod, mod,
                  pl.BlockSpec((d, tf), lambda b, t, f: (0, f), pipeline_mode=once),
                  pl.BlockSpec((d, tf), lambda b, t, f: (0, f), pipeline_mode=once),
                  pl.BlockSpec((tf, d), lambda b, t, f: (f, 0), pipeline_mode=once)],
        out_specs=pl.BlockSpec((1, tm, d), row),
        out_shape=jax.ShapeDtypeStruct(xu.shape, F32),
        scratch_shapes=[pltpu.VMEM((tm, d), BF16), pltpu.VMEM((tm, d), F32)],
        compiler_params=_cparams("parallel", "parallel", "arbitrary"),
        name="dense_swiglu",
    )(xu, g2, scale, shift, gate, w_gate.astype(BF16), w_up.astype(BF16), w_down.astype(BF16))


def _s5_in_kernel(x_ref, g_ref, sc_ref, sh_ref, w_ref, o_ref):
    h = _norm_mod(x_ref[0], g_ref[...], sc_ref[0], sh_ref[0]).astype(BF16)
    u = _dot(h, w_ref[...])
    for k in range(o_ref.shape[0]):
        o_ref[k, 0] = u[:, k * LANES:(k + 1) * LANES]


def _s5_in(xu, norm_g, scale, shift, w_in, seq):
    nb, lt, d = xu.shape
    tm = 256
    w = w_in.astype(BF16)
    row = lambda b, t: (b, t, 0)
    mod = _mod_spec(seq // tm, nb, d)
    g2 = norm_g.reshape(1, d)
    return pl.pallas_call(
        _s5_in_kernel,
        grid=(nb, lt // tm),
        in_specs=[pl.BlockSpec((1, tm, d), row), pl.BlockSpec((1, d), lambda b, t: (0, 0)), mod, mod,
                  pl.BlockSpec(w.shape, lambda b, t: (0, 0))],
        out_specs=pl.BlockSpec((w.shape[1] // LANES, 1, tm, LANES), lambda b, t: (0, b, t, 0)),
        out_shape=jax.ShapeDtypeStruct((w.shape[1] // LANES, nb, lt, LANES), F32),
        compiler_params=_cparams("parallel", "parallel"),
        name="s5_in",
    )(xu, g2, scale, shift, w)


def _s5_operators(a_re, a_im, log_step, b_re, b_im, c_re, c_im):
    hp = lax.Precision.HIGHEST
    t_len = S5_CHUNK
    ops = []
    for direction in range(2):
        dt = jnp.exp(log_step[direction])[:, None]
        lre, lim = a_re[direction] * dt, a_im[direction] * dt
        decay = jnp.exp(lre)
        ab_re, ab_im = decay * jnp.cos(lim), decay * jnp.sin(lim)
        den = a_re[direction] ** 2 + a_im[direction] ** 2
        f_re = ((ab_re - 1) * a_re[direction] + ab_im * a_im[direction]) / den
        f_im = (ab_im * a_re[direction] - (ab_re - 1) * a_im[direction]) / den
        bb_re = f_re[..., None] * b_re[direction] - f_im[..., None] * b_im[direction]
        bb_im = f_re[..., None] * b_im[direction] + f_im[..., None] * b_re[direction]
        cr, ci = c_re[direction], c_im[direction]
        tau = jnp.arange(t_len + 1, dtype=F32)[:, None, None]
        pw = jnp.exp(tau * lre[None])
        pw_re, pw_im = pw * jnp.cos(tau * lim[None]), pw * jnp.sin(tau * lim[None])
        ab_b_re = pw_re[..., None] * bb_re[None] - pw_im[..., None] * bb_im[None]
        ab_b_im = pw_re[..., None] * bb_im[None] + pw_im[..., None] * bb_re[None]
        kk = (jnp.einsum('gip,tgpj->tgij', cr, ab_b_re, precision=hp)
              - jnp.einsum('gip,tgpj->tgij', ci, ab_b_im, precision=hp))
        ca_re = cr[None] * pw_re[:, :, None, :] - ci[None] * pw_im[:, :, None, :]
        ca_im = cr[None] * pw_im[:, :, None, :] + ci[None] * pw_re[:, :, None, :]
        s = np.arange(t_len)
        z_pow = (t_len - 1 - s) if direction == 0 else s
        c_pow = (s + 1) if direction == 0 else (t_len - s)
        ops.append((kk, [ab_b_re[z_pow], ab_b_im[z_pow]],
                    [ca_re[c_pow], -ca_im[c_pow]],
                    [pw_re[t_len], pw_im[t_len]]))
    lg = S5_LANE_GROUPS
    nblk = a_re.shape[1] // lg
    eye = jnp.eye(lg, dtype=BF16)
    width = t_len * LANES
    n_state = 4 * lg * S5_STATE
    kf, kb = ops[0][0], ops[1][0]
    kd = jnp.concatenate([kb[t_len - 1:0:-1], (kf[0] + kb[0])[None], kf[1:t_len]])
    kd = kd.astype(BF16).reshape(2 * t_len - 1, nblk, lg, S5_GROUP, S5_GROUP).transpose(1, 0, 4, 2, 3)
    d_blk = kd[:, :, None] * eye[:, None, :, None]
    d_cat = d_blk.reshape(nblk, 2 * t_len - 1, LANES, LANES).transpose(0, 2, 1, 3).reshape(nblk, LANES, -1)
    w_intra = jnp.stack([d_cat[:, :, (t_len - 1 - si) * LANES:(2 * t_len - 1 - si) * LANES] for si in range(t_len)],
                        axis=1).reshape(nblk, width, width)
    wz4 = jnp.stack(ops[0][1] + ops[1][1]).astype(BF16)
    wz4 = wz4.reshape(4, t_len, nblk, lg, S5_STATE, S5_GROUP).transpose(2, 1, 5, 0, 3, 4)
    w_z = (wz4[:, :, None] * eye[:, None, None, :, None]).reshape(nblk, width, n_state)
    wc4 = jnp.stack(ops[0][2] + ops[1][2]).astype(BF16)
    wc4 = wc4.reshape(4, t_len, nblk, lg, S5_GROUP, S5_STATE).transpose(2, 1, 4, 0, 3, 5)
    w_c_t = (wc4[:, :, None] * eye[:, None, None, :, None]).reshape(nblk, width, n_state)
    a_t = jnp.stack(ops[0][3] + ops[1][3]).reshape(4, nblk, lg * S5_STATE).transpose(1, 0, 2)
    return w_intra, w_z, w_c_t, a_t


def _s5_core_kernel(u_ref, wi_ref, wz_ref, wct_ref, at_ref, y_ref, z_scr, h_scr, *, n_ctx_chunks):
    n_chunks = z_scr.shape[0]
    n_lat = n_chunks - n_ctx_chunks
    w = z_scr.shape[1] // 4
    a = jnp.concatenate([u_ref[0, 0, pl.ds(s, n_chunks, stride=S5_CHUNK), :].astype(BF16)
                         for s in range(S5_CHUNK)], axis=1)
    z_scr[...] = _dot(a, wz_ref[0])
    ar_f, ai_f, ar_b, ai_b = [at_ref[0, k:k + 1, :] for k in range(4)]

    def step(k, carry):
        (fr, fi), (br, bi) = carry
        cf = jnp.where(k < n_ctx_chunks, n_lat + k, k - n_ctx_chunks)
        cb = n_chunks - 1 - k
        rf, rb = pl.ds(cf, 1), pl.ds(cb, 1)
        h_scr[rf, 0:w] = fr
        h_scr[rf, w:2 * w] = fi
        h_scr[rb, 2 * w:3 * w] = br
        h_scr[rb, 3 * w:4 * w] = bi
        fwd = (ar_f * fr - ai_f * fi + z_scr[rf, 0:w], ar_f * fi + ai_f * fr + z_scr[rf, w:2 * w])
        bwd = (ar_b * br - ai_b * bi + z_scr[rb, 2 * w:3 * w], ar_b * bi + ai_b * br + z_scr[rb, 3 * w:4 * w])
        return fwd, bwd

    zero = jnp.zeros((1, w), F32)
    lax.fori_loop(0, n_chunks, step, ((zero, zero), (zero, zero)))
    y = _dot(a, wi_ref[0]) + _dot_nt(h_scr[...].astype(BF16), wct_ref[0])
    for t in range(S5_CHUNK):
        y_ref[0, 0, pl.ds(t, n_chunks, stride=S5_CHUNK), :] = y[:, t * LANES:(t + 1) * LANES]


def _s5_core(u_blocks, ops, n_ctx_chunks):
    nblk, nb, lt, lanes = u_blocks.shape
    n_chunks = lt // S5_CHUNK
    w_intra, w_z, w_c_t, a_t = ops
    kern = functools.partial(_s5_core_kernel, n_ctx_chunks=n_ctx_chunks)
    once = lambda a: pl.BlockSpec((1,) + a.shape[1:], lambda g, b: (g, 0, 0), pipeline_mode=pl.Buffered(1))
    tok = pl.BlockSpec((1, 1, lt, lanes), lambda g, b: (g, b, 0, 0))
    return pl.pallas_call(
        kern,
        grid=(nblk, nb),
        in_specs=[tok, once(w_intra), once(w_z), once(w_c_t),
                  pl.BlockSpec((1,) + a_t.shape[1:], lambda g, b: (g, 0, 0))],
        out_specs=tok,
        out_shape=jax.ShapeDtypeStruct(u_blocks.shape, F32),
        scratch_shapes=[pltpu.VMEM((n_chunks, w_z.shape[2]), F32), pltpu.VMEM((n_chunks, w_c_t.shape[2]), F32)],
        compiler_params=_cparams("parallel", "parallel"),
        name="s5_core",
    )(u_blocks, w_intra, w_z, w_c_t, a_t)


def _s5_out_kernel(x_ref, u_ref, y_ref, d_ref, w_ref, gate_ref, o_ref):
    d = x_ref.shape[2]
    u = jnp.concatenate([u_ref[k, 0] for k in range(u_ref.shape[0])], axis=1)
    y = jnp.concatenate([y_ref[k, 0] for k in range(y_ref.shape[0])], axis=1)
    y = u * d_ref[...] + y
    z = _dot(jax.nn.gelu(y).astype(BF16), w_ref[...])
    o_ref[0] = x_ref[0] + gate_ref[0] * (z[:, :d] * jax.nn.sigmoid(z[:, d:]))


def _s5_out(x, u_blocks, y_blocks, d_skip, w_glu, gate):
    nb, seq, d = x.shape
    tm = 256
    nblk, _, _, lanes = u_blocks.shape
    w = nblk * lanes
    row = lambda b, t: (b, t, 0)
    blk = pl.BlockSpec((nblk, 1, tm, lanes), lambda b, t: (0, b, t, 0))
    wg = w_glu.astype(BF16)
    return pl.pallas_call(
        _s5_out_kernel,
        grid=(nb, seq // tm),
        in_specs=[pl.BlockSpec((1, tm, d), row), blk, blk,
                  pl.BlockSpec((1, w), lambda b, t: (0, 0)), pl.BlockSpec(wg.shape, lambda b, t: (0, 0)),
                  pl.BlockSpec((1, 1, d), lambda b, t: (b, 0, 0))],
        out_specs=pl.BlockSpec((1, tm, d), row),
        out_shape=jax.ShapeDtypeStruct(x.shape, F32),
        compiler_params=_cparams("parallel", "parallel"),
        name="s5_out",
    )(x, u_blocks, y_blocks, d_skip.reshape(1, w), wg, gate)


R_IDX, R_WT, R_RANK = 0, TOP_K, 2 * TOP_K


def _router_kernel(x_ref, g_ref, sc_ref, sh_ref, w_ref, tri_ref, rec_ref, cnt_ref, base_scr):
    @pl.when((pl.program_id(0) == 0) & (pl.program_id(1) == 0))
    def _():
        base_scr[...] = jnp.zeros_like(base_scr)

    t = _norm_mod(x_ref[0], g_ref[...], sc_ref[0], sh_ref[0]).astype(BF16)
    logits = _dot(t, w_ref[...])
    lane = lax.broadcasted_iota(jnp.int32, logits.shape, 1)
    big = jnp.int32(LANES)
    l1 = jnp.where(lane < N_EXPERTS, logits, -jnp.inf)
    v1 = jnp.max(l1, axis=-1, keepdims=True)
    i1 = jnp.min(jnp.where(l1 == v1, lane, big), axis=-1, keepdims=True)
    l2 = jnp.where(lane == i1, -jnp.inf, l1)
    v2 = jnp.max(l2, axis=-1, keepdims=True)
    i2 = jnp.min(jnp.where(l2 == v2, lane, big), axis=-1, keepdims=True)
    e2 = jnp.exp(v2 - v1)
    w1 = 1.0 / (1.0 + e2)
    w2 = e2 / (1.0 + e2)
    hit = jnp.where((lane == i1) | (lane == i2), 1.0, 0.0)
    before = _dot(tri_ref[...], hit.astype(BF16)) + base_scr[0:1, :]
    r1 = jnp.sum(jnp.where(lane == i1, before, 0.0), axis=-1, keepdims=True)
    r2 = jnp.sum(jnp.where(lane == i2, before, 0.0), axis=-1, keepdims=True)
    base_scr[...] = base_scr[...] + jnp.sum(hit, axis=0, keepdims=True)
    cnt_ref[...] = base_scr[...]
    rec = jnp.zeros(logits.shape, F32)
    for k, val in enumerate((i1.astype(F32), i2.astype(F32), w1, w2, r1, r2)):
        rec = jnp.where(lane == k, val, rec)
    rec_ref[0] = rec


def _router(x, norm_g, scale, shift, w_router):
    nb, seq, d = x.shape
    tm = 256
    wr = jnp.zeros((d, LANES), F32).at[:, :N_EXPERTS].set(w_router).astype(BF16)
    tri = jnp.asarray(np.tril(np.ones((tm, tm), np.float32), -1), BF16)
    row = lambda b, t: (b, t, 0)
    mod = pl.BlockSpec((1, 1, d), lambda b, t: (b, 0, 0))
    g2 = norm_g.reshape(1, d)
    return pl.pallas_call(
        _router_kernel,
        grid=(nb, seq // tm),
        in_specs=[pl.BlockSpec((1, tm, d), row), pl.BlockSpec((1, d), lambda b, t: (0, 0)), mod, mod,
                  pl.BlockSpec(wr.shape, lambda b, t: (0, 0)), pl.BlockSpec(tri.shape, lambda b, t: (0, 0))],
        out_specs=[pl.BlockSpec((1, tm, LANES), row), pl.BlockSpec((8, LANES), lambda b, t: (0, 0))],
        out_shape=[jax.ShapeDtypeStruct((nb, seq, LANES), F32), jax.ShapeDtypeStruct((8, LANES), F32)],
        scratch_shapes=[pltpu.VMEM((8, LANES), F32)],
        compiler_params=_cparams("arbitrary", "arbitrary"),
        name="router",
    )(x, g2, scale, shift, wr, tri)


def _row_copies_wait(src_ref, dst_ref, sem):
    pltpu.make_async_copy(src_ref, dst_ref, sem).wait()


def _dispatch_kernel(dest_ref, fill_ref, x_ref, g_ref, sc_ref, sh_ref, xs_ref, tbuf, zbuf, sem, zsem):
    tm = x_ref.shape[1]
    ztm = zbuf.shape[0]
    step = pl.program_id(0) * pl.num_programs(1) + pl.program_id(1)
    n_steps = pl.num_programs(0) * pl.num_programs(1)
    slot = step % 2

    def drain(s):
        for _ in range(TOP_K):
            _row_copies_wait(tbuf.at[s], xs_ref.at[pl.ds(0, tm)], sem.at[s])

    @pl.when(step == 0)
    def _():
        zbuf[...] = jnp.zeros_like(zbuf)

        def fill(start):
            def body(i, carry):
                @pl.when(fill_ref[i] > 0)
                def _():
                    cp = pltpu.make_async_copy(zbuf, xs_ref.at[pl.ds(pl.multiple_of(i * ztm, ztm), ztm)], zsem)
                    if start:
                        cp.start()
                    else:
                        cp.wait()
                return carry
            lax.fori_loop(0, fill_ref.shape[0], body, 0)

        fill(True)
        fill(False)

    @pl.when(step >= 2)
    def _():
        drain(slot)

    tbuf[slot] = _norm_mod(x_ref[0], g_ref[...], sc_ref[0], sh_ref[0])
    base = step * (tm * TOP_K)

    def body(r, carry):
        for k in range(TOP_K):
            dst = dest_ref[base + r * TOP_K + k]
            pltpu.make_async_copy(tbuf.at[slot, pl.ds(r, 1)], xs_ref.at[pl.ds(dst, 1)], sem.at[slot]).start()
        return carry

    lax.fori_loop(0, tm, body, 0, unroll=8)

    @pl.when(step == n_steps - 1)
    def _():
        drain(slot)

        @pl.when(n_steps > 1)
        def _():
            drain(1 - slot)


def _dispatch(x, norm_g, scale, shift, dest, tile_fill, n_rows, expert_tm):
    nb, seq, d = x.shape
    tm = 256
    row = lambda b, t, dr, fr: (b, t, 0)
    mod = pl.BlockSpec((1, 1, d), lambda b, t, dr, fr: (b, 0, 0))
    g2 = norm_g.reshape(1, d)
    grid_spec = pltpu.PrefetchScalarGridSpec(
        num_scalar_prefetch=2,
        grid=(nb, seq // tm),
        in_specs=[pl.BlockSpec((1, tm, d), row), pl.BlockSpec((1, d), lambda b, t, dr, fr: (0, 0)), mod, mod],
        out_specs=pl.BlockSpec(memory_space=pl.ANY),
        scratch_shapes=[pltpu.VMEM((2, tm, d), F32), pltpu.VMEM((expert_tm, d), F32),
                        pltpu.SemaphoreType.DMA((2,)), pltpu.SemaphoreType.DMA(())],
    )
    return pl.pallas_call(
        _dispatch_kernel,
        grid_spec=grid_spec,
        out_shape=jax.ShapeDtypeStruct((n_rows, d), F32),
        compiler_params=_cparams("arbitrary", "arbitrary"),
        name="dispatch",
    )(dest, tile_fill, x, g2, scale, shift)


def _expert_kernel(te_ref, tr_ref, x_ref, wg_ref, wu_ref, wd_ref, o_ref, xb_scr, acc_scr):
    i = pl.program_id(0)
    f = pl.program_id(1)
    nf = pl.num_programs(1)
    live = tr_ref[i] > 0

    @pl.when(jnp.logical_not(live) & (f == nf - 1))
    def _():
        o_ref[...] = jnp.zeros_like(o_ref)

    @pl.when(live)
    def _():
        @pl.when(f == 0)
        def _():
            xb_scr[...] = x_ref[...].astype(BF16)
            acc_scr[...] = jnp.zeros_like(acc_scr)

        x = xb_scr[...]
        a = _dot(x, wg_ref[0])
        u = _dot(x, wu_ref[0])
        acc_scr[...] += _dot((a * jax.nn.sigmoid(a) * u).astype(BF16), wd_ref[0])

        @pl.when(f == nf - 1)
        def _():
            o_ref[...] = acc_scr[...]


def _experts(xs, tile_expert, tile_rows, w_gate, w_up, w_down, tm, tf):
    n_rows, d = xs.shape
    dff = w_gate.shape[2]
    nf = dff // tf

    def fsel(i, f, te, tr):
        return jnp.where(tr[i] > 0, f, nf - 1)

    grid_spec = pltpu.PrefetchScalarGridSpec(
        num_scalar_prefetch=2,
        grid=(n_rows // tm, nf),
        in_specs=[pl.BlockSpec((tm, d), lambda i, f, te, tr: (i, 0)),
                  pl.BlockSpec((1, d, tf), lambda i, f, te, tr: (te[i], 0, fsel(i, f, te, tr))),
                  pl.BlockSpec((1, d, tf), lambda i, f, te, tr: (te[i], 0, fsel(i, f, te, tr))),
                  pl.BlockSpec((1, tf, d), lambda i, f, te, tr: (te[i], fsel(i, f, te, tr), 0))],
        out_specs=pl.BlockSpec((tm, d), lambda i, f, te, tr: (i, 0)),
        scratch_shapes=[pltpu.VMEM((tm, d), BF16), pltpu.VMEM((tm, d), F32)],
    )
    return pl.pallas_call(
        _expert_kernel,
        grid_spec=grid_spec,
        out_shape=jax.ShapeDtypeStruct((n_rows, d), F32),
        compiler_params=_cparams("arbitrary", "arbitrary"),
        name="experts",
    )(tile_expert, tile_rows, xs, w_gate, w_up, w_down)


def _combine_kernel(dest_ref, x_ref, rec_ref, gate_ref, g_ref, ys_ref, o_ref, ybuf, sem):
    tm = x_ref.shape[1]
    step = pl.program_id(0) * pl.num_programs(1) + pl.program_id(1)
    n_steps = pl.num_programs(0) * pl.num_programs(1)
    slot = step % 2

    def issue(st, sl):
        base = st * (tm * TOP_K)

        def body(r, carry):
            for k in range(TOP_K):
                src = dest_ref[base + r * TOP_K + k]
                pltpu.make_async_copy(ys_ref.at[pl.ds(src, 1)], ybuf.at[sl, k, pl.ds(r, 1)], sem.at[sl]).start()
            return carry

        lax.fori_loop(0, tm, body, 0, unroll=8)

    @pl.when(step == 0)
    def _():
        issue(step, slot)

    @pl.when(step + 1 < n_steps)
    def _():
        issue(step + 1, 1 - slot)

    for k in range(TOP_K):
        _row_copies_wait(ys_ref.at[pl.ds(0, tm)], ybuf.at[slot, k], sem.at[slot])

    rec = rec_ref[0]
    y = rec[:, R_WT:R_WT + 1] * ybuf[slot, 0] + rec[:, R_WT + 1:R_WT + 2] * ybuf[slot, 1]
    x = x_ref[0] + gate_ref[0] * y
    o_ref[0] = _rms(x) * g_ref[...]


def _combine(x, rec, gate, final_g, ys, dest):
    nb, seq, d = x.shape
    tm = 256
    row = lambda b, t, dr: (b, t, 0)
    grid_spec = pltpu.PrefetchScalarGridSpec(
        num_scalar_prefetch=1,
        grid=(nb, seq // tm),
        in_specs=[pl.BlockSpec((1, tm, d), row), pl.BlockSpec((1, tm, LANES), row),
                  pl.BlockSpec((1, 1, d), lambda b, t, dr: (b, 0, 0)), pl.BlockSpec((1, d), lambda b, t, dr: (0, 0)),
                  pl.BlockSpec(memory_space=pl.ANY)],
        out_specs=pl.BlockSpec((1, tm, d), row),
        scratch_shapes=[pltpu.VMEM((2, TOP_K, tm, d), F32), pltpu.SemaphoreType.DMA((2,))],
    )
    return pl.pallas_call(
        _combine_kernel,
        grid_spec=grid_spec,
        out_shape=jax.ShapeDtypeStruct(x.shape, F32),
        compiler_params=_cparams("arbitrary", "arbitrary"),
        name="combine_final_norm",
    )(dest, x, rec, gate, final_g.reshape(1, d), ys)


def _moe(x, norm_g, scale, shift, gate, w_router, w_gate, w_up, w_down, final_g):
    nb, seq, d = x.shape
    n_tok = nb * seq
    tm, tf = 512, w_gate.shape[2] // 2
    rec, cnt = _router(x, norm_g, scale, shift, w_router)
    counts = cnt[0, :N_EXPERTS].astype(jnp.int32)
    padded = ((counts + tm - 1) // tm) * tm
    pend = jnp.cumsum(padded)
    pstart = pend - padded
    idx = rec[:, :, R_IDX:R_IDX + TOP_K].astype(jnp.int32)
    rank = rec[:, :, R_RANK:R_RANK + TOP_K].astype(jnp.int32)
    seg = jnp.sum(jnp.where(idx[..., None] == jnp.arange(N_EXPERTS), pstart, 0), axis=-1)
    dest = (seg + rank).reshape(n_tok * TOP_K)
    n_rows = n_tok * TOP_K + N_EXPERTS * tm
    tile_start = jnp.arange(n_rows // tm, dtype=jnp.int32) * tm
    tile_expert = jnp.sum((tile_start[:, None] >= pend[None, :]).astype(jnp.int32), axis=1)
    live = tile_expert < N_EXPERTS
    last_live = jnp.max(jnp.where(live, tile_expert, 0))
    tile_expert = jnp.where(live, tile_expert, last_live)
    seg_rows = jnp.sum(jnp.where(tile_expert[:, None] == jnp.arange(N_EXPERTS), (pstart + counts)[None, :], 0), axis=1)
    tile_rows = jnp.where(live, jnp.clip(seg_rows - tile_start, 0, tm), 0).astype(jnp.int32)
    tile_fill = (tile_rows < tm).astype(jnp.int32)

    xs = _dispatch(x, norm_g, scale, shift, dest, tile_fill, n_rows, tm)
    ys = _experts(xs, tile_expert.astype(jnp.int32), tile_rows,
                  w_gate.astype(BF16), w_up.astype(BF16), w_down.astype(BF16), tm, tf)
    return _combine(x, rec, gate, final_g, ys, dest)


def kernel(x, c, ctx, c_ctx, mod_w, mod_b, norm1_g, norm2_g, ev_w_in, ev_q_norm_g, ev_w_qb, ev_kv_norm_g, ev_w_kvb, ev_na_rpb, ev_w_out, ev_ffn_w_gate, ev_ffn_w_up, ev_ffn_w_down, od_w_in, od_a_re, od_a_im, od_log_step, od_b_re, od_b_im, od_c_re, od_c_im, od_d, od_w_glu, moe_w_router, moe_w_gate, moe_w_up, moe_w_down, final_g):
    nb, seq, d = x.shape
    ctx_len = ctx.shape[1]
    assert mod_w.shape[0] == 2 and nb < 8
    assert seq % (NA_ROWS_PER_BLOCK * GRID_W) == 0 and ctx_len == NA_ROWS_PER_BLOCK * GRID_W

    cond = jnp.zeros((8, d), F32).at[:nb].set(c).at[nb].set(c_ctx)
    mods = _adaln(cond, mod_w, mod_b)

    def mod_parts(layer):
        return [mods[layer, :, i * d:(i + 1) * d].reshape(8, 1, d) for i in range(N_MOD)]

    xu = jnp.concatenate([x, ctx], axis=1)

    sh1, sc1, g1, sh2, sc2, g2 = mod_parts(0)
    cos, sin = _rope_tables(seq, ctx_len)
    weights = _even_weights(ev_w_in[0], ev_w_qb[0], ev_w_kvb[0])
    q, k, v, nq, nk, nv = _even_project(xu, norm1_g[0], sc1, sh1, weights, ev_q_norm_g[0], ev_kv_norm_g[0],
                                        cos, sin, seq)
    mla = _mla_attention(q, k, v, seq, ctx_len)
    bias = _na_bias_tables(ev_na_rpb[0], seq // GRID_W)
    na = _na_attention(nq, nk, nv, bias, seq, ctx_len)
    xu = _out_proj(xu, mla, na, ev_w_out[0], g1, seq)
    xu = _ffn(xu, norm2_g[0], sc2, sh2, g2, ev_ffn_w_gate[0], ev_ffn_w_up[0], ev_ffn_w_down[0], seq)

    sh1, sc1, g1, sh2, sc2, g2 = mod_parts(1)
    u = _s5_in(xu, norm1_g[1], sc1, sh1, od_w_in[0], seq)
    ops = _s5_operators(od_a_re[0], od_a_im[0], od_log_step[0], od_b_re[0], od_b_im[0], od_c_re[0], od_c_im[0])
    y = _s5_core(u, ops, ctx_len // S5_CHUNK)
    xl = xu[:, :seq]
    xl = _s5_out(xl, u, y, od_d[0], od_w_glu[0], g1[:nb])
    return _moe(xl, norm2_g[1], sc2[:nb], sh2[:nb], g2[:nb], moe_w_router[0], moe_w_gate[0], moe_w_up[0],
                moe_w_down[0], final_g)
```

```python
import functools
import math

import numpy as np
import jax
import jax.numpy as jnp
from jax import lax
from jax.experimental import pallas as pl
from jax.experimental.pallas import tpu as pltpu

F32 = jnp.float32
BF16 = jnp.bfloat16

LANES = 128
VMEM_LIMIT_BYTES = 52 * 1024 * 1024

NORM_EPS = 1e-6
ROPE_BASE = 10000.0
GRID_W = 64
N_MOD = 6

MLA_HEADS = 8
MLA_NOPE = 64
MLA_ROPE = 32
MLA_V = 64
Q_LORA = 384
KV_LORA = 256
MLA_SCALE = (MLA_NOPE + MLA_ROPE) ** -0.5
MLA_EXP2_SCALE = MLA_SCALE * math.log2(math.e)

NA_HEADS = 8
NA_HEAD_DIM = 64
NA_WIN_H = 8
NA_WIN_W = 16
NA_SCALE = NA_HEAD_DIM ** -0.5
NA_ROWS_PER_BLOCK = 4
NA_KEY_ROWS = NA_ROWS_PER_BLOCK + NA_WIN_H - 1

S5_GROUP = 16
S5_STATE = 64
S5_CHUNK = 16
S5_LANE_GROUPS = LANES // S5_GROUP

N_EXPERTS = 8
TOP_K = 2

NEG_BIG = -1e30


def _cparams(*sem):
    return pltpu.CompilerParams(dimension_semantics=sem, vmem_limit_bytes=VMEM_LIMIT_BYTES)


def _rms(x):
    return x * lax.rsqrt(jnp.mean(x * x, axis=-1, keepdims=True) + NORM_EPS)


def _norm_mod(x, g, scale, shift):
    return (_rms(x) * g) * (1 + scale) + shift


def _dot(a, b):
    return jnp.dot(a, b, preferred_element_type=F32)


def _dot_nt(a, b):
    return lax.dot_general(a, b, (((1,), (1,)), ((), ())), preferred_element_type=F32)


def _adaln_kernel(c_ref, w_ref, b_ref, o_ref):
    c = c_ref[...]
    s = (c * jax.nn.sigmoid(c)).astype(BF16)
    o_ref[0] = _dot(s, w_ref[0].astype(BF16)) + b_ref[0]


def _adaln(cond, mod_w, mod_b):
    nl, d, n = mod_w.shape
    tn = 1536
    return pl.pallas_call(
        _adaln_kernel,
        grid=(nl, n // tn),
        in_specs=[pl.BlockSpec((8, d), lambda l, j: (0, 0)),
                  pl.BlockSpec((1, d, tn), lambda l, j: (l, 0, j)),
                  pl.BlockSpec((1, 1, tn), lambda l, j: (l, 0, j))],
        out_specs=pl.BlockSpec((1, 8, tn), lambda l, j: (l, 0, j)),
        out_shape=jax.ShapeDtypeStruct((nl, 8, n), F32),
        compiler_params=_cparams("parallel", "parallel"),
        name="adaln",
    )(cond, mod_w, mod_b.reshape(nl, 1, n))


def _even_proj_kernel(x_ref, g_ref, sc_ref, sh_ref, win_ref, qg_ref, kvg_ref, wq_ref, wqr_ref, wkk_ref,
                      wkv_ref, cos_ref, sin_ref, q_ref, k_ref, v_ref, nq_ref, nk_ref, nv_ref):
    h = _norm_mod(x_ref[0], g_ref[...], sc_ref[0], sh_ref[0]).astype(BF16)
    p = _dot(h, win_ref[...])
    c0, c1, c2, c3 = Q_LORA, Q_LORA + KV_LORA, Q_LORA + KV_LORA + LANES, Q_LORA + KV_LORA + 2 * LANES
    cqn = (_rms(p[:, :c0]) * qg_ref[...]).astype(BF16)
    ckvn = (_rms(p[:, c0:c1]) * kvg_ref[...]).astype(BF16)
    cos = cos_ref[...]
    sin = sin_ref[...]
    kr = p[:, c1:c2] * cos + p[:, c2:c3] * sin
    qa = _dot(cqn, wq_ref[...])
    qb = _dot(cqn, wqr_ref[...])
    kk = _dot(ckvn, wkk_ref[...])
    for hd in range(MLA_HEADS):
        sl = slice(hd * LANES, (hd + 1) * LANES)
        q_ref[0, :, sl] = ((qa[:, sl] * cos + qb[:, sl] * sin) * MLA_EXP2_SCALE).astype(BF16)
        k_ref[0, :, sl] = (kk[:, sl] + kr).astype(BF16)
    vlane = lax.broadcasted_iota(jnp.int32, (1, MLA_HEADS * LANES), 1) & (LANES - 1)
    v_ref[0] = (_dot(ckvn, wkv_ref[...]) + jnp.where(vlane == MLA_V, 1.0, 0.0)).astype(BF16)
    w = NA_HEADS * NA_HEAD_DIM
    nq_ref[0] = (p[:, c3:c3 + w] * NA_SCALE).astype(BF16)
    nk_ref[0] = p[:, c3 + w:c3 + 2 * w].astype(BF16)
    nv_ref[0] = p[:, c3 + 2 * w:c3 + 3 * w].astype(BF16)


def _rot_half_cols(w):
    q = MLA_ROPE // 4
    return jnp.concatenate([-w[:, q:2 * q], w[:, :q], -w[:, 3 * q:], w[:, 2 * q:3 * q]], axis=1)


def _rope_tables(seq, ctx_len):
    q = MLA_ROPE // 4
    t = np.arange(seq)
    inv_freq = np.float32(ROPE_BASE) ** (-np.arange(q, dtype=np.float32) / np.float32(q))
    ang_r = (t // GRID_W).astype(np.float32)[:, None] * inv_freq[None, :]
    ang_c = (t % GRID_W).astype(np.float32)[:, None] * inv_freq[None, :]
    cos = np.ones((seq + ctx_len, LANES), np.float32)
    sin = np.zeros((seq + ctx_len, LANES), np.float32)
    cos[:seq, MLA_NOPE:MLA_NOPE + MLA_ROPE] = np.concatenate([np.cos(ang_r)] * 2 + [np.cos(ang_c)] * 2, axis=1)
    sin[:seq, MLA_NOPE:MLA_NOPE + MLA_ROPE] = np.concatenate([np.sin(ang_r)] * 2 + [np.sin(ang_c)] * 2, axis=1)
    return jnp.asarray(cos), jnp.asarray(sin)


def _even_weights(w_in, w_qb, w_kvb):
    d = w_in.shape[0]
    c1 = Q_LORA + KV_LORA
    wkr = w_in[:, c1:c1 + MLA_ROPE]
    pad = lambda w: jnp.zeros((d, LANES), F32).at[:, MLA_NOPE:MLA_NOPE + MLA_ROPE].set(w)
    win = jnp.concatenate([w_in[:, :c1], pad(wkr), pad(_rot_half_cols(wkr)), w_in[:, c1 + MLA_ROPE:]], axis=1)
    hq = MLA_NOPE + MLA_ROPE
    wq = w_qb.reshape(Q_LORA, MLA_HEADS, hq)
    zq = jnp.zeros((Q_LORA, MLA_HEADS, LANES - hq), F32)
    wq_main = jnp.concatenate([wq, zq], axis=2).reshape(Q_LORA, MLA_HEADS * LANES)
    rot = jnp.stack([_rot_half_cols(wq[:, h, MLA_NOPE:]) for h in range(MLA_HEADS)], axis=1)
    wq_rot = jnp.concatenate([jnp.zeros((Q_LORA, MLA_HEADS, MLA_NOPE), F32), rot, zq], axis=2)
    wq_rot = wq_rot.reshape(Q_LORA, MLA_HEADS * LANES)
    wkv = w_kvb.reshape(KV_LORA, MLA_HEADS, MLA_NOPE + MLA_V)
    wkk = jnp.concatenate([wkv[:, :, :MLA_NOPE], jnp.zeros((KV_LORA, MLA_HEADS, LANES - MLA_NOPE), F32)], axis=2)
    wkk = wkk.reshape(KV_LORA, MLA_HEADS * LANES)
    wv = jnp.concatenate([wkv[:, :, MLA_NOPE:], jnp.zeros((KV_LORA, MLA_HEADS, LANES - MLA_V), F32)], axis=2)
    wv = wv.reshape(KV_LORA, MLA_HEADS * LANES)
    return tuple(a.astype(BF16) for a in (win, wq_main, wq_rot, wkk, wv))


def _mod_spec(n_lat_tiles, nb, d):
    return pl.BlockSpec((1, 1, d), lambda b, t: (jnp.where(t < n_lat_tiles, b, nb), 0, 0))


def _even_project(xu, norm_g, scale, shift, weights, q_norm_g, kv_norm_g, cos, sin, seq):
    nb, lt, d = xu.shape
    tm = 256
    win, wq, wqr, wkk, wv = weights
    n_lat = seq // tm
    row = lambda b, t: (b, t, 0)
    full = lambda a: pl.BlockSpec(a.shape, lambda b, t: (0,) * a.ndim)
    mod = _mod_spec(n_lat, nb, d)
    g2 = norm_g.reshape(1, d)
    qg2 = q_norm_g.reshape(1, Q_LORA)
    kvg2 = kv_norm_g.reshape(1, KV_LORA)
    wide = MLA_HEADS * LANES
    half = MLA_HEADS * MLA_V
    outs = [jax.ShapeDtypeStruct((nb, lt, wide), BF16), jax.ShapeDtypeStruct((nb, lt, wide), BF16)] + \
           [jax.ShapeDtypeStruct((nb, lt, wide), BF16)] + [jax.ShapeDtypeStruct((nb, lt, half), BF16)] * 3
    return pl.pallas_call(
        _even_proj_kernel,
        grid=(nb, lt // tm),
        in_specs=[pl.BlockSpec((1, tm, d), row), full(g2), mod, mod, full(win), full(qg2), full(kvg2),
                  full(wq), full(wqr), full(wkk), full(wv),
                  pl.BlockSpec((tm, LANES), lambda b, t: (t, 0)), pl.BlockSpec((tm, LANES), lambda b, t: (t, 0))],
        out_specs=[pl.BlockSpec((1, tm, wide), row), pl.BlockSpec((1, tm, wide), row)] +
                  [pl.BlockSpec((1, tm, wide), row)] + [pl.BlockSpec((1, tm, half), row)] * 3,
        out_shape=outs,
        compiler_params=_cparams("parallel", "parallel"),
        name="even_project",
    )(xu, g2, scale, shift, win, qg2, kvg2, wq, wqr, wkk, wv, cos, sin)


def _mla_pair_update(scores, vcs, carry):
    out = []
    for s, vc, (m, acc) in zip(scores, vcs, carry):
        m_new = jnp.maximum(m, jnp.max(s, axis=-1, keepdims=True))
        alpha = jnp.exp2(m - m_new)
        p = jnp.exp2(s - m_new)
        acc = alpha * acc + _dot(p.astype(BF16), vc)
        out.append((m_new, acc))
    return tuple(out)


def _mla_kernel(q_ref, k_ref, v_ref, o_ref, *, seq, ctx_len, tk):
    tq = q_ref.shape[1]
    n_lat_tiles = seq // tq
    t = pl.program_id(1)
    lane = lax.broadcasted_iota(jnp.int32, (tq, LANES), 1)

    def run(with_latent):
        for pair in range(MLA_HEADS // 2):
            sls = [slice((2 * pair + sub) * LANES, (2 * pair + sub + 1) * LANES) for sub in range(2)]
            qs = [q_ref[0, :, sl] for sl in sls]

            def scores(off, width):
                return tuple(_dot_nt(q, k_ref[0, pl.ds(off, width), sl]) for q, sl in zip(qs, sls))

            def values(off, width):
                return [v_ref[0, pl.ds(off, width), sl] for sl in sls]

            init = (jnp.full((tq, 1), NEG_BIG, F32), jnp.zeros((tq, LANES), F32))
            carry = (init, init)
            if with_latent:
                n_chunks = seq // tk

                def body(i, c):
                    cur, state = c
                    nxt = scores(pl.multiple_of((i + 1) * tk, tk), tk)
                    state = _mla_pair_update(cur, values(pl.multiple_of(i * tk, tk), tk), state)
                    return nxt, state

                cur, carry = lax.fori_loop(0, n_chunks - 1, body, (scores(0, tk), carry))
                ctx_scores = scores(seq, ctx_len)
                carry = _mla_pair_update(cur, values((n_chunks - 1) * tk, tk), carry)
            else:
                ctx_scores = scores(seq, ctx_len)
            carry = _mla_pair_update(ctx_scores, values(seq, ctx_len), carry)
            outs = [acc / acc[:, MLA_V:MLA_V + 1] for (_, acc) in carry]
            second = pltpu.roll(outs[1], MLA_V, 1)
            o_ref[0, :, pair * LANES:(pair + 1) * LANES] = jnp.where(lane < MLA_V, outs[0], second).astype(BF16)

    @pl.when(t < n_lat_tiles)
    def _():
        run(True)

    @pl.when(t >= n_lat_tiles)
    def _():
        run(False)


def _mla_attention(q, k, v, seq, ctx_len):
    nb, lt, wide = q.shape
    half = MLA_HEADS * MLA_V
    tq, tk = 256, 512
    kern = functools.partial(_mla_kernel, seq=seq, ctx_len=ctx_len, tk=tk)
    return pl.pallas_call(
        kern,
        grid=(nb, lt // tq),
        in_specs=[pl.BlockSpec((1, tq, wide), lambda b, t: (b, t, 0)),
                  pl.BlockSpec((1, lt, wide), lambda b, t: (b, 0, 0)),
                  pl.BlockSpec((1, lt, wide), lambda b, t: (b, 0, 0))],
        out_specs=pl.BlockSpec((1, tq, half), lambda b, t: (b, t, 0)),
        out_shape=jax.ShapeDtypeStruct((nb, lt, half), BF16),
        compiler_params=_cparams("parallel", "parallel"),
        name="mla_attention",
    )(q, k, v)


def _na_bias_tables(rpb, rows):
    kh = min(NA_WIN_H, rows)
    last_r0 = rows - NA_ROWS_PER_BLOCK
    tabs = []
    for r0 in (0, 2 * NA_ROWS_PER_BLOCK, last_r0):
        kstart = int(np.clip(r0 - kh // 2, 0, rows - NA_KEY_ROWS))
        r = r0 + np.arange(NA_ROWS_PER_BLOCK)
        rs = np.clip(r - kh // 2, 0, rows - kh)
        kr = kstart + np.arange(NA_KEY_ROWS)
        row_ok = (kr[None, :] >= rs[:, None]) & (kr[None, :] < rs[:, None] + kh)
        row_off = np.clip(kr[None, :] - r[:, None] + (NA_WIN_H - 1), 0, 2 * NA_WIN_H - 2)
        c = np.arange(GRID_W)
        cs = np.clip(c - NA_WIN_W // 2, 0, GRID_W - NA_WIN_W)
        col_ok = (c[None, :] >= cs[:, None]) & (c[None, :] < cs[:, None] + NA_WIN_W)
        col_off = np.clip(c[None, :] - c[:, None] + (NA_WIN_W - 1), 0, 2 * NA_WIN_W - 2)
        sel_r = jnp.asarray(row_off[..., None] == np.arange(2 * NA_WIN_H - 1), F32)
        sel_c = jnp.asarray(col_off[..., None] == np.arange(2 * NA_WIN_W - 1), F32)
        b = jnp.einsum('aeu,huv,cdv->haced', sel_r, rpb, sel_c, precision=lax.Precision.HIGHEST)
        ok = row_ok[:, None, :, None] & col_ok[None, :, None, :]
        b = jnp.where(jnp.asarray(ok)[None], b, NEG_BIG)
        tabs.append(b.reshape(rpb.shape[0], NA_ROWS_PER_BLOCK * GRID_W, NA_KEY_ROWS * GRID_W))
    return jnp.stack(tabs)


def _na_kernel(q_ref, k_ref, v_ref, bias_ref, o_ref, *, seq, ctx_len):
    tq = q_ref.shape[1]
    rows = seq // GRID_W
    n_lat_tiles = seq // tq
    nkeys = NA_KEY_ROWS * GRID_W
    t = pl.program_id(1)
    lane = lax.broadcasted_iota(jnp.int32, (tq, LANES), 1)
    kstart = jnp.clip(t * NA_ROWS_PER_BLOCK - NA_WIN_H // 2, 0, rows - NA_KEY_ROWS)
    koff = pl.multiple_of(kstart * GRID_W, GRID_W)

    def run(with_window):
        for pair in range(NA_HEADS // 2):
            sl = slice(pair * LANES, (pair + 1) * LANES)
            q2 = q_ref[0, :, sl].astype(F32)
            kc = k_ref[0, seq:seq + ctx_len, sl]
            vc = v_ref[0, seq:seq + ctx_len, sl]
            if with_window:
                kw = k_ref[0, pl.ds(koff, nkeys), sl]
                vw = v_ref[0, pl.ds(koff, nkeys), sl]
            outs = []
            for sub in range(2):
                keep = (lane < NA_HEAD_DIM) if sub == 0 else (lane >= NA_HEAD_DIM)
                q = jnp.where(keep, q2, 0.0).astype(BF16)
                s_c = _dot_nt(q, kc)
                m = jnp.max(s_c, axis=-1, keepdims=True)
                if with_window:
                    s_w = _dot_nt(q, kw) + bias_ref[0, 2 * pair + sub]
                    m = jnp.maximum(m, jnp.max(s_w, axis=-1, keepdims=True))
                p_c = jnp.exp(s_c - m)
                l = jnp.sum(p_c, axis=-1, keepdims=True)
                acc = _dot(p_c.astype(BF16), vc)
                if with_window:
                    p_w = jnp.exp(s_w - m)
                    l = l + jnp.sum(p_w, axis=-1, keepdims=True)
                    acc = acc + _dot(p_w.astype(BF16), vw)
                outs.append(acc / l)
            o_ref[0, :, sl] = jnp.where(lane < NA_HEAD_DIM, outs[0], outs[1]).astype(BF16)

    @pl.when(t < n_lat_tiles)
    def _():
        run(True)

    @pl.when(t >= n_lat_tiles)
    def _():
        run(False)


def _na_attention(q, k, v, bias, seq, ctx_len):
    nb, lt, w = q.shape
    tq = NA_ROWS_PER_BLOCK * GRID_W
    n_lat = seq // tq
    kern = functools.partial(_na_kernel, seq=seq, ctx_len=ctx_len)

    def variant(b, t):
        return (jnp.where(t == 0, 0, jnp.where(t >= n_lat - 1, 2, 1)), 0, 0, 0)

    return pl.pallas_call(
        kern,
        grid=(nb, lt // tq),
        in_specs=[pl.BlockSpec((1, tq, w), lambda b, t: (b, t, 0)),
                  pl.BlockSpec((1, lt, w), lambda b, t: (b, 0, 0)),
                  pl.BlockSpec((1, lt, w), lambda b, t: (b, 0, 0)),
                  pl.BlockSpec((1,) + bias.shape[1:], variant)],
        out_specs=pl.BlockSpec((1, tq, w), lambda b, t: (b, t, 0)),
        out_shape=jax.ShapeDtypeStruct((nb, lt, w), BF16),
        compiler_params=_cparams("parallel", "parallel"),
        name="na_attention",
    )(q, k, v, bias)


def _out_proj_kernel(x_ref, a_ref, b_ref, wa_ref, wb_ref, gate_ref, o_ref):
    y = _dot(a_ref[0], wa_ref[...]) + _dot(b_ref[0], wb_ref[...])
    o_ref[0] = x_ref[0] + gate_ref[0] * y


def _out_proj(xu, a, b, w_out, gate, seq):
    nb, lt, d = xu.shape
    tm = 256
    ka = a.shape[2]
    wa = w_out[:ka].astype(BF16)
    wb = w_out[ka:].astype(BF16)
    row = lambda bb, t: (bb, t, 0)
    full = lambda arr: pl.BlockSpec(arr.shape, lambda bb, t: (0,) * arr.ndim)
    return pl.pallas_call(
        _out_proj_kernel,
        grid=(nb, lt // tm),
        in_specs=[pl.BlockSpec((1, tm, d), row), pl.BlockSpec((1, tm, ka), row),
                  pl.BlockSpec((1, tm, b.shape[2]), row), full(wa), full(wb), _mod_spec(seq // tm, nb, d)],
        out_specs=pl.BlockSpec((1, tm, d), row),
        out_shape=jax.ShapeDtypeStruct(xu.shape, F32),
        compiler_params=_cparams("parallel", "parallel"),
        name="out_proj",
    )(xu, a, b, wa, wb, gate)


def _ffn_kernel(x_ref, g_ref, sc_ref, sh_ref, gate_ref, wg_ref, wu_ref, wd_ref, o_ref, h_scr, acc_scr):
    f = pl.program_id(2)

    @pl.when(f == 0)
    def _():
        h_scr[...] = _norm_mod(x_ref[0], g_ref[...], sc_ref[0], sh_ref[0]).astype(BF16)
        acc_scr[...] = jnp.zeros_like(acc_scr)

    h = h_scr[...]
    a = _dot(h, wg_ref[...])
    u = _dot(h, wu_ref[...])
    acc_scr[...] += _dot((a * jax.nn.sigmoid(a) * u).astype(BF16), wd_ref[...])

    @pl.when(f == pl.num_programs(2) - 1)
    def _():
        o_ref[0] = x_ref[0] + gate_ref[0] * acc_scr[...]


def _ffn(xu, norm_g, scale, shift, gate, w_gate, w_up, w_down, seq):
    nb, lt, d = xu.shape
    dff = w_gate.shape[1]
    tm = 256
    tf = dff
    row = lambda b, t, f: (b, t, 0)
    n_lat = seq // tm
    mod = pl.BlockSpec((1, 1, d), lambda b, t, f: (jnp.where(t < n_lat, b, nb), 0, 0))
    g2 = norm_g.reshape(1, d)
    once = pl.Buffered(1)
    return pl.pallas_call(
        _ffn_kernel,
        grid=(nb, lt // tm, dff // tf),
        in_specs=[pl.BlockSpec((1, tm, d), row), pl.BlockSpec((1, d), lambda b, t, f: (0, 0)), mod, mod, mod,
                  pl.BlockSpec((d, tf), lambda b, t, f: (0, f), pipeline_mode=once),
                  pl.BlockSpec((d, tf), lambda b, t, f: (0, f), pipeline_mode=once),
                  pl.BlockSpec((tf, d), lambda b, t, f: (f, 0), pipeline_mode=once)],
        out_specs=pl.BlockSpec((1, tm, d), row),
        out_shape=jax.ShapeDtypeStruct(xu.shape, F32),
        scratch_shapes=[pltpu.VMEM((tm, d), BF16), pltpu.VMEM((tm, d), F32)],
        compiler_params=_cparams("parallel", "parallel", "arbitrary"),
        name="dense_swiglu",
    )(xu, g2, scale, shift, gate, w_gate.astype(BF16), w_up.astype(BF16), w_down.astype(BF16))


def _s5_in_kernel(x_ref, g_ref, sc_ref, sh_ref, w_ref, o_ref):
    h = _norm_mod(x_ref[0], g_ref[...], sc_ref[0], sh_ref[0]).astype(BF16)
    u = _dot(h, w_ref[...])
    for k in range(o_ref.shape[0]):
        o_ref[k, 0] = u[:, k * LANES:(k + 1) * LANES]


def _s5_in(xu, norm_g, scale, shift, w_in, seq):
    nb, lt, d = xu.shape
    tm = 256
    w = w_in.astype(BF16)
    row = lambda b, t: (b, t, 0)
    mod = _mod_spec(seq // tm, nb, d)
    g2 = norm_g.reshape(1, d)
    return pl.pallas_call(
        _s5_in_kernel,
        grid=(nb, lt // tm),
        in_specs=[pl.BlockSpec((1, tm, d), row), pl.BlockSpec((1, d), lambda b, t: (0, 0)), mod, mod,
                  pl.BlockSpec(w.shape, lambda b, t: (0, 0))],
        out_specs=pl.BlockSpec((w.shape[1] // LANES, 1, tm, LANES), lambda b, t: (0, b, t, 0)),
        out_shape=jax.ShapeDtypeStruct((w.shape[1] // LANES, nb, lt, LANES), F32),
        compiler_params=_cparams("parallel", "parallel"),
        name="s5_in",
    )(xu, g2, scale, shift, w)


def _s5_operators(a_re, a_im, log_step, b_re, b_im, c_re, c_im):
    hp = lax.Precision.HIGHEST
    t_len = S5_CHUNK
    ops = []
    for direction in range(2):
        dt = jnp.exp(log_step[direction])[:, None]
        lre, lim = a_re[direction] * dt, a_im[direction] * dt
        decay = jnp.exp(lre)
        ab_re, ab_im = decay * jnp.cos(lim), decay * jnp.sin(lim)
        den = a_re[direction] ** 2 + a_im[direction] ** 2
        f_re = ((ab_re - 1) * a_re[direction] + ab_im * a_im[direction]) / den
        f_im = (ab_im * a_re[direction] - (ab_re - 1) * a_im[direction]) / den
        bb_re = f_re[..., None] * b_re[direction] - f_im[..., None] * b_im[direction]
        bb_im = f_re[..., None] * b_im[direction] + f_im[..., None] * b_re[direction]
        cr, ci = c_re[direction], c_im[direction]
        tau = jnp.arange(t_len + 1, dtype=F32)[:, None, None]
        pw = jnp.exp(tau * lre[None])
        pw_re, pw_im = pw * jnp.cos(tau * lim[None]), pw * jnp.sin(tau * lim[None])
        ab_b_re = pw_re[..., None] * bb_re[None] - pw_im[..., None] * bb_im[None]
        ab_b_im = pw_re[..., None] * bb_im[None] + pw_im[..., None] * bb_re[None]
        kk = (jnp.einsum('gip,tgpj->tgij', cr, ab_b_re, precision=hp)
              - jnp.einsum('gip,tgpj->tgij', ci, ab_b_im, precision=hp))
        ca_re = cr[None] * pw_re[:, :, None, :] - ci[None] * pw_im[:, :, None, :]
        ca_im = cr[None] * pw_im[:, :, None, :] + ci[None] * pw_re[:, :, None, :]
        s = np.arange(t_len)
        z_pow = (t_len - 1 - s) if direction == 0 else s
        c_pow = (s + 1) if direction == 0 else (t_len - s)
        ops.append((kk, [ab_b_re[z_pow], ab_b_im[z_pow]],
                    [ca_re[c_pow], -ca_im[c_pow]],
                    [pw_re[t_len], pw_im[t_len]]))
    lg = S5_LANE_GROUPS
    nblk = a_re.shape[1] // lg
    eye = jnp.eye(lg, dtype=BF16)
    n_state = 4 * lg * S5_STATE
    kf, kb = ops[0][0], ops[1][0]
    kd = jnp.concatenate([kb[t_len - 1:0:-1], (kf[0] + kb[0])[None], kf[1:t_len]])
    kd = kd.astype(BF16).reshape(2 * t_len - 1, nblk, lg, S5_GROUP, S5_GROUP).transpose(1, 0, 4, 2, 3)
    d_blk = kd[:, :, None] * eye[:, None, :, None]
    d_cat = d_blk.reshape(nblk, 2 * t_len - 1, LANES, LANES).transpose(0, 2, 1, 3).reshape(nblk, LANES, -1)
    wz4 = jnp.stack(ops[0][1] + ops[1][1]).astype(BF16)
    wz4 = wz4.reshape(4, t_len, nblk, lg, S5_STATE, S5_GROUP).transpose(2, 1, 5, 0, 3, 4)
    wc4 = jnp.stack(ops[0][2] + ops[1][2]).astype(BF16)
    wc4 = wc4.reshape(4, t_len, nblk, lg, S5_GROUP, S5_STATE).transpose(2, 1, 4, 0, 3, 5)
    compact = (nblk, t_len * S5_GROUP, n_state)
    group_of_lane = (np.arange(n_state) // S5_STATE) % lg
    mask = np.broadcast_to((group_of_lane[None, :] == np.arange(lg)[:, None])[:, None, :], (lg, S5_GROUP, n_state))
    a_t = jnp.stack(ops[0][3] + ops[1][3]).reshape(4, nblk, lg * S5_STATE).transpose(1, 0, 2)
    return d_cat, wz4.reshape(compact), wc4.reshape(compact), jnp.asarray(mask, BF16), a_t


def _s5_core_kernel(u_ref, dcat_ref, wzc_ref, wcc_ref, mask_ref, at_ref, y_ref,
                    wi_ref, wz_ref, wct_ref, z_scr, h_scr, *, n_ctx_chunks):
    n_chunks = z_scr.shape[0]
    n_lat = n_chunks - n_ctx_chunks
    w = z_scr.shape[1] // 4
    width = wi_ref.shape[1]

    @pl.when(pl.program_id(1) == 0)
    def _():
        for s in range(S5_CHUNK):
            lag0 = (S5_CHUNK - 1 - s) * LANES
            wi_ref[s * LANES:(s + 1) * LANES, :] = dcat_ref[0, :, lag0:lag0 + width]
            rows = slice(s * S5_GROUP, (s + 1) * S5_GROUP)
            for a in range(S5_LANE_GROUPS):
                dst = slice(s * LANES + a * S5_GROUP, s * LANES + (a + 1) * S5_GROUP)
                wz_ref[dst, :] = wzc_ref[0, rows, :] * mask_ref[a]
                wct_ref[dst, :] = wcc_ref[0, rows, :] * mask_ref[a]

    a = jnp.concatenate([u_ref[0, 0, pl.ds(s, n_chunks, stride=S5_CHUNK), :].astype(BF16)
                         for s in range(S5_CHUNK)], axis=1)
    z_scr[...] = _dot(a, wz_ref[...])
    ar_f, ai_f, ar_b, ai_b = [at_ref[0, k:k + 1, :] for k in range(4)]

    def step(k, carry):
        (fr, fi), (br, bi) = carry
        cf = jnp.where(k < n_ctx_chunks, n_lat + k, k - n_ctx_chunks)
        cb = n_chunks - 1 - k
        rf, rb = pl.ds(cf, 1), pl.ds(cb, 1)
        h_scr[rf, 0:w] = fr
        h_scr[rf, w:2 * w] = fi
        h_scr[rb, 2 * w:3 * w] = br
        h_scr[rb, 3 * w:4 * w] = bi
        fwd = (ar_f * fr - ai_f * fi + z_scr[rf, 0:w], ar_f * fi + ai_f * fr + z_scr[rf, w:2 * w])
        bwd = (ar_b * br - ai_b * bi + z_scr[rb, 2 * w:3 * w], ar_b * bi + ai_b * br + z_scr[rb, 3 * w:4 * w])
        return fwd, bwd

    zero = jnp.zeros((1, w), F32)
    lax.fori_loop(0, n_chunks, step, ((zero, zero), (zero, zero)))
    y = _dot(a, wi_ref[...]) + _dot_nt(h_scr[...].astype(BF16), wct_ref[...])
    for t in range(S5_CHUNK):
        y_ref[0, 0, pl.ds(t, n_chunks, stride=S5_CHUNK), :] = y[:, t * LANES:(t + 1) * LANES]


def _s5_core(u_blocks, ops, n_ctx_chunks):
    nblk, nb, lt, lanes = u_blocks.shape
    n_chunks = lt // S5_CHUNK
    d_cat, wz_c, wc_c, mask, a_t = ops
    width = S5_CHUNK * lanes
    n_state = wz_c.shape[2]
    kern = functools.partial(_s5_core_kernel, n_ctx_chunks=n_ctx_chunks)
    per_block = lambda a: pl.BlockSpec((1,) + a.shape[1:], lambda g, b: (g, 0, 0))
    tok = pl.BlockSpec((1, 1, lt, lanes), lambda g, b: (g, b, 0, 0))
    return pl.pallas_call(
        kern,
        grid=(nblk, nb),
        in_specs=[tok, per_block(d_cat), per_block(wz_c), per_block(wc_c),
                  pl.BlockSpec(mask.shape, lambda g, b: (0, 0, 0)), per_block(a_t)],
        out_specs=tok,
        out_shape=jax.ShapeDtypeStruct(u_blocks.shape, F32),
        scratch_shapes=[pltpu.VMEM((width, width), BF16), pltpu.VMEM((width, n_state), BF16),
                        pltpu.VMEM((width, n_state), BF16),
                        pltpu.VMEM((n_chunks, n_state), F32), pltpu.VMEM((n_chunks, n_state), F32)],
        compiler_params=_cparams("arbitrary", "arbitrary"),
        name="s5_core",
    )(u_blocks, d_cat, wz_c, wc_c, mask, a_t)


def _s5_out_kernel(x_ref, u_ref, y_ref, d_ref, w_ref, gate_ref, o_ref):
    d = x_ref.shape[2]
    u = jnp.concatenate([u_ref[k, 0] for k in range(u_ref.shape[0])], axis=1)
    y = jnp.concatenate([y_ref[k, 0] for k in range(y_ref.shape[0])], axis=1)
    y = u * d_ref[...] + y
    z = _dot(jax.nn.gelu(y).astype(BF16), w_ref[...])
    o_ref[0] = x_ref[0] + gate_ref[0] * (z[:, :d] * jax.nn.sigmoid(z[:, d:]))


def _s5_out(x, u_blocks, y_blocks, d_skip, w_glu, gate):
    nb, seq, d = x.shape
    tm = 256
    nblk, _, _, lanes = u_blocks.shape
    w = nblk * lanes
    row = lambda b, t: (b, t, 0)
    blk = pl.BlockSpec((nblk, 1, tm, lanes), lambda b, t: (0, b, t, 0))
    wg = w_glu.astype(BF16)
    return pl.pallas_call(
        _s5_out_kernel,
        grid=(nb, seq // tm),
        in_specs=[pl.BlockSpec((1, tm, d), row), blk, blk,
                  pl.BlockSpec((1, w), lambda b, t: (0, 0)), pl.BlockSpec(wg.shape, lambda b, t: (0, 0)),
                  pl.BlockSpec((1, 1, d), lambda b, t: (b, 0, 0))],
        out_specs=pl.BlockSpec((1, tm, d), row),
        out_shape=jax.ShapeDtypeStruct(x.shape, F32),
        compiler_params=_cparams("parallel", "parallel"),
        name="s5_out",
    )(x, u_blocks, y_blocks, d_skip.reshape(1, w), wg, gate)


R_IDX, R_WT, R_RANK = 0, TOP_K, 2 * TOP_K


def _router_kernel(x_ref, g_ref, sc_ref, sh_ref, w_ref, tri_ref, rec_ref, cnt_ref, base_scr):
    @pl.when((pl.program_id(0) == 0) & (pl.program_id(1) == 0))
    def _():
        base_scr[...] = jnp.zeros_like(base_scr)

    t = _norm_mod(x_ref[0], g_ref[...], sc_ref[0], sh_ref[0]).astype(BF16)
    logits = _dot(t, w_ref[...])
    lane = lax.broadcasted_iota(jnp.int32, logits.shape, 1)
    big = jnp.int32(LANES)
    l1 = jnp.where(lane < N_EXPERTS, logits, -jnp.inf)
    v1 = jnp.max(l1, axis=-1, keepdims=True)
    i1 = jnp.min(jnp.where(l1 == v1, lane, big), axis=-1, keepdims=True)
    l2 = jnp.where(lane == i1, -jnp.inf, l1)
    v2 = jnp.max(l2, axis=-1, keepdims=True)
    i2 = jnp.min(jnp.where(l2 == v2, lane, big), axis=-1, keepdims=True)
    e2 = jnp.exp(v2 - v1)
    w1 = 1.0 / (1.0 + e2)
    w2 = e2 / (1.0 + e2)
    hit = jnp.where((lane == i1) | (lane == i2), 1.0, 0.0)
    before = _dot(tri_ref[...], hit.astype(BF16)) + base_scr[0:1, :]
    r1 = jnp.sum(jnp.where(lane == i1, before, 0.0), axis=-1, keepdims=True)
    r2 = jnp.sum(jnp.where(lane == i2, before, 0.0), axis=-1, keepdims=True)
    base_scr[...] = base_scr[...] + jnp.sum(hit, axis=0, keepdims=True)
    cnt_ref[...] = base_scr[...]
    rec = jnp.zeros(logits.shape, F32)
    for k, val in enumerate((i1.astype(F32), i2.astype(F32), w1, w2, r1, r2)):
        rec = jnp.where(lane == k, val, rec)
    rec_ref[0] = rec


def _router(x, norm_g, scale, shift, w_router):
    nb, seq, d = x.shape
    tm = 256
    wr = jnp.zeros((d, LANES), F32).at[:, :N_EXPERTS].set(w_router).astype(BF16)
    tri = jnp.asarray(np.tril(np.ones((tm, tm), np.float32), -1), BF16)
    row = lambda b, t: (b, t, 0)
    mod = pl.BlockSpec((1, 1, d), lambda b, t: (b, 0, 0))
    g2 = norm_g.reshape(1, d)
    return pl.pallas_call(
        _router_kernel,
        grid=(nb, seq // tm),
        in_specs=[pl.BlockSpec((1, tm, d), row), pl.BlockSpec((1, d), lambda b, t: (0, 0)), mod, mod,
                  pl.BlockSpec(wr.shape, lambda b, t: (0, 0)), pl.BlockSpec(tri.shape, lambda b, t: (0, 0))],
        out_specs=[pl.BlockSpec((1, tm, LANES), row), pl.BlockSpec((8, LANES), lambda b, t: (0, 0))],
        out_shape=[jax.ShapeDtypeStruct((nb, seq, LANES), F32), jax.ShapeDtypeStruct((8, LANES), F32)],
        scratch_shapes=[pltpu.VMEM((8, LANES), F32)],
        compiler_params=_cparams("arbitrary", "arbitrary"),
        name="router",
    )(x, g2, scale, shift, wr, tri)


def _row_copies_wait(src_ref, dst_ref, sem):
    pltpu.make_async_copy(src_ref, dst_ref, sem).wait()


def _dispatch_kernel(dest_ref, fill_ref, x_ref, g_ref, sc_ref, sh_ref, xs_ref, tbuf, zbuf, sem, zsem):
    tm = x_ref.shape[1]
    ztm = zbuf.shape[0]
    step = pl.program_id(0) * pl.num_programs(1) + pl.program_id(1)
    n_steps = pl.num_programs(0) * pl.num_programs(1)
    slot = step % 2

    def drain(s):
        for _ in range(TOP_K):
            _row_copies_wait(tbuf.at[s], xs_ref.at[pl.ds(0, tm)], sem.at[s])

    @pl.when(step == 0)
    def _():
        zbuf[...] = jnp.zeros_like(zbuf)

        def fill(start):
            def body(i, carry):
                @pl.when(fill_ref[i] > 0)
                def _():
                    cp = pltpu.make_async_copy(zbuf, xs_ref.at[pl.ds(pl.multiple_of(i * ztm, ztm), ztm)], zsem)
                    if start:
                        cp.start()
                    else:
                        cp.wait()
                return carry
            lax.fori_loop(0, fill_ref.shape[0], body, 0)

        fill(True)
        fill(False)

    @pl.when(step >= 2)
    def _():
        drain(slot)

    tbuf[slot] = _norm_mod(x_ref[0], g_ref[...], sc_ref[0], sh_ref[0])
    base = step * (tm * TOP_K)

    def body(r, carry):
        for k in range(TOP_K):
            dst = dest_ref[base + r * TOP_K + k]
            pltpu.make_async_copy(tbuf.at[slot, pl.ds(r, 1)], xs_ref.at[pl.ds(dst, 1)],
                                  sem.at[slot]).start(priority=k % 2)
        return carry

    lax.fori_loop(0, tm, body, 0, unroll=8)

    @pl.when(step == n_steps - 1)
    def _():
        drain(slot)

        @pl.when(n_steps > 1)
        def _():
            drain(1 - slot)


def _dispatch(x, norm_g, scale, shift, dest, tile_fill, n_rows, expert_tm):
    nb, seq, d = x.shape
    tm = 256
    row = lambda b, t, dr, fr: (b, t, 0)
    mod = pl.BlockSpec((1, 1, d), lambda b, t, dr, fr: (b, 0, 0))
    g2 = norm_g.reshape(1, d)
    grid_spec = pltpu.PrefetchScalarGridSpec(
        num_scalar_prefetch=2,
        grid=(nb, seq // tm),
        in_specs=[pl.BlockSpec((1, tm, d), row), pl.BlockSpec((1, d), lambda b, t, dr, fr: (0, 0)), mod, mod],
        out_specs=pl.BlockSpec(memory_space=pl.ANY),
        scratch_shapes=[pltpu.VMEM((2, tm, d), F32), pltpu.VMEM((expert_tm, d), F32),
                        pltpu.SemaphoreType.DMA((2,)), pltpu.SemaphoreType.DMA(())],
    )
    return pl.pallas_call(
        _dispatch_kernel,
        grid_spec=grid_spec,
        out_shape=jax.ShapeDtypeStruct((n_rows, d), F32),
        compiler_params=_cparams("arbitrary", "arbitrary"),
        name="dispatch",
    )(dest, tile_fill, x, g2, scale, shift)


def _expert_kernel(te_ref, tr_ref, x_ref, wg_ref, wu_ref, wd_ref, o_ref, xb_scr, acc_scr):
    i = pl.program_id(0)
    f = pl.program_id(1)
    nf = pl.num_programs(1)
    live = tr_ref[i] > 0

    @pl.when(jnp.logical_not(live) & (f == nf - 1))
    def _():
        o_ref[...] = jnp.zeros_like(o_ref)

    @pl.when(live)
    def _():
        @pl.when(f == 0)
        def _():
            xb_scr[...] = x_ref[...].astype(BF16)
            acc_scr[...] = jnp.zeros_like(acc_scr)

        x = xb_scr[...]
        a = _dot(x, wg_ref[0])
        u = _dot(x, wu_ref[0])
        acc_scr[...] += _dot((a * jax.nn.sigmoid(a) * u).astype(BF16), wd_ref[0])

        @pl.when(f == nf - 1)
        def _():
            o_ref[...] = acc_scr[...]


def _experts(xs, tile_expert, tile_rows, w_gate, w_up, w_down, tm, tf):
    n_rows, d = xs.shape
    dff = w_gate.shape[2]
    nf = dff // tf

    def fsel(i, f, te, tr):
        return jnp.where(tr[i] > 0, f, nf - 1)

    grid_spec = pltpu.PrefetchScalarGridSpec(
        num_scalar_prefetch=2,
        grid=(n_rows // tm, nf),
        in_specs=[pl.BlockSpec((tm, d), lambda i, f, te, tr: (i, 0)),
                  pl.BlockSpec((1, d, tf), lambda i, f, te, tr: (te[i], 0, fsel(i, f, te, tr))),
                  pl.BlockSpec((1, d, tf), lambda i, f, te, tr: (te[i], 0, fsel(i, f, te, tr))),
                  pl.BlockSpec((1, tf, d), lambda i, f, te, tr: (te[i], fsel(i, f, te, tr), 0))],
        out_specs=pl.BlockSpec((tm, d), lambda i, f, te, tr: (i, 0)),
        scratch_shapes=[pltpu.VMEM((tm, d), BF16), pltpu.VMEM((tm, d), F32)],
    )
    return pl.pallas_call(
        _expert_kernel,
        grid_spec=grid_spec,
        out_shape=jax.ShapeDtypeStruct((n_rows, d), F32),
        compiler_params=_cparams("arbitrary", "arbitrary"),
        name="experts",
    )(tile_expert, tile_rows, xs, w_gate, w_up, w_down)


def _combine_kernel(dest_ref, x_ref, rec_ref, gate_ref, g_ref, ys_ref, o_ref, ybuf, sem):
    tm = x_ref.shape[1]
    step = pl.program_id(0) * pl.num_programs(1) + pl.program_id(1)
    n_steps = pl.num_programs(0) * pl.num_programs(1)
    slot = step % 2

    def issue(st, sl):
        base = st * (tm * TOP_K)

        def body(r, carry):
            for k in range(TOP_K):
                src = dest_ref[base + r * TOP_K + k]
                pltpu.make_async_copy(ys_ref.at[pl.ds(src, 1)], ybuf.at[sl, k, pl.ds(r, 1)],
                                      sem.at[sl]).start(priority=k % 2)
            return carry

        lax.fori_loop(0, tm, body, 0, unroll=8)

    @pl.when(step == 0)
    def _():
        issue(step, slot)

    @pl.when(step + 1 < n_steps)
    def _():
        issue(step + 1, 1 - slot)

    for k in range(TOP_K):
        _row_copies_wait(ys_ref.at[pl.ds(0, tm)], ybuf.at[slot, k], sem.at[slot])

    rec = rec_ref[0]
    y = rec[:, R_WT:R_WT + 1] * ybuf[slot, 0] + rec[:, R_WT + 1:R_WT + 2] * ybuf[slot, 1]
    x = x_ref[0] + gate_ref[0] * y
    o_ref[0] = _rms(x) * g_ref[...]


def _combine(x, rec, gate, final_g, ys, dest):
    nb, seq, d = x.shape
    tm = 256
    row = lambda b, t, dr: (b, t, 0)
    grid_spec = pltpu.PrefetchScalarGridSpec(
        num_scalar_prefetch=1,
        grid=(nb, seq // tm),
        in_specs=[pl.BlockSpec((1, tm, d), row), pl.BlockSpec((1, tm, LANES), row),
                  pl.BlockSpec((1, 1, d), lambda b, t, dr: (b, 0, 0)), pl.BlockSpec((1, d), lambda b, t, dr: (0, 0)),
                  pl.BlockSpec(memory_space=pl.ANY)],
        out_specs=pl.BlockSpec((1, tm, d), row),
        scratch_shapes=[pltpu.VMEM((2, TOP_K, tm, d), F32), pltpu.SemaphoreType.DMA((2,))],
    )
    return pl.pallas_call(
        _combine_kernel,
        grid_spec=grid_spec,
        out_shape=jax.ShapeDtypeStruct(x.shape, F32),
        compiler_params=_cparams("arbitrary", "arbitrary"),
        name="combine_final_norm",
    )(dest, x, rec, gate, final_g.reshape(1, d), ys)


def _moe(x, norm_g, scale, shift, gate, w_router, w_gate, w_up, w_down, final_g):
    nb, seq, d = x.shape
    n_tok = nb * seq
    tm, tf = 512, w_gate.shape[2] // 2
    rec, cnt = _router(x, norm_g, scale, shift, w_router)
    counts = cnt[0, :N_EXPERTS].astype(jnp.int32)
    padded = ((counts + tm - 1) // tm) * tm
    pend = jnp.cumsum(padded)
    pstart = pend - padded
    idx = rec[:, :, R_IDX:R_IDX + TOP_K].astype(jnp.int32)
    rank = rec[:, :, R_RANK:R_RANK + TOP_K].astype(jnp.int32)
    seg = jnp.sum(jnp.where(idx[..., None] == jnp.arange(N_EXPERTS), pstart, 0), axis=-1)
    dest = (seg + rank).reshape(n_tok * TOP_K)
    n_rows = n_tok * TOP_K + N_EXPERTS * tm
    tile_start = jnp.arange(n_rows // tm, dtype=jnp.int32) * tm
    tile_expert = jnp.sum((tile_start[:, None] >= pend[None, :]).astype(jnp.int32), axis=1)
    live = tile_expert < N_EXPERTS
    last_live = jnp.max(jnp.where(live, tile_expert, 0))
    tile_expert = jnp.where(live, tile_expert, last_live)
    seg_rows = jnp.sum(jnp.where(tile_expert[:, None] == jnp.arange(N_EXPERTS), (pstart + counts)[None, :], 0), axis=1)
    tile_rows = jnp.where(live, jnp.clip(seg_rows - tile_start, 0, tm), 0).astype(jnp.int32)
    tile_fill = (tile_rows < tm).astype(jnp.int32)

    xs = _dispatch(x, norm_g, scale, shift, dest, tile_fill, n_rows, tm)
    ys = _experts(xs, tile_expert.astype(jnp.int32), tile_rows,
                  w_gate.astype(BF16), w_up.astype(BF16), w_down.astype(BF16), tm, tf)
    return _combine(x, rec, gate, final_g, ys, dest)


def kernel(x, c, ctx, c_ctx, mod_w, mod_b, norm1_g, norm2_g, ev_w_in, ev_q_norm_g, ev_w_qb, ev_kv_norm_g, ev_w_kvb, ev_na_rpb, ev_w_out, ev_ffn_w_gate, ev_ffn_w_up, ev_ffn_w_down, od_w_in, od_a_re, od_a_im, od_log_step, od_b_re, od_b_im, od_c_re, od_c_im, od_d, od_w_glu, moe_w_router, moe_w_gate, moe_w_up, moe_w_down, final_g):
    nb, seq, d = x.shape
    ctx_len = ctx.shape[1]
    assert mod_w.shape[0] == 2 and nb < 8
    assert seq % (NA_ROWS_PER_BLOCK * GRID_W) == 0 and ctx_len == NA_ROWS_PER_BLOCK * GRID_W

    cond = jnp.zeros((8, d), F32).at[:nb].set(c).at[nb].set(c_ctx)
    mods = _adaln(cond, mod_w, mod_b)

    def mod_parts(layer):
        return [mods[layer, :, i * d:(i + 1) * d].reshape(8, 1, d) for i in range(N_MOD)]

    xu = jnp.concatenate([x, ctx], axis=1)

    sh1, sc1, g1, sh2, sc2, g2 = mod_parts(0)
    cos, sin = _rope_tables(seq, ctx_len)
    weights = _even_weights(ev_w_in[0], ev_w_qb[0], ev_w_kvb[0])
    q, k, v, nq, nk, nv = _even_project(xu, norm1_g[0], sc1, sh1, weights, ev_q_norm_g[0], ev_kv_norm_g[0],
                                        cos, sin, seq)
    mla = _mla_attention(q, k, v, seq, ctx_len)
    bias = _na_bias_tables(ev_na_rpb[0], seq // GRID_W)
    na = _na_attention(nq, nk, nv, bias, seq, ctx_len)
    xu = _out_proj(xu, mla, na, ev_w_out[0], g1, seq)
    xu = _ffn(xu, norm2_g[0], sc2, sh2, g2, ev_ffn_w_gate[0], ev_ffn_w_up[0], ev_ffn_w_down[0], seq)

    sh1, sc1, g1, sh2, sc2, g2 = mod_parts(1)
    u = _s5_in(xu, norm1_g[1], sc1, sh1, od_w_in[0], seq)
    ops = _s5_operators(od_a_re[0], od_a_im[0], od_log_step[0], od_b_re[0], od_b_im[0], od_c_re[0], od_c_im[0])
    y = _s5_core(u, ops, ctx_len // S5_CHUNK)
    xl = xu[:, :seq]
    xl = _s5_out(xl, u, y, od_d[0], od_w_glu[0], g1[:nb])
    return _moe(xl, norm2_g[1], sc2[:nb], sh2[:nb], g2[:nb], moe_w_router[0], moe_w_gate[0], moe_w_up[0],
                moe_w_down[0], final_g)
```

```python
import functools
import math

import numpy as np
import jax
import jax.numpy as jnp
from jax import lax
from jax.experimental import pallas as pl
from jax.experimental.pallas import tpu as pltpu

F32 = jnp.float32
BF16 = jnp.bfloat16

LANES = 128
VMEM_LIMIT_BYTES = 52 * 1024 * 1024

NORM_EPS = 1e-6
ROPE_BASE = 10000.0
GRID_W = 64
N_MOD = 6

MLA_HEADS = 8
MLA_NOPE = 64
MLA_ROPE = 32
MLA_V = 64
Q_LORA = 384
KV_LORA = 256
MLA_SCALE = (MLA_NOPE + MLA_ROPE) ** -0.5
MLA_EXP2_SCALE = MLA_SCALE * math.log2(math.e)

NA_HEADS = 8
NA_HEAD_DIM = 64
NA_WIN_H = 8
NA_WIN_W = 16
NA_SCALE = NA_HEAD_DIM ** -0.5
NA_ROWS_PER_BLOCK = 4
NA_KEY_ROWS = NA_ROWS_PER_BLOCK + NA_WIN_H - 1

S5_GROUP = 16
S5_STATE = 64
S5_CHUNK = 16
S5_LANE_GROUPS = LANES // S5_GROUP

N_EXPERTS = 8
TOP_K = 2

NEG_BIG = -1e30


def _cparams(*sem):
    return pltpu.CompilerParams(dimension_semantics=sem, vmem_limit_bytes=VMEM_LIMIT_BYTES)


def _rms(x):
    return x * lax.rsqrt(jnp.mean(x * x, axis=-1, keepdims=True) + NORM_EPS)


def _norm_mod(x, g, scale, shift):
    return (_rms(x) * g) * (1 + scale) + shift


def _dot(a, b):
    return jnp.dot(a, b, preferred_element_type=F32)


def _dot_nt(a, b):
    return lax.dot_general(a, b, (((1,), (1,)), ((), ())), preferred_element_type=F32)


def _adaln_kernel(c_ref, w_ref, b_ref, o_ref):
    c = c_ref[...]
    s = (c * jax.nn.sigmoid(c)).astype(BF16)
    o_ref[0] = _dot(s, w_ref[0].astype(BF16)) + b_ref[0]


def _adaln(cond, mod_w, mod_b):
    nl, d, n = mod_w.shape
    tn = 1536
    return pl.pallas_call(
        _adaln_kernel,
        grid=(nl, n // tn),
        in_specs=[pl.BlockSpec((8, d), lambda l, j: (0, 0)),
                  pl.BlockSpec((1, d, tn), lambda l, j: (l, 0, j)),
                  pl.BlockSpec((1, 1, tn), lambda l, j: (l, 0, j))],
        out_specs=pl.BlockSpec((1, 8, tn), lambda l, j: (l, 0, j)),
        out_shape=jax.ShapeDtypeStruct((nl, 8, n), F32),
        compiler_params=_cparams("parallel", "parallel"),
        name="adaln",
    )(cond, mod_w, mod_b.reshape(nl, 1, n))


def _even_proj_kernel(x_ref, g_ref, sc_ref, sh_ref, win_ref, qg_ref, kvg_ref, wq_ref, wqr_ref, wkk_ref,
                      wkv_ref, cos_ref, sin_ref, q_ref, k_ref, v_ref, nq_ref, nk_ref, nv_ref):
    h = _norm_mod(x_ref[0], g_ref[...], sc_ref[0], sh_ref[0]).astype(BF16)
    p = _dot(h, win_ref[...])
    c0, c1, c2, c3 = Q_LORA, Q_LORA + KV_LORA, Q_LORA + KV_LORA + LANES, Q_LORA + KV_LORA + 2 * LANES
    cqn = (_rms(p[:, :c0]) * qg_ref[...]).astype(BF16)
    ckvn = (_rms(p[:, c0:c1]) * kvg_ref[...]).astype(BF16)
    cos = cos_ref[...]
    sin = sin_ref[...]
    kr = p[:, c1:c2] * cos + p[:, c2:c3] * sin
    qa = _dot(cqn, wq_ref[...])
    qb = _dot(cqn, wqr_ref[...])
    kk = _dot(ckvn, wkk_ref[...])
    for hd in range(MLA_HEADS):
        sl = slice(hd * LANES, (hd + 1) * LANES)
        q_ref[0, :, sl] = ((qa[:, sl] * cos + qb[:, sl] * sin) * MLA_EXP2_SCALE).astype(BF16)
        k_ref[0, :, sl] = (kk[:, sl] + kr).astype(BF16)
    vlane = lax.broadcasted_iota(jnp.int32, (1, MLA_HEADS * LANES), 1) & (LANES - 1)
    v_ref[0] = (_dot(ckvn, wkv_ref[...]) + jnp.where(vlane == MLA_V, 1.0, 0.0)).astype(BF16)
    w = NA_HEADS * NA_HEAD_DIM
    nq_ref[0] = (p[:, c3:c3 + w] * NA_SCALE).astype(BF16)
    nk_ref[0] = p[:, c3 + w:c3 + 2 * w].astype(BF16)
    nv_ref[0] = p[:, c3 + 2 * w:c3 + 3 * w].astype(BF16)


def _rot_half_cols(w):
    q = MLA_ROPE // 4
    return jnp.concatenate([-w[:, q:2 * q], w[:, :q], -w[:, 3 * q:], w[:, 2 * q:3 * q]], axis=1)


def _rope_tables(seq, ctx_len):
    q = MLA_ROPE // 4
    t = np.arange(seq)
    inv_freq = np.float32(ROPE_BASE) ** (-np.arange(q, dtype=np.float32) / np.float32(q))
    ang_r = (t // GRID_W).astype(np.float32)[:, None] * inv_freq[None, :]
    ang_c = (t % GRID_W).astype(np.float32)[:, None] * inv_freq[None, :]
    cos = np.ones((seq + ctx_len, LANES), np.float32)
    sin = np.zeros((seq + ctx_len, LANES), np.float32)
    cos[:seq, MLA_NOPE:MLA_NOPE + MLA_ROPE] = np.concatenate([np.cos(ang_r)] * 2 + [np.cos(ang_c)] * 2, axis=1)
    sin[:seq, MLA_NOPE:MLA_NOPE + MLA_ROPE] = np.concatenate([np.sin(ang_r)] * 2 + [np.sin(ang_c)] * 2, axis=1)
    return jnp.asarray(cos), jnp.asarray(sin)


def _even_weights(w_in, w_qb, w_kvb):
    d = w_in.shape[0]
    c1 = Q_LORA + KV_LORA
    wkr = w_in[:, c1:c1 + MLA_ROPE]
    pad = lambda w: jnp.zeros((d, LANES), F32).at[:, MLA_NOPE:MLA_NOPE + MLA_ROPE].set(w)
    win = jnp.concatenate([w_in[:, :c1], pad(wkr), pad(_rot_half_cols(wkr)), w_in[:, c1 + MLA_ROPE:]], axis=1)
    hq = MLA_NOPE + MLA_ROPE
    wq = w_qb.reshape(Q_LORA, MLA_HEADS, hq)
    zq = jnp.zeros((Q_LORA, MLA_HEADS, LANES - hq), F32)
    wq_main = jnp.concatenate([wq, zq], axis=2).reshape(Q_LORA, MLA_HEADS * LANES)
    rot = jnp.stack([_rot_half_cols(wq[:, h, MLA_NOPE:]) for h in range(MLA_HEADS)], axis=1)
    wq_rot = jnp.concatenate([jnp.zeros((Q_LORA, MLA_HEADS, MLA_NOPE), F32), rot, zq], axis=2)
    wq_rot = wq_rot.reshape(Q_LORA, MLA_HEADS * LANES)
    wkv = w_kvb.reshape(KV_LORA, MLA_HEADS, MLA_NOPE + MLA_V)
    wkk = jnp.concatenate([wkv[:, :, :MLA_NOPE], jnp.zeros((KV_LORA, MLA_HEADS, LANES - MLA_NOPE), F32)], axis=2)
    wkk = wkk.reshape(KV_LORA, MLA_HEADS * LANES)
    wv = jnp.concatenate([wkv[:, :, MLA_NOPE:], jnp.zeros((KV_LORA, MLA_HEADS, LANES - MLA_V), F32)], axis=2)
    wv = wv.reshape(KV_LORA, MLA_HEADS * LANES)
    return tuple(a.astype(BF16) for a in (win, wq_main, wq_rot, wkk, wv))


def _mod_spec(n_lat_tiles, nb, d):
    return pl.BlockSpec((1, 1, d), lambda b, t: (jnp.where(t < n_lat_tiles, b, nb), 0, 0))


def _even_project(xu, norm_g, scale, shift, weights, q_norm_g, kv_norm_g, cos, sin, seq):
    nb, lt, d = xu.shape
    tm = 256
    win, wq, wqr, wkk, wv = weights
    n_lat = seq // tm
    row = lambda b, t: (b, t, 0)
    full = lambda a: pl.BlockSpec(a.shape, lambda b, t: (0,) * a.ndim)
    mod = _mod_spec(n_lat, nb, d)
    g2 = norm_g.reshape(1, d)
    qg2 = q_norm_g.reshape(1, Q_LORA)
    kvg2 = kv_norm_g.reshape(1, KV_LORA)
    wide = MLA_HEADS * LANES
    half = MLA_HEADS * MLA_V
    outs = [jax.ShapeDtypeStruct((nb, lt, wide), BF16), jax.ShapeDtypeStruct((nb, lt, wide), BF16)] + \
           [jax.ShapeDtypeStruct((nb, lt, wide), BF16)] + [jax.ShapeDtypeStruct((nb, lt, half), BF16)] * 3
    return pl.pallas_call(
        _even_proj_kernel,
        grid=(nb, lt // tm),
        in_specs=[pl.BlockSpec((1, tm, d), row), full(g2), mod, mod, full(win), full(qg2), full(kvg2),
                  full(wq), full(wqr), full(wkk), full(wv),
                  pl.BlockSpec((tm, LANES), lambda b, t: (t, 0)), pl.BlockSpec((tm, LANES), lambda b, t: (t, 0))],
        out_specs=[pl.BlockSpec((1, tm, wide), row), pl.BlockSpec((1, tm, wide), row)] +
                  [pl.BlockSpec((1, tm, wide), row)] + [pl.BlockSpec((1, tm, half), row)] * 3,
        out_shape=outs,
        compiler_params=_cparams("parallel", "parallel"),
        name="even_project",
    )(xu, g2, scale, shift, win, qg2, kvg2, wq, wqr, wkk, wv, cos, sin)


MLA_KEY_CHUNK = 512


def _mla_kernel(q_ref, k_ref, v_ref, o_ref, s_a, s_b, p_a, p_b, *, seq, ctx_len):
    tq = q_ref.shape[1]
    n_lat_tiles = seq // tq
    t = pl.program_id(1)
    lane = lax.broadcasted_iota(jnp.int32, (tq, LANES), 1)
    stages = ((s_a, p_a), (s_b, p_b))
    head_lanes = lambda h: slice(h * LANES, (h + 1) * LANES)

    def run(key0, n_keys):
        def scores(h):
            s_scr = stages[h % 2][0]
            q = q_ref[0, :, head_lanes(h)]
            m = None
            for c0 in range(0, n_keys, MLA_KEY_CHUNK):
                w = min(MLA_KEY_CHUNK, n_keys - c0)
                s = _dot_nt(q, k_ref[0, key0 + c0:key0 + c0 + w, head_lanes(h)])
                s_scr[:, c0:c0 + w] = s
                mc = jnp.max(s, axis=-1, keepdims=True)
                m = mc if m is None else jnp.maximum(m, mc)
            return m

        def attend(h, m):
            s_scr, p_scr = stages[h % 2]
            p_scr[:, 0:n_keys] = jnp.exp2(s_scr[:, 0:n_keys] - m).astype(BF16)
            acc = _dot(p_scr[:, 0:n_keys], v_ref[0, key0:key0 + n_keys, head_lanes(h)])
            return acc / acc[:, MLA_V:MLA_V + 1]

        m_next = scores(0)
        for h in range(MLA_HEADS):
            m_cur = m_next
            if h + 1 < MLA_HEADS:
                m_next = scores(h + 1)
            out = attend(h, m_cur)
            if h % 2 == 0:
                first = out
            else:
                pair = h // 2
                second = pltpu.roll(out, MLA_V, 1)
                o_ref[0, :, pair * LANES:(pair + 1) * LANES] = jnp.where(lane < MLA_V, first, second).astype(BF16)

    @pl.when(t < n_lat_tiles)
    def _():
        run(0, seq + ctx_len)

    @pl.when(t >= n_lat_tiles)
    def _():
        run(seq, ctx_len)


def _mla_attention(q, k, v, seq, ctx_len):
    nb, lt, wide = q.shape
    half = MLA_HEADS * MLA_V
    tq = 256
    kern = functools.partial(_mla_kernel, seq=seq, ctx_len=ctx_len)
    per_batch = pl.BlockSpec((1, lt, wide), lambda b, t: (b, 0, 0), pipeline_mode=pl.Buffered(1))
    return pl.pallas_call(
        kern,
        grid=(nb, lt // tq),
        in_specs=[pl.BlockSpec((1, tq, wide), lambda b, t: (b, t, 0)), per_batch, per_batch],
        out_specs=pl.BlockSpec((1, tq, half), lambda b, t: (b, t, 0)),
        out_shape=jax.ShapeDtypeStruct((nb, lt, half), BF16),
        scratch_shapes=[pltpu.VMEM((tq, lt), F32), pltpu.VMEM((tq, lt), F32),
                        pltpu.VMEM((tq, lt), BF16), pltpu.VMEM((tq, lt), BF16)],
        compiler_params=_cparams("parallel", "parallel"),
        name="mla_attention",
    )(q, k, v)


def _na_bias_tables(rpb, rows):
    kh = min(NA_WIN_H, rows)
    last_r0 = rows - NA_ROWS_PER_BLOCK
    tabs = []
    for r0 in (0, 2 * NA_ROWS_PER_BLOCK, last_r0):
        kstart = int(np.clip(r0 - kh // 2, 0, rows - NA_KEY_ROWS))
        r = r0 + np.arange(NA_ROWS_PER_BLOCK)
        rs = np.clip(r - kh // 2, 0, rows - kh)
        kr = kstart + np.arange(NA_KEY_ROWS)
        row_ok = (kr[None, :] >= rs[:, None]) & (kr[None, :] < rs[:, None] + kh)
        row_off = np.clip(kr[None, :] - r[:, None] + (NA_WIN_H - 1), 0, 2 * NA_WIN_H - 2)
        c = np.arange(GRID_W)
        cs = np.clip(c - NA_WIN_W // 2, 0, GRID_W - NA_WIN_W)
        col_ok = (c[None, :] >= cs[:, None]) & (c[None, :] < cs[:, None] + NA_WIN_W)
        col_off = np.clip(c[None, :] - c[:, None] + (NA_WIN_W - 1), 0, 2 * NA_WIN_W - 2)
        sel_r = jnp.asarray(row_off[..., None] == np.arange(2 * NA_WIN_H - 1), F32)
        sel_c = jnp.asarray(col_off[..., None] == np.arange(2 * NA_WIN_W - 1), F32)
        b = jnp.einsum('aeu,huv,cdv->haced', sel_r, rpb, sel_c, precision=lax.Precision.HIGHEST)
        ok = row_ok[:, None, :, None] & col_ok[None, :, None, :]
        b = jnp.where(jnp.asarray(ok)[None], b, NEG_BIG)
        tabs.append(b.reshape(rpb.shape[0], NA_ROWS_PER_BLOCK * GRID_W, NA_KEY_ROWS * GRID_W))
    return jnp.stack(tabs)


def _na_kernel(q_ref, k_ref, v_ref, bias_ref, o_ref, *, seq, ctx_len):
    tq = q_ref.shape[1]
    rows = seq // GRID_W
    n_lat_tiles = seq // tq
    nkeys = NA_KEY_ROWS * GRID_W
    t = pl.program_id(1)
    lane = lax.broadcasted_iota(jnp.int32, (tq, LANES), 1)
    kstart = jnp.clip(t * NA_ROWS_PER_BLOCK - NA_WIN_H // 2, 0, rows - NA_KEY_ROWS)
    koff = pl.multiple_of(kstart * GRID_W, GRID_W)

    def run(with_window):
        for pair in range(NA_HEADS // 2):
            sl = slice(pair * LANES, (pair + 1) * LANES)
            q2 = q_ref[0, :, sl].astype(F32)
            kc = k_ref[0, seq:seq + ctx_len, sl]
            vc = v_ref[0, seq:seq + ctx_len, sl]
            if with_window:
                kw = k_ref[0, pl.ds(koff, nkeys), sl]
                vw = v_ref[0, pl.ds(koff, nkeys), sl]
            outs = []
            for sub in range(2):
                keep = (lane < NA_HEAD_DIM) if sub == 0 else (lane >= NA_HEAD_DIM)
                q = jnp.where(keep, q2, 0.0).astype(BF16)
                s_c = _dot_nt(q, kc)
                m = jnp.max(s_c, axis=-1, keepdims=True)
                if with_window:
                    s_w = _dot_nt(q, kw) + bias_ref[0, 2 * pair + sub]
                    m = jnp.maximum(m, jnp.max(s_w, axis=-1, keepdims=True))
                p_c = jnp.exp(s_c - m)
                l = jnp.sum(p_c, axis=-1, keepdims=True)
                acc = _dot(p_c.astype(BF16), vc)
                if with_window:
                    p_w = jnp.exp(s_w - m)
                    l = l + jnp.sum(p_w, axis=-1, keepdims=True)
                    acc = acc + _dot(p_w.astype(BF16), vw)
                outs.append(acc / l)
            o_ref[0, :, sl] = jnp.where(lane < NA_HEAD_DIM, outs[0], outs[1]).astype(BF16)

    @pl.when(t < n_lat_tiles)
    def _():
        run(True)

    @pl.when(t >= n_lat_tiles)
    def _():
        run(False)


def _na_attention(q, k, v, bias, seq, ctx_len):
    nb, lt, w = q.shape
    tq = NA_ROWS_PER_BLOCK * GRID_W
    n_lat = seq // tq
    kern = functools.partial(_na_kernel, seq=seq, ctx_len=ctx_len)

    def variant(b, t):
        return (jnp.where(t == 0, 0, jnp.where(t >= n_lat - 1, 2, 1)), 0, 0, 0)

    return pl.pallas_call(
        kern,
        grid=(nb, lt // tq),
        in_specs=[pl.BlockSpec((1, tq, w), lambda b, t: (b, t, 0)),
                  pl.BlockSpec((1, lt, w), lambda b, t: (b, 0, 0)),
                  pl.BlockSpec((1, lt, w), lambda b, t: (b, 0, 0)),
                  pl.BlockSpec((1,) + bias.shape[1:], variant)],
        out_specs=pl.BlockSpec((1, tq, w), lambda b, t: (b, t, 0)),
        out_shape=jax.ShapeDtypeStruct((nb, lt, w), BF16),
        compiler_params=_cparams("parallel", "parallel"),
        name="na_attention",
    )(q, k, v, bias)


def _out_proj_kernel(x_ref, a_ref, b_ref, wa_ref, wb_ref, gate_ref, o_ref):
    y = _dot(a_ref[0], wa_ref[...]) + _dot(b_ref[0], wb_ref[...])
    o_ref[0] = x_ref[0] + gate_ref[0] * y


def _out_proj(xu, a, b, w_out, gate, seq):
    nb, lt, d = xu.shape
    tm = 256
    ka = a.shape[2]
    wa = w_out[:ka].astype(BF16)
    wb = w_out[ka:].astype(BF16)
    row = lambda bb, t: (bb, t, 0)
    full = lambda arr: pl.BlockSpec(arr.shape, lambda bb, t: (0,) * arr.ndim)
    return pl.pallas_call(
        _out_proj_kernel,
        grid=(nb, lt // tm),
        in_specs=[pl.BlockSpec((1, tm, d), row), pl.BlockSpec((1, tm, ka), row),
                  pl.BlockSpec((1, tm, b.shape[2]), row), full(wa), full(wb), _mod_spec(seq // tm, nb, d)],
        out_specs=pl.BlockSpec((1, tm, d), row),
        out_shape=jax.ShapeDtypeStruct(xu.shape, F32),
        compiler_params=_cparams("parallel", "parallel"),
        name="out_proj",
    )(xu, a, b, wa, wb, gate)


def _ffn_kernel(x_ref, g_ref, sc_ref, sh_ref, gate_ref, wg_ref, wu_ref, wd_ref, o_ref, h_scr, acc_scr):
    f = pl.program_id(2)

    @pl.when(f == 0)
    def _():
        h_scr[...] = _norm_mod(x_ref[0], g_ref[...], sc_ref[0], sh_ref[0]).astype(BF16)
        acc_scr[...] = jnp.zeros_like(acc_scr)

    h = h_scr[...]
    a = _dot(h, wg_ref[...])
    u = _dot(h, wu_ref[...])
    acc_scr[...] += _dot((a * jax.nn.sigmoid(a) * u).astype(BF16), wd_ref[...])

    @pl.when(f == pl.num_programs(2) - 1)
    def _():
        o_ref[0] = x_ref[0] + gate_ref[0] * acc_scr[...]


def _ffn(xu, norm_g, scale, shift, gate, w_gate, w_up, w_down, seq):
    nb, lt, d = xu.shape
    dff = w_gate.shape[1]
    tm = 256
    tf = dff
    row = lambda b, t, f: (b, t, 0)
    n_lat = seq // tm
    mod = pl.BlockSpec((1, 1, d), lambda b, t, f: (jnp.where(t < n_lat, b, nb), 0, 0))
    g2 = norm_g.reshape(1, d)
    once = pl.Buffered(1)
    return pl.pallas_call(
        _ffn_kernel,
        grid=(nb, lt // tm, dff // tf),
        in_specs=[pl.BlockSpec((1, tm, d), row), pl.BlockSpec((1, d), lambda b, t, f: (0, 0)), mod, mod, mod,
                  pl.BlockSpec((d, tf), lambda b, t, f: (0, f), pipeline_mode=once),
                  pl.BlockSpec((d, tf), lambda b, t, f: (0, f), pipeline_mode=once),
                  pl.BlockSpec((tf, d), lambda b, t, f: (f, 0), pipeline_mode=once)],
        out_specs=pl.BlockSpec((1, tm, d), row),
        out_shape=jax.ShapeDtypeStruct(xu.shape, F32),
        scratch_shapes=[pltpu.VMEM((tm, d), BF16), pltpu.VMEM((tm, d), F32)],
        compiler_params=_cparams("parallel", "parallel", "arbitrary"),
        name="dense_swiglu",
    )(xu, g2, scale, shift, gate, w_gate.astype(BF16), w_up.astype(BF16), w_down.astype(BF16))


def _s5_in_kernel(x_ref, g_ref, sc_ref, sh_ref, w_ref, o_ref):
    h = _norm_mod(x_ref[0], g_ref[...], sc_ref[0], sh_ref[0]).astype(BF16)
    u = _dot(h, w_ref[...])
    for k in range(o_ref.shape[0]):
        o_ref[k, 0] = u[:, k * LANES:(k + 1) * LANES]


def _s5_in(xu, norm_g, scale, shift, w_in, seq):
    nb, lt, d = xu.shape
    tm = 256
    w = w_in.astype(BF16)
    row = lambda b, t: (b, t, 0)
    mod = _mod_spec(seq // tm, nb, d)
    g2 = norm_g.reshape(1, d)
    return pl.pallas_call(
        _s5_in_kernel,
        grid=(nb, lt // tm),
        in_specs=[pl.BlockSpec((1, tm, d), row), pl.BlockSpec((1, d), lambda b, t: (0, 0)), mod, mod,
                  pl.BlockSpec(w.shape, lambda b, t: (0, 0))],
        out_specs=pl.BlockSpec((w.shape[1] // LANES, 1, tm, LANES), lambda b, t: (0, b, t, 0)),
        out_shape=jax.ShapeDtypeStruct((w.shape[1] // LANES, nb, lt, LANES), F32),
        compiler_params=_cparams("parallel", "parallel"),
        name="s5_in",
    )(xu, g2, scale, shift, w)


def _s5_operators(a_re, a_im, log_step, b_re, b_im, c_re, c_im):
    hp = lax.Precision.HIGHEST
    t_len = S5_CHUNK
    ops = []
    for direction in range(2):
        dt = jnp.exp(log_step[direction])[:, None]
        lre, lim = a_re[direction] * dt, a_im[direction] * dt
        decay = jnp.exp(lre)
        ab_re, ab_im = decay * jnp.cos(lim), decay * jnp.sin(lim)
        den = a_re[direction] ** 2 + a_im[direction] ** 2
        f_re = ((ab_re - 1) * a_re[direction] + ab_im * a_im[direction]) / den
        f_im = (ab_im * a_re[direction] - (ab_re - 1) * a_im[direction]) / den
        bb_re = f_re[..., None] * b_re[direction] - f_im[..., None] * b_im[direction]
        bb_im = f_re[..., None] * b_im[direction] + f_im[..., None] * b_re[direction]
        cr, ci = c_re[direction], c_im[direction]
        tau = jnp.arange(t_len + 1, dtype=F32)[:, None, None]
        pw = jnp.exp(tau * lre[None])
        pw_re, pw_im = pw * jnp.cos(tau * lim[None]), pw * jnp.sin(tau * lim[None])
        ab_b_re = pw_re[..., None] * bb_re[None] - pw_im[..., None] * bb_im[None]
        ab_b_im = pw_re[..., None] * bb_im[None] + pw_im[..., None] * bb_re[None]
        kk = (jnp.einsum('gip,tgpj->tgij', cr, ab_b_re, precision=hp)
              - jnp.einsum('gip,tgpj->tgij', ci, ab_b_im, precision=hp))
        ca_re = cr[None] * pw_re[:, :, None, :] - ci[None] * pw_im[:, :, None, :]
        ca_im = cr[None] * pw_im[:, :, None, :] + ci[None] * pw_re[:, :, None, :]
        s = np.arange(t_len)
        z_pow = (t_len - 1 - s) if direction == 0 else s
        c_pow = (s + 1) if direction == 0 else (t_len - s)
        ops.append((kk, [ab_b_re[z_pow], ab_b_im[z_pow]],
                    [ca_re[c_pow], -ca_im[c_pow]],
                    [pw_re[t_len], pw_im[t_len]]))
    lg = S5_LANE_GROUPS
    nblk = a_re.shape[1] // lg
    eye = jnp.eye(lg, dtype=BF16)
    n_state = 4 * lg * S5_STATE
    kf, kb = ops[0][0], ops[1][0]
    kd = jnp.concatenate([kb[t_len - 1:0:-1], (kf[0] + kb[0])[None], kf[1:t_len]])
    kd = kd.astype(BF16).reshape(2 * t_len - 1, nblk, lg, S5_GROUP, S5_GROUP).transpose(1, 0, 4, 2, 3)
    d_blk = kd[:, :, None] * eye[:, None, :, None]
    d_cat = d_blk.reshape(nblk, 2 * t_len - 1, LANES, LANES).transpose(0, 2, 1, 3).reshape(nblk, LANES, -1)
    wz4 = jnp.stack(ops[0][1] + ops[1][1]).astype(BF16)
    wz4 = wz4.reshape(4, t_len, nblk, lg, S5_STATE, S5_GROUP).transpose(2, 1, 5, 0, 3, 4)
    wc4 = jnp.stack(ops[0][2] + ops[1][2]).astype(BF16)
    wc4 = wc4.reshape(4, t_len, nblk, lg, S5_GROUP, S5_STATE).transpose(2, 1, 4, 0, 3, 5)
    compact = (nblk, t_len * S5_GROUP, n_state)
    group_of_lane = (np.arange(n_state) // S5_STATE) % lg
    mask = np.broadcast_to((group_of_lane[None, :] == np.arange(lg)[:, None])[:, None, :], (lg, S5_GROUP, n_state))
    a_t = jnp.stack(ops[0][3] + ops[1][3]).reshape(4, nblk, lg * S5_STATE).transpose(1, 0, 2)
    return d_cat, wz4.reshape(compact), wc4.reshape(compact), jnp.asarray(mask, BF16), a_t


def _s5_core_kernel(u_ref, dcat_ref, wzc_ref, wcc_ref, mask_ref, at_ref, y_ref,
                    wi_ref, wz_ref, wct_ref, z_scr, h_scr, *, n_ctx_chunks):
    n_chunks = z_scr.shape[0]
    n_lat = n_chunks - n_ctx_chunks
    w = z_scr.shape[1] // 4
    width = wi_ref.shape[1]

    @pl.when(pl.program_id(1) == 0)
    def _():
        for s in range(S5_CHUNK):
            lag0 = (S5_CHUNK - 1 - s) * LANES
            wi_ref[s * LANES:(s + 1) * LANES, :] = dcat_ref[0, :, lag0:lag0 + width]
            rows = slice(s * S5_GROUP, (s + 1) * S5_GROUP)
            for a in range(S5_LANE_GROUPS):
                dst = slice(s * LANES + a * S5_GROUP, s * LANES + (a + 1) * S5_GROUP)
                wz_ref[dst, :] = wzc_ref[0, rows, :] * mask_ref[a]
                wct_ref[dst, :] = wcc_ref[0, rows, :] * mask_ref[a]

    a = jnp.concatenate([u_ref[0, 0, pl.ds(s, n_chunks, stride=S5_CHUNK), :].astype(BF16)
                         for s in range(S5_CHUNK)], axis=1)
    z_scr[...] = _dot(a, wz_ref[...])
    ar_f, ai_f, ar_b, ai_b = [at_ref[0, k:k + 1, :] for k in range(4)]

    def step(k, carry):
        (fr, fi), (br, bi) = carry
        cf = jnp.where(k < n_ctx_chunks, n_lat + k, k - n_ctx_chunks)
        cb = n_chunks - 1 - k
        rf, rb = pl.ds(cf, 1), pl.ds(cb, 1)
        h_scr[rf, 0:w] = fr
        h_scr[rf, w:2 * w] = fi
        h_scr[rb, 2 * w:3 * w] = br
        h_scr[rb, 3 * w:4 * w] = bi
        fwd = (ar_f * fr - ai_f * fi + z_scr[rf, 0:w], ar_f * fi + ai_f * fr + z_scr[rf, w:2 * w])
        bwd = (ar_b * br - ai_b * bi + z_scr[rb, 2 * w:3 * w], ar_b * bi + ai_b * br + z_scr[rb, 3 * w:4 * w])
        return fwd, bwd

    zero = jnp.zeros((1, w), F32)
    lax.fori_loop(0, n_chunks, step, ((zero, zero), (zero, zero)))
    y = _dot(a, wi_ref[...]) + _dot_nt(h_scr[...].astype(BF16), wct_ref[...])
    for t in range(S5_CHUNK):
        y_ref[0, 0, pl.ds(t, n_chunks, stride=S5_CHUNK), :] = y[:, t * LANES:(t + 1) * LANES]


def _s5_core(u_blocks, ops, n_ctx_chunks):
    nblk, nb, lt, lanes = u_blocks.shape
    n_chunks = lt // S5_CHUNK
    d_cat, wz_c, wc_c, mask, a_t = ops
    width = S5_CHUNK * lanes
    n_state = wz_c.shape[2]
    kern = functools.partial(_s5_core_kernel, n_ctx_chunks=n_ctx_chunks)
    per_block = lambda a: pl.BlockSpec((1,) + a.shape[1:], lambda g, b: (g, 0, 0))
    tok = pl.BlockSpec((1, 1, lt, lanes), lambda g, b: (g, b, 0, 0))
    return pl.pallas_call(
        kern,
        grid=(nblk, nb),
        in_specs=[tok, per_block(d_cat), per_block(wz_c), per_block(wc_c),
                  pl.BlockSpec(mask.shape, lambda g, b: (0, 0, 0)), per_block(a_t)],
        out_specs=tok,
        out_shape=jax.ShapeDtypeStruct(u_blocks.shape, F32),
        scratch_shapes=[pltpu.VMEM((width, width), BF16), pltpu.VMEM((width, n_state), BF16),
                        pltpu.VMEM((width, n_state), BF16),
                        pltpu.VMEM((n_chunks, n_state), F32), pltpu.VMEM((n_chunks, n_state), F32)],
        compiler_params=_cparams("arbitrary", "arbitrary"),
        name="s5_core",
    )(u_blocks, d_cat, wz_c, wc_c, mask, a_t)


def _s5_out_kernel(x_ref, u_ref, y_ref, d_ref, w_ref, gate_ref, o_ref):
    d = x_ref.shape[2]
    u = jnp.concatenate([u_ref[k, 0] for k in range(u_ref.shape[0])], axis=1)
    y = jnp.concatenate([y_ref[k, 0] for k in range(y_ref.shape[0])], axis=1)
    y = u * d_ref[...] + y
    z = _dot(jax.nn.gelu(y).astype(BF16), w_ref[...])
    o_ref[0] = x_ref[0] + gate_ref[0] * (z[:, :d] * jax.nn.sigmoid(z[:, d:]))


def _s5_out(x, u_blocks, y_blocks, d_skip, w_glu, gate):
    nb, seq, d = x.shape
    tm = 256
    nblk, _, _, lanes = u_blocks.shape
    w = nblk * lanes
    row = lambda b, t: (b, t, 0)
    blk = pl.BlockSpec((nblk, 1, tm, lanes), lambda b, t: (0, b, t, 0))
    wg = w_glu.astype(BF16)
    return pl.pallas_call(
        _s5_out_kernel,
        grid=(nb, seq // tm),
        in_specs=[pl.BlockSpec((1, tm, d), row), blk, blk,
                  pl.BlockSpec((1, w), lambda b, t: (0, 0)), pl.BlockSpec(wg.shape, lambda b, t: (0, 0)),
                  pl.BlockSpec((1, 1, d), lambda b, t: (b, 0, 0))],
        out_specs=pl.BlockSpec((1, tm, d), row),
        out_shape=jax.ShapeDtypeStruct(x.shape, F32),
        compiler_params=_cparams("parallel", "parallel"),
        name="s5_out",
    )(x, u_blocks, y_blocks, d_skip.reshape(1, w), wg, gate)


R_IDX, R_WT, R_RANK = 0, TOP_K, 2 * TOP_K


def _router_kernel(x_ref, g_ref, sc_ref, sh_ref, w_ref, tri_ref, rec_ref, cnt_ref, base_scr):
    @pl.when((pl.program_id(0) == 0) & (pl.program_id(1) == 0))
    def _():
        base_scr[...] = jnp.zeros_like(base_scr)

    t = _norm_mod(x_ref[0], g_ref[...], sc_ref[0], sh_ref[0]).astype(BF16)
    logits = _dot(t, w_ref[...])
    lane = lax.broadcasted_iota(jnp.int32, logits.shape, 1)
    big = jnp.int32(LANES)
    l1 = jnp.where(lane < N_EXPERTS, logits, -jnp.inf)
    v1 = jnp.max(l1, axis=-1, keepdims=True)
    i1 = jnp.min(jnp.where(l1 == v1, lane, big), axis=-1, keepdims=True)
    l2 = jnp.where(lane == i1, -jnp.inf, l1)
    v2 = jnp.max(l2, axis=-1, keepdims=True)
    i2 = jnp.min(jnp.where(l2 == v2, lane, big), axis=-1, keepdims=True)
    e2 = jnp.exp(v2 - v1)
    w1 = 1.0 / (1.0 + e2)
    w2 = e2 / (1.0 + e2)
    hit = jnp.where((lane == i1) | (lane == i2), 1.0, 0.0)
    before = _dot(tri_ref[...], hit.astype(BF16)) + base_scr[0:1, :]
    r1 = jnp.sum(jnp.where(lane == i1, before, 0.0), axis=-1, keepdims=True)
    r2 = jnp.sum(jnp.where(lane == i2, before, 0.0), axis=-1, keepdims=True)
    base_scr[...] = base_scr[...] + jnp.sum(hit, axis=0, keepdims=True)
    cnt_ref[...] = base_scr[...]
    rec = jnp.zeros(logits.shape, F32)
    for k, val in enumerate((i1.astype(F32), i2.astype(F32), w1, w2, r1, r2)):
        rec = jnp.where(lane == k, val, rec)
    rec_ref[0] = rec


def _router(x, norm_g, scale, shift, w_router):
    nb, seq, d = x.shape
    tm = 256
    wr = jnp.zeros((d, LANES), F32).at[:, :N_EXPERTS].set(w_router).astype(BF16)
    tri = jnp.asarray(np.tril(np.ones((tm, tm), np.float32), -1), BF16)
    row = lambda b, t: (b, t, 0)
    mod = pl.BlockSpec((1, 1, d), lambda b, t: (b, 0, 0))
    g2 = norm_g.reshape(1, d)
    return pl.pallas_call(
        _router_kernel,
        grid=(nb, seq // tm),
        in_specs=[pl.BlockSpec((1, tm, d), row), pl.BlockSpec((1, d), lambda b, t: (0, 0)), mod, mod,
                  pl.BlockSpec(wr.shape, lambda b, t: (0, 0)), pl.BlockSpec(tri.shape, lambda b, t: (0, 0))],
        out_specs=[pl.BlockSpec((1, tm, LANES), row), pl.BlockSpec((8, LANES), lambda b, t: (0, 0))],
        out_shape=[jax.ShapeDtypeStruct((nb, seq, LANES), F32), jax.ShapeDtypeStruct((8, LANES), F32)],
        scratch_shapes=[pltpu.VMEM((8, LANES), F32)],
        compiler_params=_cparams("arbitrary", "arbitrary"),
        name="router",
    )(x, g2, scale, shift, wr, tri)


def _row_copies_wait(src_ref, dst_ref, sem):
    pltpu.make_async_copy(src_ref, dst_ref, sem).wait()


def _dispatch_kernel(dest_ref, fill_ref, x_ref, g_ref, sc_ref, sh_ref, xs_ref, tbuf, zbuf, sem, zsem):
    tm = x_ref.shape[1]
    ztm = zbuf.shape[0]
    step = pl.program_id(0) * pl.num_programs(1) + pl.program_id(1)
    n_steps = pl.num_programs(0) * pl.num_programs(1)
    slot = step % 2

    def drain(s):
        for _ in range(TOP_K):
            _row_copies_wait(tbuf.at[s], xs_ref.at[pl.ds(0, tm)], sem.at[s])

    @pl.when(step == 0)
    def _():
        zbuf[...] = jnp.zeros_like(zbuf)

        def fill(start):
            def body(i, carry):
                @pl.when(fill_ref[i] > 0)
                def _():
                    cp = pltpu.make_async_copy(zbuf, xs_ref.at[pl.ds(pl.multiple_of(i * ztm, ztm), ztm)], zsem)
                    if start:
                        cp.start()
                    else:
                        cp.wait()
                return carry
            lax.fori_loop(0, fill_ref.shape[0], body, 0)

        fill(True)
        fill(False)

    @pl.when(step >= 2)
    def _():
        drain(slot)

    tbuf[slot] = _norm_mod(x_ref[0], g_ref[...], sc_ref[0], sh_ref[0])
    base = step * (tm * TOP_K)

    def body(r, carry):
        for k in range(TOP_K):
            dst = dest_ref[base + r * TOP_K + k]
            pltpu.make_async_copy(tbuf.at[slot, pl.ds(r, 1)], xs_ref.at[pl.ds(dst, 1)],
                                  sem.at[slot]).start(priority=k % 2)
        return carry

    lax.fori_loop(0, tm, body, 0, unroll=8)

    @pl.when(step == n_steps - 1)
    def _():
        drain(slot)

        @pl.when(n_steps > 1)
        def _():
            drain(1 - slot)


def _dispatch(x, norm_g, scale, shift, dest, tile_fill, n_rows, expert_tm):
    nb, seq, d = x.shape
    tm = 256
    row = lambda b, t, dr, fr: (b, t, 0)
    mod = pl.BlockSpec((1, 1, d), lambda b, t, dr, fr: (b, 0, 0))
    g2 = norm_g.reshape(1, d)
    grid_spec = pltpu.PrefetchScalarGridSpec(
        num_scalar_prefetch=2,
        grid=(nb, seq // tm),
        in_specs=[pl.BlockSpec((1, tm, d), row), pl.BlockSpec((1, d), lambda b, t, dr, fr: (0, 0)), mod, mod],
        out_specs=pl.BlockSpec(memory_space=pl.ANY),
        scratch_shapes=[pltpu.VMEM((2, tm, d), F32), pltpu.VMEM((expert_tm, d), F32),
                        pltpu.SemaphoreType.DMA((2,)), pltpu.SemaphoreType.DMA(())],
    )
    return pl.pallas_call(
        _dispatch_kernel,
        grid_spec=grid_spec,
        out_shape=jax.ShapeDtypeStruct((n_rows, d), F32),
        compiler_params=_cparams("arbitrary", "arbitrary"),
        name="dispatch",
    )(dest, tile_fill, x, g2, scale, shift)


def _expert_kernel(te_ref, tr_ref, x_ref, wg_ref, wu_ref, wd_ref, o_ref, xb_scr, acc_scr):
    i = pl.program_id(0)
    f = pl.program_id(1)
    nf = pl.num_programs(1)
    live = tr_ref[i] > 0

    @pl.when(jnp.logical_not(live) & (f == nf - 1))
    def _():
        o_ref[...] = jnp.zeros_like(o_ref)

    @pl.when(live)
    def _():
        @pl.when(f == 0)
        def _():
            xb_scr[...] = x_ref[...].astype(BF16)
            acc_scr[...] = jnp.zeros_like(acc_scr)

        x = xb_scr[...]
        a = _dot(x, wg_ref[0])
        u = _dot(x, wu_ref[0])
        acc_scr[...] += _dot((a * jax.nn.sigmoid(a) * u).astype(BF16), wd_ref[0])

        @pl.when(f == nf - 1)
        def _():
            o_ref[...] = acc_scr[...]


def _experts(xs, tile_expert, tile_rows, w_gate, w_up, w_down, tm, tf):
    n_rows, d = xs.shape
    dff = w_gate.shape[2]
    nf = dff // tf

    def fsel(i, f, te, tr):
        return jnp.where(tr[i] > 0, f, nf - 1)

    grid_spec = pltpu.PrefetchScalarGridSpec(
        num_scalar_prefetch=2,
        grid=(n_rows // tm, nf),
        in_specs=[pl.BlockSpec((tm, d), lambda i, f, te, tr: (i, 0)),
                  pl.BlockSpec((1, d, tf), lambda i, f, te, tr: (te[i], 0, fsel(i, f, te, tr))),
                  pl.BlockSpec((1, d, tf), lambda i, f, te, tr: (te[i], 0, fsel(i, f, te, tr))),
                  pl.BlockSpec((1, tf, d), lambda i, f, te, tr: (te[i], fsel(i, f, te, tr), 0))],
        out_specs=pl.BlockSpec((tm, d), lambda i, f, te, tr: (i, 0)),
        scratch_shapes=[pltpu.VMEM((tm, d), BF16), pltpu.VMEM((tm, d), F32)],
    )
    return pl.pallas_call(
        _expert_kernel,
        grid_spec=grid_spec,
        out_shape=jax.ShapeDtypeStruct((n_rows, d), F32),
        compiler_params=_cparams("arbitrary", "arbitrary"),
        name="experts",
    )(tile_expert, tile_rows, xs, w_gate, w_up, w_down)


def _combine_kernel(dest_ref, x_ref, rec_ref, gate_ref, g_ref, ys_ref, o_ref, ybuf, sem):
    tm = x_ref.shape[1]
    step = pl.program_id(0) * pl.num_programs(1) + pl.program_id(1)
    n_steps = pl.num_programs(0) * pl.num_programs(1)
    slot = step % 2

    def issue(st, sl):
        base = st * (tm * TOP_K)

        def body(r, carry):
            for k in range(TOP_K):
                src = dest_ref[base + r * TOP_K + k]
                pltpu.make_async_copy(ys_ref.at[pl.ds(src, 1)], ybuf.at[sl, k, pl.ds(r, 1)],
                                      sem.at[sl]).start(priority=k % 2)
            return carry

        lax.fori_loop(0, tm, body, 0, unroll=8)

    @pl.when(step == 0)
    def _():
        issue(step, slot)

    @pl.when(step + 1 < n_steps)
    def _():
        issue(step + 1, 1 - slot)

    for k in range(TOP_K):
        _row_copies_wait(ys_ref.at[pl.ds(0, tm)], ybuf.at[slot, k], sem.at[slot])

    rec = rec_ref[0]
    y = rec[:, R_WT:R_WT + 1] * ybuf[slot, 0] + rec[:, R_WT + 1:R_WT + 2] * ybuf[slot, 1]
    x = x_ref[0] + gate_ref[0] * y
    o_ref[0] = _rms(x) * g_ref[...]


def _combine(x, rec, gate, final_g, ys, dest):
    nb, seq, d = x.shape
    tm = 256
    row = lambda b, t, dr: (b, t, 0)
    grid_spec = pltpu.PrefetchScalarGridSpec(
        num_scalar_prefetch=1,
        grid=(nb, seq // tm),
        in_specs=[pl.BlockSpec((1, tm, d), row), pl.BlockSpec((1, tm, LANES), row),
                  pl.BlockSpec((1, 1, d), lambda b, t, dr: (b, 0, 0)), pl.BlockSpec((1, d), lambda b, t, dr: (0, 0)),
                  pl.BlockSpec(memory_space=pl.ANY)],
        out_specs=pl.BlockSpec((1, tm, d), row),
        scratch_shapes=[pltpu.VMEM((2, TOP_K, tm, d), F32), pltpu.SemaphoreType.DMA((2,))],
    )
    return pl.pallas_call(
        _combine_kernel,
        grid_spec=grid_spec,
        out_shape=jax.ShapeDtypeStruct(x.shape, F32),
        compiler_params=_cparams("arbitrary", "arbitrary"),
        name="combine_final_norm",
    )(dest, x, rec, gate, final_g.reshape(1, d), ys)


def _moe(x, norm_g, scale, shift, gate, w_router, w_gate, w_up, w_down, final_g):
    nb, seq, d = x.shape
    n_tok = nb * seq
    tm, tf = 512, w_gate.shape[2] // 2
    rec, cnt = _router(x, norm_g, scale, shift, w_router)
    counts = cnt[0, :N_EXPERTS].astype(jnp.int32)
    padded = ((counts + tm - 1) // tm) * tm
    pend = jnp.cumsum(padded)
    pstart = pend - padded
    idx = rec[:, :, R_IDX:R_IDX + TOP_K].astype(jnp.int32)
    rank = rec[:, :, R_RANK:R_RANK + TOP_K].astype(jnp.int32)
    seg = jnp.sum(jnp.where(idx[..., None] == jnp.arange(N_EXPERTS), pstart, 0), axis=-1)
    dest = (seg + rank).reshape(n_tok * TOP_K)
    n_rows = n_tok * TOP_K + N_EXPERTS * tm
    tile_start = jnp.arange(n_rows // tm, dtype=jnp.int32) * tm
    tile_expert = jnp.sum((tile_start[:, None] >= pend[None, :]).astype(jnp.int32), axis=1)
    live = tile_expert < N_EXPERTS
    last_live = jnp.max(jnp.where(live, tile_expert, 0))
    tile_expert = jnp.where(live, tile_expert, last_live)
    seg_rows = jnp.sum(jnp.where(tile_expert[:, None] == jnp.arange(N_EXPERTS), (pstart + counts)[None, :], 0), axis=1)
    tile_rows = jnp.where(live, jnp.clip(seg_rows - tile_start, 0, tm), 0).astype(jnp.int32)
    tile_fill = (tile_rows < tm).astype(jnp.int32)

    xs = _dispatch(x, norm_g, scale, shift, dest, tile_fill, n_rows, tm)
    ys = _experts(xs, tile_expert.astype(jnp.int32), tile_rows,
                  w_gate.astype(BF16), w_up.astype(BF16), w_down.astype(BF16), tm, tf)
    return _combine(x, rec, gate, final_g, ys, dest)


def kernel(x, c, ctx, c_ctx, mod_w, mod_b, norm1_g, norm2_g, ev_w_in, ev_q_norm_g, ev_w_qb, ev_kv_norm_g, ev_w_kvb, ev_na_rpb, ev_w_out, ev_ffn_w_gate, ev_ffn_w_up, ev_ffn_w_down, od_w_in, od_a_re, od_a_im, od_log_step, od_b_re, od_b_im, od_c_re, od_c_im, od_d, od_w_glu, moe_w_router, moe_w_gate, moe_w_up, moe_w_down, final_g):
    nb, seq, d = x.shape
    ctx_len = ctx.shape[1]
    assert mod_w.shape[0] == 2 and nb < 8
    assert seq % (NA_ROWS_PER_BLOCK * GRID_W) == 0 and ctx_len == NA_ROWS_PER_BLOCK * GRID_W

    cond = jnp.zeros((8, d), F32).at[:nb].set(c).at[nb].set(c_ctx)
    mods = _adaln(cond, mod_w, mod_b)

    def mod_parts(layer):
        return [mods[layer, :, i * d:(i + 1) * d].reshape(8, 1, d) for i in range(N_MOD)]

    xu = jnp.concatenate([x, ctx], axis=1)

    sh1, sc1, g1, sh2, sc2, g2 = mod_parts(0)
    cos, sin = _rope_tables(seq, ctx_len)
    weights = _even_weights(ev_w_in[0], ev_w_qb[0], ev_w_kvb[0])
    q, k, v, nq, nk, nv = _even_project(xu, norm1_g[0], sc1, sh1, weights, ev_q_norm_g[0], ev_kv_norm_g[0],
                                        cos, sin, seq)
    mla = _mla_attention(q, k, v, seq, ctx_len)
    bias = _na_bias_tables(ev_na_rpb[0], seq // GRID_W)
    na = _na_attention(nq, nk, nv, bias, seq, ctx_len)
    xu = _out_proj(xu, mla, na, ev_w_out[0], g1, seq)
    xu = _ffn(xu, norm2_g[0], sc2, sh2, g2, ev_ffn_w_gate[0], ev_ffn_w_up[0], ev_ffn_w_down[0], seq)

    sh1, sc1, g1, sh2, sc2, g2 = mod_parts(1)
    u = _s5_in(xu, norm1_g[1], sc1, sh1, od_w_in[0], seq)
    ops = _s5_operators(od_a_re[0], od_a_im[0], od_log_step[0], od_b_re[0], od_b_im[0], od_c_re[0], od_c_im[0])
    y = _s5_core(u, ops, ctx_len // S5_CHUNK)
    xl = xu[:, :seq]
    xl = _s5_out(xl, u, y, od_d[0], od_w_glu[0], g1[:nb])
    return _moe(xl, norm2_g[1], sc2[:nb], sh2[:nb], g2[:nb], moe_w_router[0], moe_w_gate[0], moe_w_up[0],
                moe_w_down[0], final_g)
```

```python
import functools
import math

import numpy as np
import jax
import jax.numpy as jnp
from jax import lax
from jax.experimental import pallas as pl
from jax.experimental.pallas import tpu as pltpu

F32 = jnp.float32
BF16 = jnp.bfloat16

LANES = 128
VMEM_LIMIT_BYTES = 52 * 1024 * 1024

NORM_EPS = 1e-6
ROPE_BASE = 10000.0
GRID_W = 64
N_MOD = 6

MLA_HEADS = 8
MLA_NOPE = 64
MLA_ROPE = 32
MLA_V = 64
Q_LORA = 384
KV_LORA = 256
MLA_SCALE = (MLA_NOPE + MLA_ROPE) ** -0.5
MLA_EXP2_SCALE = MLA_SCALE * math.log2(math.e)

NA_HEADS = 8
NA_HEAD_DIM = 64
NA_WIN_H = 8
NA_WIN_W = 16
NA_SCALE = NA_HEAD_DIM ** -0.5
NA_EXP2_SCALE = NA_SCALE * math.log2(math.e)
NA_ROWS_PER_BLOCK = 4
NA_KEY_ROWS = NA_ROWS_PER_BLOCK + NA_WIN_H - 1

S5_GROUP = 16
S5_STATE = 64
S5_CHUNK = 16
S5_LANE_GROUPS = LANES // S5_GROUP

N_EXPERTS = 8
TOP_K = 2

NEG_BIG = -1e30


def _cparams(*sem):
    return pltpu.CompilerParams(dimension_semantics=sem, vmem_limit_bytes=VMEM_LIMIT_BYTES)


def _rms(x):
    return x * lax.rsqrt(jnp.mean(x * x, axis=-1, keepdims=True) + NORM_EPS)


def _norm_mod(x, g, scale, shift):
    return (_rms(x) * g) * (1 + scale) + shift


def _dot(a, b):
    return jnp.dot(a, b, preferred_element_type=F32)


def _dot_nt(a, b):
    return lax.dot_general(a, b, (((1,), (1,)), ((), ())), preferred_element_type=F32)


def _adaln_kernel(c_ref, w_ref, b_ref, o_ref):
    c = c_ref[...]
    s = (c * jax.nn.sigmoid(c)).astype(BF16)
    o_ref[0] = _dot(s, w_ref[0].astype(BF16)) + b_ref[0]


def _adaln(cond, mod_w, mod_b):
    nl, d, n = mod_w.shape
    tn = 1536
    return pl.pallas_call(
        _adaln_kernel,
        grid=(nl, n // tn),
        in_specs=[pl.BlockSpec((8, d), lambda l, j: (0, 0)),
                  pl.BlockSpec((1, d, tn), lambda l, j: (l, 0, j)),
                  pl.BlockSpec((1, 1, tn), lambda l, j: (l, 0, j))],
        out_specs=pl.BlockSpec((1, 8, tn), lambda l, j: (l, 0, j)),
        out_shape=jax.ShapeDtypeStruct((nl, 8, n), F32),
        compiler_params=_cparams("parallel", "parallel"),
        name="adaln",
    )(cond, mod_w, mod_b.reshape(nl, 1, n))


def _even_proj_kernel(x_ref, g_ref, sc_ref, sh_ref, win_ref, qg_ref, kvg_ref, wq_ref, wqr_ref, wkk_ref,
                      wkv_ref, cos_ref, sin_ref, q_ref, k_ref, v_ref, nq_ref, nk_ref, nv_ref):
    h = _norm_mod(x_ref[0], g_ref[...], sc_ref[0], sh_ref[0]).astype(BF16)
    p = _dot(h, win_ref[...])
    c0, c1, c2, c3 = Q_LORA, Q_LORA + KV_LORA, Q_LORA + KV_LORA + LANES, Q_LORA + KV_LORA + 2 * LANES
    cqn = (_rms(p[:, :c0]) * qg_ref[...]).astype(BF16)
    ckvn = (_rms(p[:, c0:c1]) * kvg_ref[...]).astype(BF16)
    cos = cos_ref[...]
    sin = sin_ref[...]
    kr = p[:, c1:c2] * cos + p[:, c2:c3] * sin
    qa = _dot(cqn, wq_ref[...])
    qb = _dot(cqn, wqr_ref[...])
    kk = _dot(ckvn, wkk_ref[...])
    for hd in range(MLA_HEADS):
        sl = slice(hd * LANES, (hd + 1) * LANES)
        q_ref[0, :, sl] = ((qa[:, sl] * cos + qb[:, sl] * sin) * MLA_EXP2_SCALE).astype(BF16)
        k_ref[0, :, sl] = (kk[:, sl] + kr).astype(BF16)
    vlane = lax.broadcasted_iota(jnp.int32, (1, MLA_HEADS * LANES), 1) & (LANES - 1)
    v_ref[0] = (_dot(ckvn, wkv_ref[...]) + jnp.where(vlane == MLA_V, 1.0, 0.0)).astype(BF16)
    w = NA_HEADS * NA_HEAD_DIM
    nq_ref[0] = (p[:, c3:c3 + w] * NA_EXP2_SCALE).astype(BF16)
    nk_ref[0] = p[:, c3 + w:c3 + 2 * w].astype(BF16)
    nv_ref[0] = (p[:, c3 + 2 * w:c3 + 2 * w + NA_HEADS * LANES]
                 + jnp.where(vlane == NA_HEAD_DIM, 1.0, 0.0)).astype(BF16)


def _rot_half_cols(w):
    q = MLA_ROPE // 4
    return jnp.concatenate([-w[:, q:2 * q], w[:, :q], -w[:, 3 * q:], w[:, 2 * q:3 * q]], axis=1)


def _rope_tables(seq, ctx_len):
    q = MLA_ROPE // 4
    t = np.arange(seq)
    inv_freq = np.float32(ROPE_BASE) ** (-np.arange(q, dtype=np.float32) / np.float32(q))
    ang_r = (t // GRID_W).astype(np.float32)[:, None] * inv_freq[None, :]
    ang_c = (t % GRID_W).astype(np.float32)[:, None] * inv_freq[None, :]
    cos = np.ones((seq + ctx_len, LANES), np.float32)
    sin = np.zeros((seq + ctx_len, LANES), np.float32)
    cos[:seq, MLA_NOPE:MLA_NOPE + MLA_ROPE] = np.concatenate([np.cos(ang_r)] * 2 + [np.cos(ang_c)] * 2, axis=1)
    sin[:seq, MLA_NOPE:MLA_NOPE + MLA_ROPE] = np.concatenate([np.sin(ang_r)] * 2 + [np.sin(ang_c)] * 2, axis=1)
    return jnp.asarray(cos), jnp.asarray(sin)


def _even_weights(w_in, w_qb, w_kvb):
    d = w_in.shape[0]
    c1 = Q_LORA + KV_LORA
    wkr = w_in[:, c1:c1 + MLA_ROPE]
    pad = lambda w: jnp.zeros((d, LANES), F32).at[:, MLA_NOPE:MLA_NOPE + MLA_ROPE].set(w)
    w_na = NA_HEADS * NA_HEAD_DIM
    na_qk = w_in[:, c1 + MLA_ROPE:c1 + MLA_ROPE + 2 * w_na]
    na_v = w_in[:, c1 + MLA_ROPE + 2 * w_na:].reshape(d, NA_HEADS, NA_HEAD_DIM)
    na_v = jnp.concatenate([na_v, jnp.zeros((d, NA_HEADS, LANES - NA_HEAD_DIM), F32)], axis=2)
    win = jnp.concatenate([w_in[:, :c1], pad(wkr), pad(_rot_half_cols(wkr)), na_qk,
                           na_v.reshape(d, NA_HEADS * LANES)], axis=1)
    hq = MLA_NOPE + MLA_ROPE
    wq = w_qb.reshape(Q_LORA, MLA_HEADS, hq)
    zq = jnp.zeros((Q_LORA, MLA_HEADS, LANES - hq), F32)
    wq_main = jnp.concatenate([wq, zq], axis=2).reshape(Q_LORA, MLA_HEADS * LANES)
    rot = jnp.stack([_rot_half_cols(wq[:, h, MLA_NOPE:]) for h in range(MLA_HEADS)], axis=1)
    wq_rot = jnp.concatenate([jnp.zeros((Q_LORA, MLA_HEADS, MLA_NOPE), F32), rot, zq], axis=2)
    wq_rot = wq_rot.reshape(Q_LORA, MLA_HEADS * LANES)
    wkv = w_kvb.reshape(KV_LORA, MLA_HEADS, MLA_NOPE + MLA_V)
    wkk = jnp.concatenate([wkv[:, :, :MLA_NOPE], jnp.zeros((KV_LORA, MLA_HEADS, LANES - MLA_NOPE), F32)], axis=2)
    wkk = wkk.reshape(KV_LORA, MLA_HEADS * LANES)
    wv = jnp.concatenate([wkv[:, :, MLA_NOPE:], jnp.zeros((KV_LORA, MLA_HEADS, LANES - MLA_V), F32)], axis=2)
    wv = wv.reshape(KV_LORA, MLA_HEADS * LANES)
    return tuple(a.astype(BF16) for a in (win, wq_main, wq_rot, wkk, wv))


def _mod_spec(n_lat_tiles, nb, d):
    return pl.BlockSpec((1, 1, d), lambda b, t: (jnp.where(t < n_lat_tiles, b, nb), 0, 0))


def _even_project(xu, norm_g, scale, shift, weights, q_norm_g, kv_norm_g, cos, sin, seq):
    nb, lt, d = xu.shape
    tm = 256
    win, wq, wqr, wkk, wv = weights
    n_lat = seq // tm
    row = lambda b, t: (b, t, 0)
    full = lambda a: pl.BlockSpec(a.shape, lambda b, t: (0,) * a.ndim)
    mod = _mod_spec(n_lat, nb, d)
    g2 = norm_g.reshape(1, d)
    qg2 = q_norm_g.reshape(1, Q_LORA)
    kvg2 = kv_norm_g.reshape(1, KV_LORA)
    wide = MLA_HEADS * LANES
    half = MLA_HEADS * MLA_V
    outs = [jax.ShapeDtypeStruct((nb, lt, wide), BF16), jax.ShapeDtypeStruct((nb, lt, wide), BF16)] + \
           [jax.ShapeDtypeStruct((nb, lt, wide), BF16)] + [jax.ShapeDtypeStruct((nb, lt, half), BF16)] * 2 + \
           [jax.ShapeDtypeStruct((nb, lt, wide), BF16)]
    return pl.pallas_call(
        _even_proj_kernel,
        grid=(nb, lt // tm),
        in_specs=[pl.BlockSpec((1, tm, d), row), full(g2), mod, mod, full(win), full(qg2), full(kvg2),
                  full(wq), full(wqr), full(wkk), full(wv),
                  pl.BlockSpec((tm, LANES), lambda b, t: (t, 0)), pl.BlockSpec((tm, LANES), lambda b, t: (t, 0))],
        out_specs=[pl.BlockSpec((1, tm, wide), row), pl.BlockSpec((1, tm, wide), row)] +
                  [pl.BlockSpec((1, tm, wide), row)] + [pl.BlockSpec((1, tm, half), row)] * 2 +
                  [pl.BlockSpec((1, tm, wide), row)],
        out_shape=outs,
        compiler_params=_cparams("parallel", "parallel"),
        name="even_project",
    )(xu, g2, scale, shift, win, qg2, kvg2, wq, wqr, wkk, wv, cos, sin)


MLA_KEY_CHUNK = 512


def _mla_kernel(q_ref, k_ref, v_ref, o_ref, s_a, s_b, p_a, p_b, *, seq, ctx_len):
    tq = q_ref.shape[1]
    n_lat_tiles = seq // tq
    t = pl.program_id(1)
    lane = lax.broadcasted_iota(jnp.int32, (tq, LANES), 1)
    stages = ((s_a, p_a), (s_b, p_b))
    head_lanes = lambda h: slice(h * LANES, (h + 1) * LANES)

    def run(key0, n_keys):
        def scores(h):
            s_scr = stages[h % 2][0]
            q = q_ref[0, :, head_lanes(h)]
            m = None
            for c0 in range(0, n_keys, MLA_KEY_CHUNK):
                w = min(MLA_KEY_CHUNK, n_keys - c0)
                s = _dot_nt(q, k_ref[0, key0 + c0:key0 + c0 + w, head_lanes(h)])
                s_scr[:, c0:c0 + w] = s
                for l0 in range(0, w, LANES):
                    m = s[:, l0:l0 + LANES] if m is None else jnp.maximum(m, s[:, l0:l0 + LANES])
            return jnp.max(m, axis=-1, keepdims=True)

        def attend(h, m):
            s_scr, p_scr = stages[h % 2]
            p_scr[:, 0:n_keys] = jnp.exp2((s_scr[:, 0:n_keys] - m).astype(BF16))
            acc = _dot(p_scr[:, 0:n_keys], v_ref[0, key0:key0 + n_keys, head_lanes(h)])
            return acc / acc[:, MLA_V:MLA_V + 1]

        m_next = scores(0)
        for h in range(MLA_HEADS):
            m_cur = m_next
            if h + 1 < MLA_HEADS:
                m_next = scores(h + 1)
            out = attend(h, m_cur)
            if h % 2 == 0:
                first = out
            else:
                pair = h // 2
                second = pltpu.roll(out, MLA_V, 1)
                o_ref[0, :, pair * LANES:(pair + 1) * LANES] = jnp.where(lane < MLA_V, first, second).astype(BF16)

    @pl.when(t < n_lat_tiles)
    def _():
        run(0, seq + ctx_len)

    @pl.when(t >= n_lat_tiles)
    def _():
        run(seq, ctx_len)


def _mla_attention(q, k, v, seq, ctx_len):
    nb, lt, wide = q.shape
    half = MLA_HEADS * MLA_V
    tq = 256
    kern = functools.partial(_mla_kernel, seq=seq, ctx_len=ctx_len)
    per_batch = pl.BlockSpec((1, lt, wide), lambda b, t: (b, 0, 0), pipeline_mode=pl.Buffered(1))
    return pl.pallas_call(
        kern,
        grid=(nb, lt // tq),
        in_specs=[pl.BlockSpec((1, tq, wide), lambda b, t: (b, t, 0)), per_batch, per_batch],
        out_specs=pl.BlockSpec((1, tq, half), lambda b, t: (b, t, 0)),
        out_shape=jax.ShapeDtypeStruct((nb, lt, half), BF16),
        scratch_shapes=[pltpu.VMEM((tq, lt), F32), pltpu.VMEM((tq, lt), F32),
                        pltpu.VMEM((tq, lt), BF16), pltpu.VMEM((tq, lt), BF16)],
        compiler_params=_cparams("parallel", "parallel"),
        name="mla_attention",
    )(q, k, v)


def _na_bias_tables(rpb, rows):
    kh = min(NA_WIN_H, rows)
    last_r0 = rows - NA_ROWS_PER_BLOCK
    tabs = []
    for r0 in (0, 2 * NA_ROWS_PER_BLOCK, last_r0):
        kstart = int(np.clip(r0 - kh // 2, 0, rows - NA_KEY_ROWS))
        r = r0 + np.arange(NA_ROWS_PER_BLOCK)
        rs = np.clip(r - kh // 2, 0, rows - kh)
        kr = kstart + np.arange(NA_KEY_ROWS)
        row_ok = (kr[None, :] >= rs[:, None]) & (kr[None, :] < rs[:, None] + kh)
        row_off = np.clip(kr[None, :] - r[:, None] + (NA_WIN_H - 1), 0, 2 * NA_WIN_H - 2)
        c = np.arange(GRID_W)
        cs = np.clip(c - NA_WIN_W // 2, 0, GRID_W - NA_WIN_W)
        col_ok = (c[None, :] >= cs[:, None]) & (c[None, :] < cs[:, None] + NA_WIN_W)
        col_off = np.clip(c[None, :] - c[:, None] + (NA_WIN_W - 1), 0, 2 * NA_WIN_W - 2)
        sel_r = jnp.asarray(row_off[..., None] == np.arange(2 * NA_WIN_H - 1), F32)
        sel_c = jnp.asarray(col_off[..., None] == np.arange(2 * NA_WIN_W - 1), F32)
        b = jnp.einsum('aeu,huv,cdv->haced', sel_r, rpb, sel_c, precision=lax.Precision.HIGHEST)
        ok = row_ok[:, None, :, None] & col_ok[None, :, None, :]
        b = jnp.where(jnp.asarray(ok)[None], b * math.log2(math.e), NEG_BIG)
        tabs.append(b.reshape(rpb.shape[0], NA_ROWS_PER_BLOCK * GRID_W, NA_KEY_ROWS * GRID_W))
    return jnp.stack(tabs)


def _na_kernel(q_ref, k_ref, v_ref, bias_ref, o_ref, *, seq, ctx_len):
    tq = q_ref.shape[1]
    rows = seq // GRID_W
    n_lat_tiles = seq // tq
    nkeys = NA_KEY_ROWS * GRID_W
    t = pl.program_id(1)
    lane = lax.broadcasted_iota(jnp.int32, (tq, LANES), 1)
    kstart = jnp.clip(t * NA_ROWS_PER_BLOCK - NA_WIN_H // 2, 0, rows - NA_KEY_ROWS)
    koff = pl.multiple_of(kstart * GRID_W, GRID_W)

    def run(with_window):
        for pair in range(NA_HEADS // 2):
            sl = slice(pair * LANES, (pair + 1) * LANES)
            q2 = q_ref[0, :, sl].astype(F32)
            kc = k_ref[0, seq:seq + ctx_len, sl]
            if with_window:
                kw = k_ref[0, pl.ds(koff, nkeys), sl]
            outs = []
            for sub in range(2):
                hl = slice((2 * pair + sub) * LANES, (2 * pair + sub + 1) * LANES)
                keep = (lane < NA_HEAD_DIM) if sub == 0 else (lane >= NA_HEAD_DIM)
                q = jnp.where(keep, q2, 0.0).astype(BF16)
                s_c = _dot_nt(q, kc)
                m = jnp.max(s_c, axis=-1, keepdims=True)
                if with_window:
                    s_w = _dot_nt(q, kw) + bias_ref[0, 2 * pair + sub]
                    m = jnp.maximum(m, jnp.max(s_w, axis=-1, keepdims=True))
                acc = _dot(jnp.exp2((s_c - m).astype(BF16)), v_ref[0, seq:seq + ctx_len, hl])
                if with_window:
                    acc = acc + _dot(jnp.exp2((s_w - m).astype(BF16)), v_ref[0, pl.ds(koff, nkeys), hl])
                outs.append(acc / acc[:, NA_HEAD_DIM:NA_HEAD_DIM + 1])
            second = pltpu.roll(outs[1], NA_HEAD_DIM, 1)
            o_ref[0, :, sl] = jnp.where(lane < NA_HEAD_DIM, outs[0], second).astype(BF16)

    @pl.when(t < n_lat_tiles)
    def _():
        run(True)

    @pl.when(t >= n_lat_tiles)
    def _():
        run(False)


def _na_attention(q, k, v, bias, seq, ctx_len):
    nb, lt, w = q.shape
    tq = NA_ROWS_PER_BLOCK * GRID_W
    n_lat = seq // tq
    kern = functools.partial(_na_kernel, seq=seq, ctx_len=ctx_len)

    def variant(b, t):
        return (jnp.where(t == 0, 0, jnp.where(t >= n_lat - 1, 2, 1)), 0, 0, 0)

    return pl.pallas_call(
        kern,
        grid=(nb, lt // tq),
        in_specs=[pl.BlockSpec((1, tq, w), lambda b, t: (b, t, 0)),
                  pl.BlockSpec((1, lt, w), lambda b, t: (b, 0, 0), pipeline_mode=pl.Buffered(1)),
                  pl.BlockSpec((1, lt, v.shape[2]), lambda b, t: (b, 0, 0), pipeline_mode=pl.Buffered(1)),
                  pl.BlockSpec((1,) + bias.shape[1:], variant)],
        out_specs=pl.BlockSpec((1, tq, w), lambda b, t: (b, t, 0)),
        out_shape=jax.ShapeDtypeStruct((nb, lt, w), BF16),
        compiler_params=_cparams("parallel", "parallel"),
        name="na_attention",
    )(q, k, v, bias)


def _out_proj_kernel(x_ref, a_ref, b_ref, wa_ref, wb_ref, gate_ref, o_ref):
    y = _dot(a_ref[0], wa_ref[...]) + _dot(b_ref[0], wb_ref[...])
    o_ref[0] = x_ref[0] + gate_ref[0] * y


def _out_proj(xu, a, b, w_out, gate, seq):
    nb, lt, d = xu.shape
    tm = 256
    ka = a.shape[2]
    wa = w_out[:ka].astype(BF16)
    wb = w_out[ka:].astype(BF16)
    row = lambda bb, t: (bb, t, 0)
    full = lambda arr: pl.BlockSpec(arr.shape, lambda bb, t: (0,) * arr.ndim)
    return pl.pallas_call(
        _out_proj_kernel,
        grid=(nb, lt // tm),
        in_specs=[pl.BlockSpec((1, tm, d), row), pl.BlockSpec((1, tm, ka), row),
                  pl.BlockSpec((1, tm, b.shape[2]), row), full(wa), full(wb), _mod_spec(seq // tm, nb, d)],
        out_specs=pl.BlockSpec((1, tm, d), row),
        out_shape=jax.ShapeDtypeStruct(xu.shape, F32),
        compiler_params=_cparams("parallel", "parallel"),
        name="out_proj",
    )(xu, a, b, wa, wb, gate)


def _ffn_kernel(x_ref, g_ref, sc_ref, sh_ref, gate_ref, wg_ref, wu_ref, wd_ref, o_ref, h_scr, acc_scr):
    f = pl.program_id(2)

    @pl.when(f == 0)
    def _():
        h_scr[...] = _norm_mod(x_ref[0], g_ref[...], sc_ref[0], sh_ref[0]).astype(BF16)
        acc_scr[...] = jnp.zeros_like(acc_scr)

    h = h_scr[...]
    a = _dot(h, wg_ref[...])
    u = _dot(h, wu_ref[...])
    acc_scr[...] += _dot((a * jax.nn.sigmoid(a) * u).astype(BF16), wd_ref[...])

    @pl.when(f == pl.num_programs(2) - 1)
    def _():
        o_ref[0] = x_ref[0] + gate_ref[0] * acc_scr[...]


def _ffn(xu, norm_g, scale, shift, gate, w_gate, w_up, w_down, seq):
    nb, lt, d = xu.shape
    dff = w_gate.shape[1]
    tm = 256
    tf = dff
    row = lambda b, t, f: (b, t, 0)
    n_lat = seq // tm
    mod = pl.BlockSpec((1, 1, d), lambda b, t, f: (jnp.where(t < n_lat, b, nb), 0, 0))
    g2 = norm_g.reshape(1, d)
    once = pl.Buffered(1)
    return pl.pallas_call(
        _ffn_kernel,
        grid=(nb, lt // tm, dff // tf),
        in_specs=[pl.BlockSpec((1, tm, d), row), pl.BlockSpec((1, d), lambda b, t, f: (0, 0)), mod, mod, mod,
                  pl.BlockSpec((d, tf), lambda b, t, f: (0, f), pipeline_mode=once),
                  pl.BlockSpec((d, tf), lambda b, t, f: (0, f), pipeline_mode=once),
                  pl.BlockSpec((tf, d), lambda b, t, f: (f, 0), pipeline_mode=once)],
        out_specs=pl.BlockSpec((1, tm, d), row),
        out_shape=jax.ShapeDtypeStruct(xu.shape, F32),
        scratch_shapes=[pltpu.VMEM((tm, d), BF16), pltpu.VMEM((tm, d), F32)],
        compiler_params=_cparams("parallel", "parallel", "arbitrary"),
        name="dense_swiglu",
    )(xu, g2, scale, shift, gate, w_gate.astype(BF16), w_up.astype(BF16), w_down.astype(BF16))


def _s5_in_kernel(x_ref, g_ref, sc_ref, sh_ref, w_ref, o_ref):
    h = _norm_mod(x_ref[0], g_ref[...], sc_ref[0], sh_ref[0]).astype(BF16)
    u = _dot(h, w_ref[...])
    for k in range(o_ref.shape[0]):
        o_ref[k, 0] = u[:, k * LANES:(k + 1) * LANES]


def _s5_in(xu, norm_g, scale, shift, w_in, seq):
    nb, lt, d = xu.shape
    tm = 256
    w = w_in.astype(BF16)
    row = lambda b, t: (b, t, 0)
    mod = _mod_spec(seq // tm, nb, d)
    g2 = norm_g.reshape(1, d)
    return pl.pallas_call(
        _s5_in_kernel,
        grid=(nb, lt // tm),
        in_specs=[pl.BlockSpec((1, tm, d), row), pl.BlockSpec((1, d), lambda b, t: (0, 0)), mod, mod,
                  pl.BlockSpec(w.shape, lambda b, t: (0, 0))],
        out_specs=pl.BlockSpec((w.shape[1] // LANES, 1, tm, LANES), lambda b, t: (0, b, t, 0)),
        out_shape=jax.ShapeDtypeStruct((w.shape[1] // LANES, nb, lt, LANES), F32),
        compiler_params=_cparams("parallel", "parallel"),
        name="s5_in",
    )(xu, g2, scale, shift, w)


def _s5_operators(a_re, a_im, log_step, b_re, b_im, c_re, c_im):
    hp = lax.Precision.HIGHEST
    t_len = S5_CHUNK
    ops = []
    for direction in range(2):
        dt = jnp.exp(log_step[direction])[:, None]
        lre, lim = a_re[direction] * dt, a_im[direction] * dt
        decay = jnp.exp(lre)
        ab_re, ab_im = decay * jnp.cos(lim), decay * jnp.sin(lim)
        den = a_re[direction] ** 2 + a_im[direction] ** 2
        f_re = ((ab_re - 1) * a_re[direction] + ab_im * a_im[direction]) / den
        f_im = (ab_im * a_re[direction] - (ab_re - 1) * a_im[direction]) / den
        bb_re = f_re[..., None] * b_re[direction] - f_im[..., None] * b_im[direction]
        bb_im = f_re[..., None] * b_im[direction] + f_im[..., None] * b_re[direction]
        cr, ci = c_re[direction], c_im[direction]
        tau = jnp.arange(t_len + 1, dtype=F32)[:, None, None]
        pw = jnp.exp(tau * lre[None])
        pw_re, pw_im = pw * jnp.cos(tau * lim[None]), pw * jnp.sin(tau * lim[None])
        ab_b_re = pw_re[..., None] * bb_re[None] - pw_im[..., None] * bb_im[None]
        ab_b_im = pw_re[..., None] * bb_im[None] + pw_im[..., None] * bb_re[None]
        kk = (jnp.einsum('gip,tgpj->tgij', cr, ab_b_re, precision=hp)
              - jnp.einsum('gip,tgpj->tgij', ci, ab_b_im, precision=hp))
        ca_re = cr[None] * pw_re[:, :, None, :] - ci[None] * pw_im[:, :, None, :]
        ca_im = cr[None] * pw_im[:, :, None, :] + ci[None] * pw_re[:, :, None, :]
        s = np.arange(t_len)
        z_pow = (t_len - 1 - s) if direction == 0 else s
        c_pow = (s + 1) if direction == 0 else (t_len - s)
        ops.append((kk, [ab_b_re[z_pow], ab_b_im[z_pow]],
                    [ca_re[c_pow], -ca_im[c_pow]],
                    [pw_re[t_len], pw_im[t_len]]))
    lg = S5_LANE_GROUPS
    nblk = a_re.shape[1] // lg
    eye = jnp.eye(lg, dtype=BF16)
    n_state = 4 * lg * S5_STATE
    kf, kb = ops[0][0], ops[1][0]
    kd = jnp.concatenate([kb[t_len - 1:0:-1], (kf[0] + kb[0])[None], kf[1:t_len]])
    kd = kd.astype(BF16).reshape(2 * t_len - 1, nblk, lg, S5_GROUP, S5_GROUP).transpose(1, 0, 4, 2, 3)
    d_blk = kd[:, :, None] * eye[:, None, :, None]
    d_cat = d_blk.reshape(nblk, 2 * t_len - 1, LANES, LANES).transpose(0, 2, 1, 3).reshape(nblk, LANES, -1)
    wz4 = jnp.stack(ops[0][1] + ops[1][1]).astype(BF16)
    wz4 = wz4.reshape(4, t_len, nblk, lg, S5_STATE, S5_GROUP).transpose(2, 1, 5, 0, 3, 4)
    wc4 = jnp.stack(ops[0][2] + ops[1][2]).astype(BF16)
    wc4 = wc4.reshape(4, t_len, nblk, lg, S5_GROUP, S5_STATE).transpose(2, 1, 4, 0, 3, 5)
    compact = (nblk, t_len * S5_GROUP, n_state)
    group_of_lane = (np.arange(n_state) // S5_STATE) % lg
    mask = np.broadcast_to((group_of_lane[None, :] == np.arange(lg)[:, None])[:, None, :], (lg, S5_GROUP, n_state))
    a_t = jnp.stack(ops[0][3] + ops[1][3]).reshape(4, nblk, lg * S5_STATE).transpose(1, 0, 2)
    return d_cat, wz4.reshape(compact), wc4.reshape(compact), jnp.asarray(mask, BF16), a_t


def _s5_core_kernel(u_ref, dcat_ref, wzc_ref, wcc_ref, mask_ref, at_ref, y_ref,
                    wi_ref, wz_ref, wct_ref, z_scr, h_scr, *, n_ctx_chunks):
    n_chunks = z_scr.shape[0]
    n_lat = n_chunks - n_ctx_chunks
    w = z_scr.shape[1] // 4
    width = wi_ref.shape[1]

    @pl.when(pl.program_id(1) == 0)
    def _():
        for s in range(S5_CHUNK):
            lag0 = (S5_CHUNK - 1 - s) * LANES
            wi_ref[s * LANES:(s + 1) * LANES, :] = dcat_ref[0, :, lag0:lag0 + width]
            rows = slice(s * S5_GROUP, (s + 1) * S5_GROUP)
            for a in range(S5_LANE_GROUPS):
                dst = slice(s * LANES + a * S5_GROUP, s * LANES + (a + 1) * S5_GROUP)
                wz_ref[dst, :] = wzc_ref[0, rows, :] * mask_ref[a]
                wct_ref[dst, :] = wcc_ref[0, rows, :] * mask_ref[a]

    a = jnp.concatenate([u_ref[0, 0, pl.ds(s, n_chunks, stride=S5_CHUNK), :].astype(BF16)
                         for s in range(S5_CHUNK)], axis=1)
    z_scr[...] = _dot(a, wz_ref[...])
    ar_f, ai_f, ar_b, ai_b = [at_ref[0, k:k + 1, :] for k in range(4)]

    def step(k, carry):
        (fr, fi), (br, bi) = carry
        cf = jnp.where(k < n_ctx_chunks, n_lat + k, k - n_ctx_chunks)
        cb = n_chunks - 1 - k
        rf, rb = pl.ds(cf, 1), pl.ds(cb, 1)
        h_scr[rf, 0:w] = fr
        h_scr[rf, w:2 * w] = fi
        h_scr[rb, 2 * w:3 * w] = br
        h_scr[rb, 3 * w:4 * w] = bi
        fwd = (ar_f * fr - ai_f * fi + z_scr[rf, 0:w], ar_f * fi + ai_f * fr + z_scr[rf, w:2 * w])
        bwd = (ar_b * br - ai_b * bi + z_scr[rb, 2 * w:3 * w], ar_b * bi + ai_b * br + z_scr[rb, 3 * w:4 * w])
        return fwd, bwd

    zero = jnp.zeros((1, w), F32)
    lax.fori_loop(0, n_chunks, step, ((zero, zero), (zero, zero)))
    y = _dot(a, wi_ref[...]) + _dot_nt(h_scr[...].astype(BF16), wct_ref[...])
    for t in range(S5_CHUNK):
        y_ref[0, 0, pl.ds(t, n_chunks, stride=S5_CHUNK), :] = y[:, t * LANES:(t + 1) * LANES]


def _s5_core(u_blocks, ops, n_ctx_chunks):
    nblk, nb, lt, lanes = u_blocks.shape
    n_chunks = lt // S5_CHUNK
    d_cat, wz_c, wc_c, mask, a_t = ops
    width = S5_CHUNK * lanes
    n_state = wz_c.shape[2]
    kern = functools.partial(_s5_core_kernel, n_ctx_chunks=n_ctx_chunks)
    per_block = lambda a: pl.BlockSpec((1,) + a.shape[1:], lambda g, b: (g, 0, 0))
    tok = pl.BlockSpec((1, 1, lt, lanes), lambda g, b: (g, b, 0, 0))
    return pl.pallas_call(
        kern,
        grid=(nblk, nb),
        in_specs=[tok, per_block(d_cat), per_block(wz_c), per_block(wc_c),
                  pl.BlockSpec(mask.shape, lambda g, b: (0, 0, 0)), per_block(a_t)],
        out_specs=tok,
        out_shape=jax.ShapeDtypeStruct(u_blocks.shape, F32),
        scratch_shapes=[pltpu.VMEM((width, width), BF16), pltpu.VMEM((width, n_state), BF16),
                        pltpu.VMEM((width, n_state), BF16),
                        pltpu.VMEM((n_chunks, n_state), F32), pltpu.VMEM((n_chunks, n_state), F32)],
        compiler_params=_cparams("arbitrary", "arbitrary"),
        name="s5_core",
    )(u_blocks, d_cat, wz_c, wc_c, mask, a_t)


def _s5_out_kernel(x_ref, u_ref, y_ref, d_ref, w_ref, gate_ref, o_ref):
    d = x_ref.shape[2]
    u = jnp.concatenate([u_ref[k, 0] for k in range(u_ref.shape[0])], axis=1)
    y = jnp.concatenate([y_ref[k, 0] for k in range(y_ref.shape[0])], axis=1)
    y = u * d_ref[...] + y
    z = _dot(jax.nn.gelu(y).astype(BF16), w_ref[...])
    o_ref[0] = x_ref[0] + gate_ref[0] * (z[:, :d] * jax.nn.sigmoid(z[:, d:]))


def _s5_out(x, u_blocks, y_blocks, d_skip, w_glu, gate):
    nb, seq, d = x.shape
    tm = 256
    nblk, _, _, lanes = u_blocks.shape
    w = nblk * lanes
    row = lambda b, t: (b, t, 0)
    blk = pl.BlockSpec((nblk, 1, tm, lanes), lambda b, t: (0, b, t, 0))
    wg = w_glu.astype(BF16)
    return pl.pallas_call(
        _s5_out_kernel,
        grid=(nb, seq // tm),
        in_specs=[pl.BlockSpec((1, tm, d), row), blk, blk,
                  pl.BlockSpec((1, w), lambda b, t: (0, 0)), pl.BlockSpec(wg.shape, lambda b, t: (0, 0)),
                  pl.BlockSpec((1, 1, d), lambda b, t: (b, 0, 0))],
        out_specs=pl.BlockSpec((1, tm, d), row),
        out_shape=jax.ShapeDtypeStruct(x.shape, F32),
        compiler_params=_cparams("parallel", "parallel"),
        name="s5_out",
    )(x, u_blocks, y_blocks, d_skip.reshape(1, w), wg, gate)


R_IDX, R_WT, R_RANK = 0, TOP_K, 2 * TOP_K


def _router_kernel(x_ref, g_ref, sc_ref, sh_ref, w_ref, tri_ref, rec_ref, cnt_ref, base_scr):
    @pl.when((pl.program_id(0) == 0) & (pl.program_id(1) == 0))
    def _():
        base_scr[...] = jnp.zeros_like(base_scr)

    t = _norm_mod(x_ref[0], g_ref[...], sc_ref[0], sh_ref[0]).astype(BF16)
    logits = _dot(t, w_ref[...])
    lane = lax.broadcasted_iota(jnp.int32, logits.shape, 1)
    big = jnp.int32(LANES)
    l1 = jnp.where(lane < N_EXPERTS, logits, -jnp.inf)
    v1 = jnp.max(l1, axis=-1, keepdims=True)
    i1 = jnp.min(jnp.where(l1 == v1, lane, big), axis=-1, keepdims=True)
    l2 = jnp.where(lane == i1, -jnp.inf, l1)
    v2 = jnp.max(l2, axis=-1, keepdims=True)
    i2 = jnp.min(jnp.where(l2 == v2, lane, big), axis=-1, keepdims=True)
    e2 = jnp.exp(v2 - v1)
    w1 = 1.0 / (1.0 + e2)
    w2 = e2 / (1.0 + e2)
    hit = jnp.where((lane == i1) | (lane == i2), 1.0, 0.0)
    before = _dot(tri_ref[...], hit.astype(BF16)) + base_scr[0:1, :]
    r1 = jnp.sum(jnp.where(lane == i1, before, 0.0), axis=-1, keepdims=True)
    r2 = jnp.sum(jnp.where(lane == i2, before, 0.0), axis=-1, keepdims=True)
    base_scr[...] = base_scr[...] + jnp.sum(hit, axis=0, keepdims=True)
    cnt_ref[...] = base_scr[...]
    rec = jnp.zeros(logits.shape, F32)
    for k, val in enumerate((i1.astype(F32), i2.astype(F32), w1, w2, r1, r2)):
        rec = jnp.where(lane == k, val, rec)
    rec_ref[0] = rec


def _router(x, norm_g, scale, shift, w_router):
    nb, seq, d = x.shape
    tm = 256
    wr = jnp.zeros((d, LANES), F32).at[:, :N_EXPERTS].set(w_router).astype(BF16)
    tri = jnp.asarray(np.tril(np.ones((tm, tm), np.float32), -1), BF16)
    row = lambda b, t: (b, t, 0)
    mod = pl.BlockSpec((1, 1, d), lambda b, t: (b, 0, 0))
    g2 = norm_g.reshape(1, d)
    return pl.pallas_call(
        _router_kernel,
        grid=(nb, seq // tm),
        in_specs=[pl.BlockSpec((1, tm, d), row), pl.BlockSpec((1, d), lambda b, t: (0, 0)), mod, mod,
                  pl.BlockSpec(wr.shape, lambda b, t: (0, 0)), pl.BlockSpec(tri.shape, lambda b, t: (0, 0))],
        out_specs=[pl.BlockSpec((1, tm, LANES), row), pl.BlockSpec((8, LANES), lambda b, t: (0, 0))],
        out_shape=[jax.ShapeDtypeStruct((nb, seq, LANES), F32), jax.ShapeDtypeStruct((8, LANES), F32)],
        scratch_shapes=[pltpu.VMEM((8, LANES), F32)],
        compiler_params=_cparams("arbitrary", "arbitrary"),
        name="router",
    )(x, g2, scale, shift, wr, tri)


def _row_copies_wait(src_ref, dst_ref, sem):
    pltpu.make_async_copy(src_ref, dst_ref, sem).wait()


def _dispatch_kernel(dest_ref, fill_ref, x_ref, g_ref, sc_ref, sh_ref, xs_ref, tbuf, zbuf, sem, zsem):
    tm = x_ref.shape[1]
    ztm = zbuf.shape[0]
    step = pl.program_id(0) * pl.num_programs(1) + pl.program_id(1)
    n_steps = pl.num_programs(0) * pl.num_programs(1)
    slot = step % 2

    def drain(s):
        for _ in range(TOP_K):
            _row_copies_wait(tbuf.at[s], xs_ref.at[pl.ds(0, tm)], sem.at[s])

    @pl.when(step == 0)
    def _():
        zbuf[...] = jnp.zeros_like(zbuf)

        def fill(start):
            def body(i, carry):
                @pl.when(fill_ref[i] > 0)
                def _():
                    cp = pltpu.make_async_copy(zbuf, xs_ref.at[pl.ds(pl.multiple_of(i * ztm, ztm), ztm)], zsem)
                    if start:
                        cp.start()
                    else:
                        cp.wait()
                return carry
            lax.fori_loop(0, fill_ref.shape[0], body, 0)

        fill(True)
        fill(False)

    @pl.when(step >= 2)
    def _():
        drain(slot)

    tbuf[slot] = _norm_mod(x_ref[0], g_ref[...], sc_ref[0], sh_ref[0])
    base = step * (tm * TOP_K)

    def body(r, carry):
        for k in range(TOP_K):
            dst = dest_ref[base + r * TOP_K + k]
            pltpu.make_async_copy(tbuf.at[slot, pl.ds(r, 1)], xs_ref.at[pl.ds(dst, 1)],
                                  sem.at[slot]).start(priority=k % 2)
        return carry

    lax.fori_loop(0, tm, body, 0, unroll=8)

    @pl.when(step == n_steps - 1)
    def _():
        drain(slot)

        @pl.when(n_steps > 1)
        def _():
            drain(1 - slot)


def _dispatch(x, norm_g, scale, shift, dest, tile_fill, n_rows, expert_tm):
    nb, seq, d = x.shape
    tm = 256
    row = lambda b, t, dr, fr: (b, t, 0)
    mod = pl.BlockSpec((1, 1, d), lambda b, t, dr, fr: (b, 0, 0))
    g2 = norm_g.reshape(1, d)
    grid_spec = pltpu.PrefetchScalarGridSpec(
        num_scalar_prefetch=2,
        grid=(nb, seq // tm),
        in_specs=[pl.BlockSpec((1, tm, d), row), pl.BlockSpec((1, d), lambda b, t, dr, fr: (0, 0)), mod, mod],
        out_specs=pl.BlockSpec(memory_space=pl.ANY),
        scratch_shapes=[pltpu.VMEM((2, tm, d), F32), pltpu.VMEM((expert_tm, d), F32),
                        pltpu.SemaphoreType.DMA((2,)), pltpu.SemaphoreType.DMA(())],
    )
    return pl.pallas_call(
        _dispatch_kernel,
        grid_spec=grid_spec,
        out_shape=jax.ShapeDtypeStruct((n_rows, d), F32),
        compiler_params=_cparams("arbitrary", "arbitrary"),
        name="dispatch",
    )(dest, tile_fill, x, g2, scale, shift)


def _expert_kernel(te_ref, tr_ref, x_ref, wg_ref, wu_ref, wd_ref, o_ref, xb_scr, acc_scr):
    i = pl.program_id(0)
    f = pl.program_id(1)
    nf = pl.num_programs(1)
    live = tr_ref[i] > 0

    @pl.when(jnp.logical_not(live) & (f == nf - 1))
    def _():
        o_ref[...] = jnp.zeros_like(o_ref)

    @pl.when(live)
    def _():
        @pl.when(f == 0)
        def _():
            xb_scr[...] = x_ref[...].astype(BF16)
            acc_scr[...] = jnp.zeros_like(acc_scr)

        x = xb_scr[...]
        a = _dot(x, wg_ref[0])
        u = _dot(x, wu_ref[0])
        acc_scr[...] += _dot((a * jax.nn.sigmoid(a) * u).astype(BF16), wd_ref[0])

        @pl.when(f == nf - 1)
        def _():
            o_ref[...] = acc_scr[...]


def _experts(xs, tile_expert, tile_rows, w_gate, w_up, w_down, tm, tf):
    n_rows, d = xs.shape
    dff = w_gate.shape[2]
    nf = dff // tf

    def fsel(i, f, te, tr):
        return jnp.where(tr[i] > 0, f, nf - 1)

    grid_spec = pltpu.PrefetchScalarGridSpec(
        num_scalar_prefetch=2,
        grid=(n_rows // tm, nf),
        in_specs=[pl.BlockSpec((tm, d), lambda i, f, te, tr: (i, 0)),
                  pl.BlockSpec((1, d, tf), lambda i, f, te, tr: (te[i], 0, fsel(i, f, te, tr))),
                  pl.BlockSpec((1, d, tf), lambda i, f, te, tr: (te[i], 0, fsel(i, f, te, tr))),
                  pl.BlockSpec((1, tf, d), lambda i, f, te, tr: (te[i], fsel(i, f, te, tr), 0))],
        out_specs=pl.BlockSpec((tm, d), lambda i, f, te, tr: (i, 0)),
        scratch_shapes=[pltpu.VMEM((tm, d), BF16), pltpu.VMEM((tm, d), F32)],
    )
    return pl.pallas_call(
        _expert_kernel,
        grid_spec=grid_spec,
        out_shape=jax.ShapeDtypeStruct((n_rows, d), F32),
        compiler_params=_cparams("arbitrary", "arbitrary"),
        name="experts",
    )(tile_expert, tile_rows, xs, w_gate, w_up, w_down)


def _combine_kernel(dest_ref, x_ref, rec_ref, gate_ref, g_ref, ys_ref, o_ref, ybuf, sem):
    tm = x_ref.shape[1]
    step = pl.program_id(0) * pl.num_programs(1) + pl.program_id(1)
    n_steps = pl.num_programs(0) * pl.num_programs(1)
    slot = step % 2

    def issue(st, sl):
        base = st * (tm * TOP_K)

        def body(r, carry):
            for k in range(TOP_K):
                src = dest_ref[base + r * TOP_K + k]
                pltpu.make_async_copy(ys_ref.at[pl.ds(src, 1)], ybuf.at[sl, k, pl.ds(r, 1)],
                                      sem.at[sl]).start(priority=k % 2)
            return carry

        lax.fori_loop(0, tm, body, 0, unroll=8)

    @pl.when(step == 0)
    def _():
        issue(step, slot)

    @pl.when(step + 1 < n_steps)
    def _():
        issue(step + 1, 1 - slot)

    for k in range(TOP_K):
        _row_copies_wait(ys_ref.at[pl.ds(0, tm)], ybuf.at[slot, k], sem.at[slot])

    rec = rec_ref[0]
    y = rec[:, R_WT:R_WT + 1] * ybuf[slot, 0] + rec[:, R_WT + 1:R_WT + 2] * ybuf[slot, 1]
    x = x_ref[0] + gate_ref[0] * y
    o_ref[0] = _rms(x) * g_ref[...]


def _combine(x, rec, gate, final_g, ys, dest):
    nb, seq, d = x.shape
    tm = 256
    row = lambda b, t, dr: (b, t, 0)
    grid_spec = pltpu.PrefetchScalarGridSpec(
        num_scalar_prefetch=1,
        grid=(nb, seq // tm),
        in_specs=[pl.BlockSpec((1, tm, d), row), pl.BlockSpec((1, tm, LANES), row),
                  pl.BlockSpec((1, 1, d), lambda b, t, dr: (b, 0, 0)), pl.BlockSpec((1, d), lambda b, t, dr: (0, 0)),
                  pl.BlockSpec(memory_space=pl.ANY)],
        out_specs=pl.BlockSpec((1, tm, d), row),
        scratch_shapes=[pltpu.VMEM((2, TOP_K, tm, d), F32), pltpu.SemaphoreType.DMA((2,))],
    )
    return pl.pallas_call(
        _combine_kernel,
        grid_spec=grid_spec,
        out_shape=jax.ShapeDtypeStruct(x.shape, F32),
        compiler_params=_cparams("arbitrary", "arbitrary"),
        name="combine_final_norm",
    )(dest, x, rec, gate, final_g.reshape(1, d), ys)


def _moe(x, norm_g, scale, shift, gate, w_router, w_gate, w_up, w_down, final_g):
    nb, seq, d = x.shape
    n_tok = nb * seq
    tm, tf = 512, w_gate.shape[2] // 2
    rec, cnt = _router(x, norm_g, scale, shift, w_router)
    counts = cnt[0, :N_EXPERTS].astype(jnp.int32)
    padded = ((counts + tm - 1) // tm) * tm
    pend = jnp.cumsum(padded)
    pstart = pend - padded
    idx = rec[:, :, R_IDX:R_IDX + TOP_K].astype(jnp.int32)
    rank = rec[:, :, R_RANK:R_RANK + TOP_K].astype(jnp.int32)
    seg = jnp.sum(jnp.where(idx[..., None] == jnp.arange(N_EXPERTS), pstart, 0), axis=-1)
    dest = (seg + rank).reshape(n_tok * TOP_K)
    n_rows = n_tok * TOP_K + N_EXPERTS * tm
    tile_start = jnp.arange(n_rows // tm, dtype=jnp.int32) * tm
    tile_expert = jnp.sum((tile_start[:, None] >= pend[None, :]).astype(jnp.int32), axis=1)
    live = tile_expert < N_EXPERTS
    last_live = jnp.max(jnp.where(live, tile_expert, 0))
    tile_expert = jnp.where(live, tile_expert, last_live)
    seg_rows = jnp.sum(jnp.where(tile_expert[:, None] == jnp.arange(N_EXPERTS), (pstart + counts)[None, :], 0), axis=1)
    tile_rows = jnp.where(live, jnp.clip(seg_rows - tile_start, 0, tm), 0).astype(jnp.int32)
    tile_fill = (tile_rows < tm).astype(jnp.int32)

    xs = _dispatch(x, norm_g, scale, shift, dest, tile_fill, n_rows, tm)
    ys = _experts(xs, tile_expert.astype(jnp.int32), tile_rows,
                  w_gate.astype(BF16), w_up.astype(BF16), w_down.astype(BF16), tm, tf)
    return _combine(x, rec, gate, final_g, ys, dest)


def kernel(x, c, ctx, c_ctx, mod_w, mod_b, norm1_g, norm2_g, ev_w_in, ev_q_norm_g, ev_w_qb, ev_kv_norm_g, ev_w_kvb, ev_na_rpb, ev_w_out, ev_ffn_w_gate, ev_ffn_w_up, ev_ffn_w_down, od_w_in, od_a_re, od_a_im, od_log_step, od_b_re, od_b_im, od_c_re, od_c_im, od_d, od_w_glu, moe_w_router, moe_w_gate, moe_w_up, moe_w_down, final_g):
    nb, seq, d = x.shape
    ctx_len = ctx.shape[1]
    assert mod_w.shape[0] == 2 and nb < 8
    assert seq % (NA_ROWS_PER_BLOCK * GRID_W) == 0 and ctx_len == NA_ROWS_PER_BLOCK * GRID_W

    cond = jnp.zeros((8, d), F32).at[:nb].set(c).at[nb].set(c_ctx)
    mods = _adaln(cond, mod_w, mod_b)

    def mod_parts(layer):
        return [mods[layer, :, i * d:(i + 1) * d].reshape(8, 1, d) for i in range(N_MOD)]

    xu = jnp.concatenate([x, ctx], axis=1)

    sh1, sc1, g1, sh2, sc2, g2 = mod_parts(0)
    cos, sin = _rope_tables(seq, ctx_len)
    weights = _even_weights(ev_w_in[0], ev_w_qb[0], ev_w_kvb[0])
    q, k, v, nq, nk, nv = _even_project(xu, norm1_g[0], sc1, sh1, weights, ev_q_norm_g[0], ev_kv_norm_g[0],
                                        cos, sin, seq)
    mla = _mla_attention(q, k, v, seq, ctx_len)
    bias = _na_bias_tables(ev_na_rpb[0], seq // GRID_W)
    na = _na_attention(nq, nk, nv, bias, seq, ctx_len)
    xu = _out_proj(xu, mla, na, ev_w_out[0], g1, seq)
    xu = _ffn(xu, norm2_g[0], sc2, sh2, g2, ev_ffn_w_gate[0], ev_ffn_w_up[0], ev_ffn_w_down[0], seq)

    sh1, sc1, g1, sh2, sc2, g2 = mod_parts(1)
    u = _s5_in(xu, norm1_g[1], sc1, sh1, od_w_in[0], seq)
    ops = _s5_operators(od_a_re[0], od_a_im[0], od_log_step[0], od_b_re[0], od_b_im[0], od_c_re[0], od_c_im[0])
    y = _s5_core(u, ops, ctx_len // S5_CHUNK)
    xl = xu[:, :seq]
    xl = _s5_out(xl, u, y, od_d[0], od_w_glu[0], g1[:nb])
    return _moe(xl, norm2_g[1], sc2[:nb], sh2[:nb], g2[:nb], moe_w_router[0], moe_w_gate[0], moe_w_up[0],
                moe_w_down[0], final_g)
```

```python
import functools
import math

import numpy as np
import jax
import jax.numpy as jnp
from jax import lax
from jax.experimental import pallas as pl
from jax.experimental.pallas import tpu as pltpu

F32 = jnp.float32
BF16 = jnp.bfloat16

LANES = 128
VMEM_LIMIT_BYTES = 52 * 1024 * 1024

NORM_EPS = 1e-6
ROPE_BASE = 10000.0
GRID_W = 64
N_MOD = 6

MLA_HEADS = 8
MLA_NOPE = 64
MLA_ROPE = 32
MLA_V = 64
Q_LORA = 384
KV_LORA = 256
MLA_SCALE = (MLA_NOPE + MLA_ROPE) ** -0.5
MLA_EXP2_SCALE = MLA_SCALE * math.log2(math.e)

NA_HEADS = 8
NA_HEAD_DIM = 64
NA_WIN_H = 8
NA_WIN_W = 16
NA_SCALE = NA_HEAD_DIM ** -0.5
NA_EXP2_SCALE = NA_SCALE * math.log2(math.e)
NA_ROWS_PER_BLOCK = 4
NA_KEY_ROWS = NA_ROWS_PER_BLOCK + NA_WIN_H - 1

S5_GROUP = 16
S5_STATE = 64
S5_CHUNK = 16
S5_LANE_GROUPS = LANES // S5_GROUP

N_EXPERTS = 8
TOP_K = 2

NEG_BIG = -1e30


def _cparams(*sem):
    return pltpu.CompilerParams(dimension_semantics=sem, vmem_limit_bytes=VMEM_LIMIT_BYTES)


def _rms(x):
    return x * lax.rsqrt(jnp.mean(x * x, axis=-1, keepdims=True) + NORM_EPS)


def _norm_mod(x, g, scale, shift):
    return (_rms(x) * g) * (1 + scale) + shift


def _dot(a, b):
    return jnp.dot(a, b, preferred_element_type=F32)


def _dot_nt(a, b):
    return lax.dot_general(a, b, (((1,), (1,)), ((), ())), preferred_element_type=F32)


def _adaln_kernel(c_ref, w_ref, b_ref, o_ref):
    c = c_ref[...]
    s = (c * jax.nn.sigmoid(c)).astype(BF16)
    o_ref[0] = _dot(s, w_ref[0].astype(BF16)) + b_ref[0]


def _adaln(cond, mod_w, mod_b):
    nl, d, n = mod_w.shape
    tn = 1536
    return pl.pallas_call(
        _adaln_kernel,
        grid=(nl, n // tn),
        in_specs=[pl.BlockSpec((8, d), lambda l, j: (0, 0)),
                  pl.BlockSpec((1, d, tn), lambda l, j: (l, 0, j)),
                  pl.BlockSpec((1, 1, tn), lambda l, j: (l, 0, j))],
        out_specs=pl.BlockSpec((1, 8, tn), lambda l, j: (l, 0, j)),
        out_shape=jax.ShapeDtypeStruct((nl, 8, n), F32),
        compiler_params=_cparams("parallel", "parallel"),
        name="adaln",
    )(cond, mod_w, mod_b.reshape(nl, 1, n))


def _tile_rows(x_ref, c_ref):
    return jnp.where(pl.program_id(1) < pl.num_programs(1) - 1, x_ref[0], c_ref[0])


def _tile_specs(tm, d, n_lat):
    return [pl.BlockSpec((1, tm, d), lambda b, t: (b, jnp.minimum(t, n_lat - 1), 0)),
            pl.BlockSpec((1, tm, d), lambda b, t: (b, 0, 0))]


def _even_proj_kernel(x_ref, c_ref, g_ref, sc_ref, sh_ref, win_ref, qg_ref, kvg_ref, wq_ref, wqr_ref, wkk_ref,
                      wkv_ref, cos_ref, sin_ref, q_ref, k_ref, v_ref, nq_ref, nk_ref, nv_ref):
    h = _norm_mod(_tile_rows(x_ref, c_ref), g_ref[...], sc_ref[0], sh_ref[0]).astype(BF16)
    p = _dot(h, win_ref[...])
    c0, c1, c2, c3 = Q_LORA, Q_LORA + KV_LORA, Q_LORA + KV_LORA + LANES, Q_LORA + KV_LORA + 2 * LANES
    cqn = (_rms(p[:, :c0]) * qg_ref[...]).astype(BF16)
    ckvn = (_rms(p[:, c0:c1]) * kvg_ref[...]).astype(BF16)
    cos = cos_ref[...]
    sin = sin_ref[...]
    kr = p[:, c1:c2] * cos + p[:, c2:c3] * sin
    qa = _dot(cqn, wq_ref[...])
    qb = _dot(cqn, wqr_ref[...])
    kk = _dot(ckvn, wkk_ref[...])
    for hd in range(MLA_HEADS):
        sl = slice(hd * LANES, (hd + 1) * LANES)
        q_ref[0, :, sl] = ((qa[:, sl] * cos + qb[:, sl] * sin) * MLA_EXP2_SCALE).astype(BF16)
        k_ref[0, :, sl] = (kk[:, sl] + kr).astype(BF16)
    vlane = lax.broadcasted_iota(jnp.int32, (1, MLA_HEADS * LANES), 1) & (LANES - 1)
    v_ref[0] = (_dot(ckvn, wkv_ref[...]) + jnp.where(vlane == MLA_V, 1.0, 0.0)).astype(BF16)
    w = NA_HEADS * NA_HEAD_DIM
    nq_ref[0] = (p[:, c3:c3 + w] * NA_EXP2_SCALE).astype(BF16)
    nk_ref[0] = p[:, c3 + w:c3 + 2 * w].astype(BF16)
    nv_ref[0] = (p[:, c3 + 2 * w:c3 + 2 * w + NA_HEADS * LANES]
                 + jnp.where(vlane == NA_HEAD_DIM, 1.0, 0.0)).astype(BF16)


def _rot_half_cols(w):
    q = MLA_ROPE // 4
    return jnp.concatenate([-w[:, q:2 * q], w[:, :q], -w[:, 3 * q:], w[:, 2 * q:3 * q]], axis=1)


def _rope_tables(seq, ctx_len):
    q = MLA_ROPE // 4
    t = np.arange(seq)
    inv_freq = np.float32(ROPE_BASE) ** (-np.arange(q, dtype=np.float32) / np.float32(q))
    ang_r = (t // GRID_W).astype(np.float32)[:, None] * inv_freq[None, :]
    ang_c = (t % GRID_W).astype(np.float32)[:, None] * inv_freq[None, :]
    cos = np.ones((seq + ctx_len, LANES), np.float32)
    sin = np.zeros((seq + ctx_len, LANES), np.float32)
    cos[:seq, MLA_NOPE:MLA_NOPE + MLA_ROPE] = np.concatenate([np.cos(ang_r)] * 2 + [np.cos(ang_c)] * 2, axis=1)
    sin[:seq, MLA_NOPE:MLA_NOPE + MLA_ROPE] = np.concatenate([np.sin(ang_r)] * 2 + [np.sin(ang_c)] * 2, axis=1)
    return jnp.asarray(cos), jnp.asarray(sin)


def _even_weights(w_in, w_qb, w_kvb):
    d = w_in.shape[0]
    c1 = Q_LORA + KV_LORA
    wkr = w_in[:, c1:c1 + MLA_ROPE]
    pad = lambda w: jnp.zeros((d, LANES), F32).at[:, MLA_NOPE:MLA_NOPE + MLA_ROPE].set(w)
    w_na = NA_HEADS * NA_HEAD_DIM
    na_qk = w_in[:, c1 + MLA_ROPE:c1 + MLA_ROPE + 2 * w_na]
    na_v = w_in[:, c1 + MLA_ROPE + 2 * w_na:].reshape(d, NA_HEADS, NA_HEAD_DIM)
    na_v = jnp.concatenate([na_v, jnp.zeros((d, NA_HEADS, LANES - NA_HEAD_DIM), F32)], axis=2)
    win = jnp.concatenate([w_in[:, :c1], pad(wkr), pad(_rot_half_cols(wkr)), na_qk,
                           na_v.reshape(d, NA_HEADS * LANES)], axis=1)
    hq = MLA_NOPE + MLA_ROPE
    wq = w_qb.reshape(Q_LORA, MLA_HEADS, hq)
    zq = jnp.zeros((Q_LORA, MLA_HEADS, LANES - hq), F32)
    wq_main = jnp.concatenate([wq, zq], axis=2).reshape(Q_LORA, MLA_HEADS * LANES)
    rot = jnp.stack([_rot_half_cols(wq[:, h, MLA_NOPE:]) for h in range(MLA_HEADS)], axis=1)
    wq_rot = jnp.concatenate([jnp.zeros((Q_LORA, MLA_HEADS, MLA_NOPE), F32), rot, zq], axis=2)
    wq_rot = wq_rot.reshape(Q_LORA, MLA_HEADS * LANES)
    wkv = w_kvb.reshape(KV_LORA, MLA_HEADS, MLA_NOPE + MLA_V)
    wkk = jnp.concatenate([wkv[:, :, :MLA_NOPE], jnp.zeros((KV_LORA, MLA_HEADS, LANES - MLA_NOPE), F32)], axis=2)
    wkk = wkk.reshape(KV_LORA, MLA_HEADS * LANES)
    wv = jnp.concatenate([wkv[:, :, MLA_NOPE:], jnp.zeros((KV_LORA, MLA_HEADS, LANES - MLA_V), F32)], axis=2)
    wv = wv.reshape(KV_LORA, MLA_HEADS * LANES)
    return tuple(a.astype(BF16) for a in (win, wq_main, wq_rot, wkk, wv))


def _mod_spec(n_lat_tiles, nb, d):
    return pl.BlockSpec((1, 1, d), lambda b, t: (jnp.where(t < n_lat_tiles, b, nb), 0, 0))


def _even_project(x, ctx, norm_g, scale, shift, weights, q_norm_g, kv_norm_g, cos, sin):
    nb, seq, d = x.shape
    tm = ctx.shape[1]
    lt = seq + tm
    win, wq, wqr, wkk, wv = weights
    n_lat = seq // tm
    row = lambda b, t: (b, t, 0)
    full = lambda a: pl.BlockSpec(a.shape, lambda b, t: (0,) * a.ndim)
    mod = _mod_spec(n_lat, nb, d)
    g2 = norm_g.reshape(1, d)
    qg2 = q_norm_g.reshape(1, Q_LORA)
    kvg2 = kv_norm_g.reshape(1, KV_LORA)
    wide = MLA_HEADS * LANES
    half = MLA_HEADS * MLA_V
    outs = [jax.ShapeDtypeStruct((nb, lt, wide), BF16), jax.ShapeDtypeStruct((nb, lt, wide), BF16)] + \
           [jax.ShapeDtypeStruct((nb, lt, wide), BF16)] + [jax.ShapeDtypeStruct((nb, lt, half), BF16)] * 2 + \
           [jax.ShapeDtypeStruct((nb, lt, wide), BF16)]
    return pl.pallas_call(
        _even_proj_kernel,
        grid=(nb, lt // tm),
        in_specs=_tile_specs(tm, d, n_lat) + [full(g2), mod, mod, full(win), full(qg2), full(kvg2),
                  full(wq), full(wqr), full(wkk), full(wv),
                  pl.BlockSpec((tm, LANES), lambda b, t: (t, 0)), pl.BlockSpec((tm, LANES), lambda b, t: (t, 0))],
        out_specs=[pl.BlockSpec((1, tm, wide), row), pl.BlockSpec((1, tm, wide), row)] +
                  [pl.BlockSpec((1, tm, wide), row)] + [pl.BlockSpec((1, tm, half), row)] * 2 +
                  [pl.BlockSpec((1, tm, wide), row)],
        out_shape=outs,
        compiler_params=_cparams("parallel", "parallel"),
        name="even_project",
    )(x, ctx, g2, scale, shift, win, qg2, kvg2, wq, wqr, wkk, wv, cos, sin)


MLA_KEY_CHUNK = 512


def _mla_kernel(q_ref, k_ref, v_ref, o_ref, s_a, s_b, p_a, p_b, *, seq, ctx_len):
    tq = q_ref.shape[1]
    n_lat_tiles = seq // tq
    t = pl.program_id(1)
    lane = lax.broadcasted_iota(jnp.int32, (tq, LANES), 1)
    stages = ((s_a, p_a), (s_b, p_b))
    head_lanes = lambda h: slice(h * LANES, (h + 1) * LANES)

    def run(key0, n_keys):
        def scores(h):
            s_scr = stages[h % 2][0]
            q = q_ref[0, :, head_lanes(h)]
            m = None
            for c0 in range(0, n_keys, MLA_KEY_CHUNK):
                w = min(MLA_KEY_CHUNK, n_keys - c0)
                s = _dot_nt(q, k_ref[0, key0 + c0:key0 + c0 + w, head_lanes(h)])
                s_scr[:, c0:c0 + w] = s
                for l0 in range(0, w, LANES):
                    m = s[:, l0:l0 + LANES] if m is None else jnp.maximum(m, s[:, l0:l0 + LANES])
            return jnp.max(m, axis=-1, keepdims=True)

        def attend(h, m):
            s_scr, p_scr = stages[h % 2]
            p_scr[:, 0:n_keys] = jnp.exp2((s_scr[:, 0:n_keys] - m).astype(BF16))
            acc = _dot(p_scr[:, 0:n_keys], v_ref[0, key0:key0 + n_keys, head_lanes(h)])
            return acc / acc[:, MLA_V:MLA_V + 1]

        m_next = scores(0)
        for h in range(MLA_HEADS):
            m_cur = m_next
            if h + 1 < MLA_HEADS:
                m_next = scores(h + 1)
            out = attend(h, m_cur)
            if h % 2 == 0:
                first = out
            else:
                pair = h // 2
                second = pltpu.roll(out, MLA_V, 1)
                o_ref[0, :, pair * LANES:(pair + 1) * LANES] = jnp.where(lane < MLA_V, first, second).astype(BF16)

    @pl.when(t < n_lat_tiles)
    def _():
        run(0, seq + ctx_len)

    @pl.when(t >= n_lat_tiles)
    def _():
        run(seq, ctx_len)


def _mla_attention(q, k, v, seq, ctx_len):
    nb, lt, wide = q.shape
    half = MLA_HEADS * MLA_V
    tq = 256
    kern = functools.partial(_mla_kernel, seq=seq, ctx_len=ctx_len)
    per_batch = pl.BlockSpec((1, lt, wide), lambda b, t: (b, 0, 0), pipeline_mode=pl.Buffered(1))
    return pl.pallas_call(
        kern,
        grid=(nb, lt // tq),
        in_specs=[pl.BlockSpec((1, tq, wide), lambda b, t: (b, t, 0)), per_batch, per_batch],
        out_specs=pl.BlockSpec((1, tq, half), lambda b, t: (b, t, 0)),
        out_shape=jax.ShapeDtypeStruct((nb, lt, half), BF16),
        scratch_shapes=[pltpu.VMEM((tq, lt), F32), pltpu.VMEM((tq, lt), F32),
                        pltpu.VMEM((tq, lt), BF16), pltpu.VMEM((tq, lt), BF16)],
        compiler_params=_cparams("parallel", "parallel"),
        name="mla_attention",
    )(q, k, v)


def _na_bias_tables(rpb, rows):
    kh = min(NA_WIN_H, rows)
    last_r0 = rows - NA_ROWS_PER_BLOCK
    tabs = []
    for r0 in (0, 2 * NA_ROWS_PER_BLOCK, last_r0):
        kstart = int(np.clip(r0 - kh // 2, 0, rows - NA_KEY_ROWS))
        r = r0 + np.arange(NA_ROWS_PER_BLOCK)
        rs = np.clip(r - kh // 2, 0, rows - kh)
        kr = kstart + np.arange(NA_KEY_ROWS)
        row_ok = (kr[None, :] >= rs[:, None]) & (kr[None, :] < rs[:, None] + kh)
        row_off = np.clip(kr[None, :] - r[:, None] + (NA_WIN_H - 1), 0, 2 * NA_WIN_H - 2)
        c = np.arange(GRID_W)
        cs = np.clip(c - NA_WIN_W // 2, 0, GRID_W - NA_WIN_W)
        col_ok = (c[None, :] >= cs[:, None]) & (c[None, :] < cs[:, None] + NA_WIN_W)
        col_off = np.clip(c[None, :] - c[:, None] + (NA_WIN_W - 1), 0, 2 * NA_WIN_W - 2)
        sel_r = jnp.asarray(row_off[..., None] == np.arange(2 * NA_WIN_H - 1), F32)
        sel_c = jnp.asarray(col_off[..., None] == np.arange(2 * NA_WIN_W - 1), F32)
        b = jnp.einsum('aeu,huv,cdv->haced', sel_r, rpb, sel_c, precision=lax.Precision.HIGHEST)
        ok = row_ok[:, None, :, None] & col_ok[None, :, None, :]
        b = jnp.where(jnp.asarray(ok)[None], b * math.log2(math.e), NEG_BIG)
        tabs.append(b.reshape(rpb.shape[0], NA_ROWS_PER_BLOCK * GRID_W, NA_KEY_ROWS * GRID_W))
    return jnp.stack(tabs)


def _na_kernel(q_ref, k_ref, v_ref, bias_ref, o_ref, *, seq, ctx_len):
    tq = q_ref.shape[1]
    rows = seq // GRID_W
    n_lat_tiles = seq // tq
    nkeys = NA_KEY_ROWS * GRID_W
    t = pl.program_id(1)
    lane = lax.broadcasted_iota(jnp.int32, (tq, LANES), 1)
    kstart = jnp.clip(t * NA_ROWS_PER_BLOCK - NA_WIN_H // 2, 0, rows - NA_KEY_ROWS)
    koff = pl.multiple_of(kstart * GRID_W, GRID_W)

    def run(with_window):
        for pair in range(NA_HEADS // 2):
            sl = slice(pair * LANES, (pair + 1) * LANES)
            q2 = q_ref[0, :, sl].astype(F32)
            kc = k_ref[0, seq:seq + ctx_len, sl]
            if with_window:
                kw = k_ref[0, pl.ds(koff, nkeys), sl]
            outs = []
            for sub in range(2):
                hl = slice((2 * pair + sub) * LANES, (2 * pair + sub + 1) * LANES)
                keep = (lane < NA_HEAD_DIM) if sub == 0 else (lane >= NA_HEAD_DIM)
                q = jnp.where(keep, q2, 0.0).astype(BF16)
                s_c = _dot_nt(q, kc)
                m = jnp.max(s_c, axis=-1, keepdims=True)
                if with_window:
                    s_w = _dot_nt(q, kw) + bias_ref[0, 2 * pair + sub]
                    m = jnp.maximum(m, jnp.max(s_w, axis=-1, keepdims=True))
                acc = _dot(jnp.exp2((s_c - m).astype(BF16)), v_ref[0, seq:seq + ctx_len, hl])
                if with_window:
                    acc = acc + _dot(jnp.exp2((s_w - m).astype(BF16)), v_ref[0, pl.ds(koff, nkeys), hl])
                outs.append(acc / acc[:, NA_HEAD_DIM:NA_HEAD_DIM + 1])
            second = pltpu.roll(outs[1], NA_HEAD_DIM, 1)
            o_ref[0, :, sl] = jnp.where(lane < NA_HEAD_DIM, outs[0], second).astype(BF16)

    @pl.when(t < n_lat_tiles)
    def _():
        run(True)

    @pl.when(t >= n_lat_tiles)
    def _():
        run(False)


def _na_attention(q, k, v, bias, seq, ctx_len):
    nb, lt, w = q.shape
    tq = NA_ROWS_PER_BLOCK * GRID_W
    n_lat = seq // tq
    kern = functools.partial(_na_kernel, seq=seq, ctx_len=ctx_len)

    def variant(b, t):
        return (jnp.where(t == 0, 0, jnp.where(t >= n_lat - 1, 2, 1)), 0, 0, 0)

    return pl.pallas_call(
        kern,
        grid=(nb, lt // tq),
        in_specs=[pl.BlockSpec((1, tq, w), lambda b, t: (b, t, 0)),
                  pl.BlockSpec((1, lt, w), lambda b, t: (b, 0, 0), pipeline_mode=pl.Buffered(1)),
                  pl.BlockSpec((1, lt, v.shape[2]), lambda b, t: (b, 0, 0), pipeline_mode=pl.Buffered(1)),
                  pl.BlockSpec((1,) + bias.shape[1:], variant)],
        out_specs=pl.BlockSpec((1, tq, w), lambda b, t: (b, t, 0)),
        out_shape=jax.ShapeDtypeStruct((nb, lt, w), BF16),
        compiler_params=_cparams("parallel", "parallel"),
        name="na_attention",
    )(q, k, v, bias)


def _out_proj_kernel(x_ref, c_ref, a_ref, b_ref, wa_ref, wb_ref, gate_ref, o_ref):
    y = _dot(a_ref[0], wa_ref[...]) + _dot(b_ref[0], wb_ref[...])
    o_ref[0] = _tile_rows(x_ref, c_ref) + gate_ref[0] * y


def _out_proj(x, ctx, a, b, w_out, gate):
    nb, seq, d = x.shape
    tm = ctx.shape[1]
    lt = seq + tm
    ka = a.shape[2]
    wa = w_out[:ka].astype(BF16)
    wb = w_out[ka:].astype(BF16)
    row = lambda bb, t: (bb, t, 0)
    full = lambda arr: pl.BlockSpec(arr.shape, lambda bb, t: (0,) * arr.ndim)
    return pl.pallas_call(
        _out_proj_kernel,
        grid=(nb, lt // tm),
        in_specs=_tile_specs(tm, d, seq // tm) + [pl.BlockSpec((1, tm, ka), row),
                  pl.BlockSpec((1, tm, b.shape[2]), row), full(wa), full(wb), _mod_spec(seq // tm, nb, d)],
        out_specs=pl.BlockSpec((1, tm, d), row),
        out_shape=jax.ShapeDtypeStruct((nb, lt, d), F32),
        compiler_params=_cparams("parallel", "parallel"),
        name="out_proj",
    )(x, ctx, a, b, wa, wb, gate)


def _ffn_kernel(x_ref, g_ref, sc_ref, sh_ref, gate_ref, wg_ref, wu_ref, wd_ref, o_ref, h_scr, acc_scr):
    f = pl.program_id(2)

    @pl.when(f == 0)
    def _():
        h_scr[...] = _norm_mod(x_ref[0], g_ref[...], sc_ref[0], sh_ref[0]).astype(BF16)
        acc_scr[...] = jnp.zeros_like(acc_scr)

    h = h_scr[...]
    a = _dot(h, wg_ref[...])
    u = _dot(h, wu_ref[...])
    acc_scr[...] += _dot((a * jax.nn.sigmoid(a) * u).astype(BF16), wd_ref[...])

    @pl.when(f == pl.num_programs(2) - 1)
    def _():
        o_ref[0] = x_ref[0] + gate_ref[0] * acc_scr[...]


def _ffn(xu, norm_g, scale, shift, gate, w_gate, w_up, w_down, seq):
    nb, lt, d = xu.shape
    dff = w_gate.shape[1]
    tm = 256
    tf = dff
    row = lambda b, t, f: (b, t, 0)
    n_lat = seq // tm
    mod = pl.BlockSpec((1, 1, d), lambda b, t, f: (jnp.where(t < n_lat, b, nb), 0, 0))
    g2 = norm_g.reshape(1, d)
    once = pl.Buffered(1)
    return pl.pallas_call(
        _ffn_kernel,
        grid=(nb, lt // tm, dff // tf),
        in_specs=[pl.BlockSpec((1, tm, d), row), pl.BlockSpec((1, d), lambda b, t, f: (0, 0)), mod, mod, mod,
                  pl.BlockSpec((d, tf), lambda b, t, f: (0, f), pipeline_mode=once),
                  pl.BlockSpec((d, tf), lambda b, t, f: (0, f), pipeline_mode=once),
                  pl.BlockSpec((tf, d), lambda b, t, f: (f, 0), pipeline_mode=once)],
        out_specs=pl.BlockSpec((1, tm, d), row),
        out_shape=jax.ShapeDtypeStruct(xu.shape, F32),
        scratch_shapes=[pltpu.VMEM((tm, d), BF16), pltpu.VMEM((tm, d), F32)],
        compiler_params=_cparams("parallel", "parallel", "arbitrary"),
        name="dense_swiglu",
    )(xu, g2, scale, shift, gate, w_gate.astype(BF16), w_up.astype(BF16), w_down.astype(BF16))


def _s5_in_kernel(x_ref, g_ref, sc_ref, sh_ref, w_ref, o_ref):
    h = _norm_mod(x_ref[0], g_ref[...], sc_ref[0], sh_ref[0]).astype(BF16)
    u = _dot(h, w_ref[...])
    for k in range(o_ref.shape[0]):
        o_ref[k, 0] = u[:, k * LANES:(k + 1) * LANES]


def _s5_in(xu, norm_g, scale, shift, w_in, seq):
    nb, lt, d = xu.shape
    tm = 256
    w = w_in.astype(BF16)
    row = lambda b, t: (b, t, 0)
    mod = _mod_spec(seq // tm, nb, d)
    g2 = norm_g.reshape(1, d)
    return pl.pallas_call(
        _s5_in_kernel,
        grid=(nb, lt // tm),
        in_specs=[pl.BlockSpec((1, tm, d), row), pl.BlockSpec((1, d), lambda b, t: (0, 0)), mod, mod,
                  pl.BlockSpec(w.shape, lambda b, t: (0, 0))],
        out_specs=pl.BlockSpec((w.shape[1] // LANES, 1, tm, LANES), lambda b, t: (0, b, t, 0)),
        out_shape=jax.ShapeDtypeStruct((w.shape[1] // LANES, nb, lt, LANES), F32),
        compiler_params=_cparams("parallel", "parallel"),
        name="s5_in",
    )(xu, g2, scale, shift, w)


def _s5_operators(a_re, a_im, log_step, b_re, b_im, c_re, c_im):
    hp = lax.Precision.HIGHEST
    t_len = S5_CHUNK
    ops = []
    for direction in range(2):
        dt = jnp.exp(log_step[direction])[:, None]
        lre, lim = a_re[direction] * dt, a_im[direction] * dt
        decay = jnp.exp(lre)
        ab_re, ab_im = decay * jnp.cos(lim), decay * jnp.sin(lim)
        den = a_re[direction] ** 2 + a_im[direction] ** 2
        f_re = ((ab_re - 1) * a_re[direction] + ab_im * a_im[direction]) / den
        f_im = (ab_im * a_re[direction] - (ab_re - 1) * a_im[direction]) / den
        bb_re = f_re[..., None] * b_re[direction] - f_im[..., None] * b_im[direction]
        bb_im = f_re[..., None] * b_im[direction] + f_im[..., None] * b_re[direction]
        cr, ci = c_re[direction], c_im[direction]
        tau = jnp.arange(t_len + 1, dtype=F32)[:, None, None]
        pw = jnp.exp(tau * lre[None])
        pw_re, pw_im = pw * jnp.cos(tau * lim[None]), pw * jnp.sin(tau * lim[None])
        ab_b_re = pw_re[..., None] * bb_re[None] - pw_im[..., None] * bb_im[None]
        ab_b_im = pw_re[..., None] * bb_im[None] + pw_im[..., None] * bb_re[None]
        kk = (jnp.einsum('gip,tgpj->tgij', cr, ab_b_re, precision=hp)
              - jnp.einsum('gip,tgpj->tgij', ci, ab_b_im, precision=hp))
        ca_re = cr[None] * pw_re[:, :, None, :] - ci[None] * pw_im[:, :, None, :]
        ca_im = cr[None] * pw_im[:, :, None, :] + ci[None] * pw_re[:, :, None, :]
        s = np.arange(t_len)
        z_pow = (t_len - 1 - s) if direction == 0 else s
        c_pow = (s + 1) if direction == 0 else (t_len - s)
        ops.append((kk, [ab_b_re[z_pow], ab_b_im[z_pow]],
                    [ca_re[c_pow], -ca_im[c_pow]],
                    [pw_re[t_len], pw_im[t_len]]))
    lg = S5_LANE_GROUPS
    nblk = a_re.shape[1] // lg
    eye = jnp.eye(lg, dtype=BF16)
    n_state = 4 * lg * S5_STATE
    kf, kb = ops[0][0], ops[1][0]
    kd = jnp.concatenate([kb[t_len - 1:0:-1], (kf[0] + kb[0])[None], kf[1:t_len]])
    kd = kd.astype(BF16).reshape(2 * t_len - 1, nblk, lg, S5_GROUP, S5_GROUP).transpose(1, 0, 4, 2, 3)
    d_blk = kd[:, :, None] * eye[:, None, :, None]
    d_cat = d_blk.reshape(nblk, 2 * t_len - 1, LANES, LANES).transpose(0, 2, 1, 3).reshape(nblk, LANES, -1)
    wz4 = jnp.stack(ops[0][1] + ops[1][1]).astype(BF16)
    wz4 = wz4.reshape(4, t_len, nblk, lg, S5_STATE, S5_GROUP).transpose(2, 1, 5, 0, 3, 4)
    wc4 = jnp.stack(ops[0][2] + ops[1][2]).astype(BF16)
    wc4 = wc4.reshape(4, t_len, nblk, lg, S5_GROUP, S5_STATE).transpose(2, 1, 4, 0, 3, 5)
    compact = (nblk, t_len * S5_GROUP, n_state)
    group_of_lane = (np.arange(n_state) // S5_STATE) % lg
    mask = np.broadcast_to((group_of_lane[None, :] == np.arange(lg)[:, None])[:, None, :], (lg, S5_GROUP, n_state))
    a_t = jnp.stack(ops[0][3] + ops[1][3]).reshape(4, nblk, lg * S5_STATE).transpose(1, 0, 2)
    return d_cat, wz4.reshape(compact), wc4.reshape(compact), jnp.asarray(mask, BF16), a_t


def _s5_core_kernel(u_ref, dcat_ref, wzc_ref, wcc_ref, mask_ref, at_ref, y_ref,
                    wi_ref, wz_ref, wct_ref, z_scr, h_scr, *, n_ctx_chunks):
    n_chunks = z_scr.shape[0]
    n_lat = n_chunks - n_ctx_chunks
    w = z_scr.shape[1] // 4
    width = wi_ref.shape[1]

    @pl.when(pl.program_id(1) == 0)
    def _():
        for s in range(S5_CHUNK):
            lag0 = (S5_CHUNK - 1 - s) * LANES
            wi_ref[s * LANES:(s + 1) * LANES, :] = dcat_ref[0, :, lag0:lag0 + width]
            rows = slice(s * S5_GROUP, (s + 1) * S5_GROUP)
            for a in range(S5_LANE_GROUPS):
                dst = slice(s * LANES + a * S5_GROUP, s * LANES + (a + 1) * S5_GROUP)
                wz_ref[dst, :] = wzc_ref[0, rows, :] * mask_ref[a]
                wct_ref[dst, :] = wcc_ref[0, rows, :] * mask_ref[a]

    a = jnp.concatenate([u_ref[0, 0, pl.ds(s, n_chunks, stride=S5_CHUNK), :].astype(BF16)
                         for s in range(S5_CHUNK)], axis=1)
    z_scr[...] = _dot(a, wz_ref[...])
    ar_f, ai_f, ar_b, ai_b = [at_ref[0, k:k + 1, :] for k in range(4)]

    def step(k, carry):
        (fr, fi), (br, bi) = carry
        cf = jnp.where(k < n_ctx_chunks, n_lat + k, k - n_ctx_chunks)
        cb = n_chunks - 1 - k
        rf, rb = pl.ds(cf, 1), pl.ds(cb, 1)
        h_scr[rf, 0:w] = fr
        h_scr[rf, w:2 * w] = fi
        h_scr[rb, 2 * w:3 * w] = br
        h_scr[rb, 3 * w:4 * w] = bi
        fwd = (ar_f * fr - ai_f * fi + z_scr[rf, 0:w], ar_f * fi + ai_f * fr + z_scr[rf, w:2 * w])
        bwd = (ar_b * br - ai_b * bi + z_scr[rb, 2 * w:3 * w], ar_b * bi + ai_b * br + z_scr[rb, 3 * w:4 * w])
        return fwd, bwd

    zero = jnp.zeros((1, w), F32)
    lax.fori_loop(0, n_chunks, step, ((zero, zero), (zero, zero)))
    y = _dot(a, wi_ref[...]) + _dot_nt(h_scr[...].astype(BF16), wct_ref[...])
    for t in range(S5_CHUNK):
        y_ref[0, 0, pl.ds(t, n_chunks, stride=S5_CHUNK), :] = y[:, t * LANES:(t + 1) * LANES]


def _s5_core(u_blocks, ops, n_ctx_chunks):
    nblk, nb, lt, lanes = u_blocks.shape
    n_chunks = lt // S5_CHUNK
    d_cat, wz_c, wc_c, mask, a_t = ops
    width = S5_CHUNK * lanes
    n_state = wz_c.shape[2]
    kern = functools.partial(_s5_core_kernel, n_ctx_chunks=n_ctx_chunks)
    per_block = lambda a: pl.BlockSpec((1,) + a.shape[1:], lambda g, b: (g, 0, 0))
    tok = pl.BlockSpec((1, 1, lt, lanes), lambda g, b: (g, b, 0, 0))
    return pl.pallas_call(
        kern,
        grid=(nblk, nb),
        in_specs=[tok, per_block(d_cat), per_block(wz_c), per_block(wc_c),
                  pl.BlockSpec(mask.shape, lambda g, b: (0, 0, 0)), per_block(a_t)],
        out_specs=tok,
        out_shape=jax.ShapeDtypeStruct(u_blocks.shape, F32),
        scratch_shapes=[pltpu.VMEM((width, width), BF16), pltpu.VMEM((width, n_state), BF16),
                        pltpu.VMEM((width, n_state), BF16),
                        pltpu.VMEM((n_chunks, n_state), F32), pltpu.VMEM((n_chunks, n_state), F32)],
        compiler_params=_cparams("arbitrary", "arbitrary"),
        name="s5_core",
    )(u_blocks, d_cat, wz_c, wc_c, mask, a_t)


def _s5_out_kernel(x_ref, u_ref, y_ref, d_ref, w_ref, gate_ref, o_ref):
    d = x_ref.shape[2]
    u = jnp.concatenate([u_ref[k, 0] for k in range(u_ref.shape[0])], axis=1)
    y = jnp.concatenate([y_ref[k, 0] for k in range(y_ref.shape[0])], axis=1)
    y = u * d_ref[...] + y
    z = _dot(jax.nn.gelu(y).astype(BF16), w_ref[...])
    o_ref[0] = x_ref[0] + gate_ref[0] * (z[:, :d] * jax.nn.sigmoid(z[:, d:]))


def _s5_out(x, u_blocks, y_blocks, d_skip, w_glu, gate, seq):
    nb, _, d = x.shape
    tm = 256
    nblk, _, _, lanes = u_blocks.shape
    w = nblk * lanes
    row = lambda b, t: (b, t, 0)
    blk = pl.BlockSpec((nblk, 1, tm, lanes), lambda b, t: (0, b, t, 0))
    wg = w_glu.astype(BF16)
    return pl.pallas_call(
        _s5_out_kernel,
        grid=(nb, seq // tm),
        in_specs=[pl.BlockSpec((1, tm, d), row), blk, blk,
                  pl.BlockSpec((1, w), lambda b, t: (0, 0)), pl.BlockSpec(wg.shape, lambda b, t: (0, 0)),
                  pl.BlockSpec((1, 1, d), lambda b, t: (b, 0, 0))],
        out_specs=pl.BlockSpec((1, tm, d), row),
        out_shape=jax.ShapeDtypeStruct((nb, seq, d), F32),
        compiler_params=_cparams("parallel", "parallel"),
        name="s5_out",
    )(x, u_blocks, y_blocks, d_skip.reshape(1, w), wg, gate)


R_IDX, R_WT, R_RANK = 0, TOP_K, 2 * TOP_K


def _router_kernel(x_ref, g_ref, sc_ref, sh_ref, w_ref, tri_ref, rec_ref, cnt_ref, base_scr):
    @pl.when((pl.program_id(0) == 0) & (pl.program_id(1) == 0))
    def _():
        base_scr[...] = jnp.zeros_like(base_scr)

    t = _norm_mod(x_ref[0], g_ref[...], sc_ref[0], sh_ref[0]).astype(BF16)
    logits = _dot(t, w_ref[...])
    lane = lax.broadcasted_iota(jnp.int32, logits.shape, 1)
    big = jnp.int32(LANES)
    l1 = jnp.where(lane < N_EXPERTS, logits, -jnp.inf)
    v1 = jnp.max(l1, axis=-1, keepdims=True)
    i1 = jnp.min(jnp.where(l1 == v1, lane, big), axis=-1, keepdims=True)
    l2 = jnp.where(lane == i1, -jnp.inf, l1)
    v2 = jnp.max(l2, axis=-1, keepdims=True)
    i2 = jnp.min(jnp.where(l2 == v2, lane, big), axis=-1, keepdims=True)
    e2 = jnp.exp(v2 - v1)
    w1 = 1.0 / (1.0 + e2)
    w2 = e2 / (1.0 + e2)
    hit = jnp.where((lane == i1) | (lane == i2), 1.0, 0.0)
    before = _dot(tri_ref[...], hit.astype(BF16)) + base_scr[0:1, :]
    r1 = jnp.sum(jnp.where(lane == i1, before, 0.0), axis=-1, keepdims=True)
    r2 = jnp.sum(jnp.where(lane == i2, before, 0.0), axis=-1, keepdims=True)
    base_scr[...] = base_scr[...] + jnp.sum(hit, axis=0, keepdims=True)
    cnt_ref[...] = base_scr[...]
    rec = jnp.zeros(logits.shape, F32)
    for k, val in enumerate((i1.astype(F32), i2.astype(F32), w1, w2, r1, r2)):
        rec = jnp.where(lane == k, val, rec)
    rec_ref[0] = rec


def _router(x, norm_g, scale, shift, w_router):
    nb, seq, d = x.shape
    tm = 256
    wr = jnp.zeros((d, LANES), F32).at[:, :N_EXPERTS].set(w_router).astype(BF16)
    tri = jnp.asarray(np.tril(np.ones((tm, tm), np.float32), -1), BF16)
    row = lambda b, t: (b, t, 0)
    mod = pl.BlockSpec((1, 1, d), lambda b, t: (b, 0, 0))
    g2 = norm_g.reshape(1, d)
    return pl.pallas_call(
        _router_kernel,
        grid=(nb, seq // tm),
        in_specs=[pl.BlockSpec((1, tm, d), row), pl.BlockSpec((1, d), lambda b, t: (0, 0)), mod, mod,
                  pl.BlockSpec(wr.shape, lambda b, t: (0, 0)), pl.BlockSpec(tri.shape, lambda b, t: (0, 0))],
        out_specs=[pl.BlockSpec((1, tm, LANES), row), pl.BlockSpec((8, LANES), lambda b, t: (0, 0))],
        out_shape=[jax.ShapeDtypeStruct((nb, seq, LANES), F32), jax.ShapeDtypeStruct((8, LANES), F32)],
        scratch_shapes=[pltpu.VMEM((8, LANES), F32)],
        compiler_params=_cparams("arbitrary", "arbitrary"),
        name="router",
    )(x, g2, scale, shift, wr, tri)


def _row_copies_wait(src_ref, dst_ref, sem):
    pltpu.make_async_copy(src_ref, dst_ref, sem).wait()


def _dispatch_kernel(dest_ref, fill_ref, x_ref, g_ref, sc_ref, sh_ref, xs_ref, tbuf, zbuf, sem, zsem):
    tm = x_ref.shape[1]
    ztm = zbuf.shape[0]
    step = pl.program_id(0) * pl.num_programs(1) + pl.program_id(1)
    n_steps = pl.num_programs(0) * pl.num_programs(1)
    slot = step % 2

    def drain(s):
        for _ in range(TOP_K):
            _row_copies_wait(tbuf.at[s], xs_ref.at[pl.ds(0, tm)], sem.at[s])

    @pl.when(step == 0)
    def _():
        zbuf[...] = jnp.zeros_like(zbuf)

        def fill(start):
            def body(i, carry):
                @pl.when(fill_ref[i] > 0)
                def _():
                    cp = pltpu.make_async_copy(zbuf, xs_ref.at[pl.ds(pl.multiple_of(i * ztm, ztm), ztm)], zsem)
                    if start:
                        cp.start()
                    else:
                        cp.wait()
                return carry
            lax.fori_loop(0, fill_ref.shape[0], body, 0)

        fill(True)
        fill(False)

    @pl.when(step >= 2)
    def _():
        drain(slot)

    tbuf[slot] = _norm_mod(x_ref[0], g_ref[...], sc_ref[0], sh_ref[0])
    base = step * (tm * TOP_K)

    def body(r, carry):
        for k in range(TOP_K):
            dst = dest_ref[base + r * TOP_K + k]
            pltpu.make_async_copy(tbuf.at[slot, pl.ds(r, 1)], xs_ref.at[pl.ds(dst, 1)],
                                  sem.at[slot]).start(priority=k % 2)
        return carry

    lax.fori_loop(0, tm, body, 0, unroll=8)

    @pl.when(step == n_steps - 1)
    def _():
        drain(slot)

        @pl.when(n_steps > 1)
        def _():
            drain(1 - slot)


def _dispatch(x, norm_g, scale, shift, dest, tile_fill, n_rows, expert_tm):
    nb, seq, d = x.shape
    tm = 256
    row = lambda b, t, dr, fr: (b, t, 0)
    mod = pl.BlockSpec((1, 1, d), lambda b, t, dr, fr: (b, 0, 0))
    g2 = norm_g.reshape(1, d)
    grid_spec = pltpu.PrefetchScalarGridSpec(
        num_scalar_prefetch=2,
        grid=(nb, seq // tm),
        in_specs=[pl.BlockSpec((1, tm, d), row), pl.BlockSpec((1, d), lambda b, t, dr, fr: (0, 0)), mod, mod],
        out_specs=pl.BlockSpec(memory_space=pl.ANY),
        scratch_shapes=[pltpu.VMEM((2, tm, d), F32), pltpu.VMEM((expert_tm, d), F32),
                        pltpu.SemaphoreType.DMA((2,)), pltpu.SemaphoreType.DMA(())],
    )
    return pl.pallas_call(
        _dispatch_kernel,
        grid_spec=grid_spec,
        out_shape=jax.ShapeDtypeStruct((n_rows, d), F32),
        compiler_params=_cparams("arbitrary", "arbitrary"),
        name="dispatch",
    )(dest, tile_fill, x, g2, scale, shift)


def _expert_kernel(te_ref, tr_ref, x_ref, wg_ref, wu_ref, wd_ref, o_ref, xb_scr, acc_scr):
    i = pl.program_id(0)
    f = pl.program_id(1)
    nf = pl.num_programs(1)
    live = tr_ref[i] > 0

    @pl.when(jnp.logical_not(live) & (f == nf - 1))
    def _():
        o_ref[...] = jnp.zeros_like(o_ref)

    @pl.when(live)
    def _():
        @pl.when(f == 0)
        def _():
            xb_scr[...] = x_ref[...].astype(BF16)
            acc_scr[...] = jnp.zeros_like(acc_scr)

        x = xb_scr[...]
        a = _dot(x, wg_ref[0])
        u = _dot(x, wu_ref[0])
        acc_scr[...] += _dot((a * jax.nn.sigmoid(a) * u).astype(BF16), wd_ref[0])

        @pl.when(f == nf - 1)
        def _():
            o_ref[...] = acc_scr[...]


def _experts(xs, tile_expert, tile_rows, w_gate, w_up, w_down, tm, tf):
    n_rows, d = xs.shape
    dff = w_gate.shape[2]
    nf = dff // tf

    def fsel(i, f, te, tr):
        return jnp.where(tr[i] > 0, f, nf - 1)

    grid_spec = pltpu.PrefetchScalarGridSpec(
        num_scalar_prefetch=2,
        grid=(n_rows // tm, nf),
        in_specs=[pl.BlockSpec((tm, d), lambda i, f, te, tr: (i, 0)),
                  pl.BlockSpec((1, d, tf), lambda i, f, te, tr: (te[i], 0, fsel(i, f, te, tr))),
                  pl.BlockSpec((1, d, tf), lambda i, f, te, tr: (te[i], 0, fsel(i, f, te, tr))),
                  pl.BlockSpec((1, tf, d), lambda i, f, te, tr: (te[i], fsel(i, f, te, tr), 0))],
        out_specs=pl.BlockSpec((tm, d), lambda i, f, te, tr: (i, 0)),
        scratch_shapes=[pltpu.VMEM((tm, d), BF16), pltpu.VMEM((tm, d), F32)],
    )
    return pl.pallas_call(
        _expert_kernel,
        grid_spec=grid_spec,
        out_shape=jax.ShapeDtypeStruct((n_rows, d), F32),
        compiler_params=_cparams("arbitrary", "arbitrary"),
        name="experts",
    )(tile_expert, tile_rows, xs, w_gate, w_up, w_down)


def _combine_kernel(dest_ref, x_ref, rec_ref, gate_ref, g_ref, ys_ref, o_ref, ybuf, sem):
    tm = x_ref.shape[1]
    step = pl.program_id(0) * pl.num_programs(1) + pl.program_id(1)
    n_steps = pl.num_programs(0) * pl.num_programs(1)
    slot = step % 2

    def issue(st, sl):
        base = st * (tm * TOP_K)

        def body(r, carry):
            for k in range(TOP_K):
                src = dest_ref[base + r * TOP_K + k]
                pltpu.make_async_copy(ys_ref.at[pl.ds(src, 1)], ybuf.at[sl, k, pl.ds(r, 1)],
                                      sem.at[sl]).start(priority=k % 2)
            return carry

        lax.fori_loop(0, tm, body, 0, unroll=8)

    @pl.when(step == 0)
    def _():
        issue(step, slot)

    @pl.when(step + 1 < n_steps)
    def _():
        issue(step + 1, 1 - slot)

    for k in range(TOP_K):
        _row_copies_wait(ys_ref.at[pl.ds(0, tm)], ybuf.at[slot, k], sem.at[slot])

    rec = rec_ref[0]
    y = rec[:, R_WT:R_WT + 1] * ybuf[slot, 0] + rec[:, R_WT + 1:R_WT + 2] * ybuf[slot, 1]
    x = x_ref[0] + gate_ref[0] * y
    o_ref[0] = _rms(x) * g_ref[...]


def _combine(x, rec, gate, final_g, ys, dest):
    nb, seq, d = x.shape
    tm = 256
    row = lambda b, t, dr: (b, t, 0)
    grid_spec = pltpu.PrefetchScalarGridSpec(
        num_scalar_prefetch=1,
        grid=(nb, seq // tm),
        in_specs=[pl.BlockSpec((1, tm, d), row), pl.BlockSpec((1, tm, LANES), row),
                  pl.BlockSpec((1, 1, d), lambda b, t, dr: (b, 0, 0)), pl.BlockSpec((1, d), lambda b, t, dr: (0, 0)),
                  pl.BlockSpec(memory_space=pl.ANY)],
        out_specs=pl.BlockSpec((1, tm, d), row),
        scratch_shapes=[pltpu.VMEM((2, TOP_K, tm, d), F32), pltpu.SemaphoreType.DMA((2,))],
    )
    return pl.pallas_call(
        _combine_kernel,
        grid_spec=grid_spec,
        out_shape=jax.ShapeDtypeStruct(x.shape, F32),
        compiler_params=_cparams("arbitrary", "arbitrary"),
        name="combine_final_norm",
    )(dest, x, rec, gate, final_g.reshape(1, d), ys)


def _moe(x, norm_g, scale, shift, gate, w_router, w_gate, w_up, w_down, final_g):
    nb, seq, d = x.shape
    n_tok = nb * seq
    tm, tf = 512, w_gate.shape[2] // 2
    rec, cnt = _router(x, norm_g, scale, shift, w_router)
    counts = cnt[0, :N_EXPERTS].astype(jnp.int32)
    padded = ((counts + tm - 1) // tm) * tm
    pend = jnp.cumsum(padded)
    pstart = pend - padded
    idx = rec[:, :, R_IDX:R_IDX + TOP_K].astype(jnp.int32)
    rank = rec[:, :, R_RANK:R_RANK + TOP_K].astype(jnp.int32)
    seg = jnp.sum(jnp.where(idx[..., None] == jnp.arange(N_EXPERTS), pstart, 0), axis=-1)
    dest = (seg + rank).reshape(n_tok * TOP_K)
    n_rows = n_tok * TOP_K + N_EXPERTS * tm
    tile_start = jnp.arange(n_rows // tm, dtype=jnp.int32) * tm
    tile_expert = jnp.sum((tile_start[:, None] >= pend[None, :]).astype(jnp.int32), axis=1)
    live = tile_expert < N_EXPERTS
    last_live = jnp.max(jnp.where(live, tile_expert, 0))
    tile_expert = jnp.where(live, tile_expert, last_live)
    seg_rows = jnp.sum(jnp.where(tile_expert[:, None] == jnp.arange(N_EXPERTS), (pstart + counts)[None, :], 0), axis=1)
    tile_rows = jnp.where(live, jnp.clip(seg_rows - tile_start, 0, tm), 0).astype(jnp.int32)
    tile_fill = (tile_rows < tm).astype(jnp.int32)

    xs = _dispatch(x, norm_g, scale, shift, dest, tile_fill, n_rows, tm)
    ys = _experts(xs, tile_expert.astype(jnp.int32), tile_rows,
                  w_gate.astype(BF16), w_up.astype(BF16), w_down.astype(BF16), tm, tf)
    return _combine(x, rec, gate, final_g, ys, dest)


def kernel(x, c, ctx, c_ctx, mod_w, mod_b, norm1_g, norm2_g, ev_w_in, ev_q_norm_g, ev_w_qb, ev_kv_norm_g, ev_w_kvb, ev_na_rpb, ev_w_out, ev_ffn_w_gate, ev_ffn_w_up, ev_ffn_w_down, od_w_in, od_a_re, od_a_im, od_log_step, od_b_re, od_b_im, od_c_re, od_c_im, od_d, od_w_glu, moe_w_router, moe_w_gate, moe_w_up, moe_w_down, final_g):
    nb, seq, d = x.shape
    ctx_len = ctx.shape[1]
    assert mod_w.shape[0] == 2 and nb < 8
    assert seq % (NA_ROWS_PER_BLOCK * GRID_W) == 0 and ctx_len == NA_ROWS_PER_BLOCK * GRID_W

    cond = jnp.zeros((8, d), F32).at[:nb].set(c).at[nb].set(c_ctx)
    mods = _adaln(cond, mod_w, mod_b)

    def mod_parts(layer):
        return [mods[layer, :, i * d:(i + 1) * d].reshape(8, 1, d) for i in range(N_MOD)]

    sh1, sc1, g1, sh2, sc2, g2 = mod_parts(0)
    cos, sin = _rope_tables(seq, ctx_len)
    weights = _even_weights(ev_w_in[0], ev_w_qb[0], ev_w_kvb[0])
    q, k, v, nq, nk, nv = _even_project(x, ctx, norm1_g[0], sc1, sh1, weights, ev_q_norm_g[0], ev_kv_norm_g[0],
                                        cos, sin)
    mla = _mla_attention(q, k, v, seq, ctx_len)
    bias = _na_bias_tables(ev_na_rpb[0], seq // GRID_W)
    na = _na_attention(nq, nk, nv, bias, seq, ctx_len)
    xu = _out_proj(x, ctx, mla, na, ev_w_out[0], g1)
    xu = _ffn(xu, norm2_g[0], sc2, sh2, g2, ev_ffn_w_gate[0], ev_ffn_w_up[0], ev_ffn_w_down[0], seq)

    sh1, sc1, g1, sh2, sc2, g2 = mod_parts(1)
    u = _s5_in(xu, norm1_g[1], sc1, sh1, od_w_in[0], seq)
    ops = _s5_operators(od_a_re[0], od_a_im[0], od_log_step[0], od_b_re[0], od_b_im[0], od_c_re[0], od_c_im[0])
    y = _s5_core(u, ops, ctx_len // S5_CHUNK)
    xl = _s5_out(xu, u, y, od_d[0], od_w_glu[0], g1[:nb], seq)
    return _moe(xl, norm2_g[1], sc2[:nb], sh2[:nb], g2[:nb], moe_w_router[0], moe_w_gate[0], moe_w_up[0],
                moe_w_down[0], final_g)
```

```python
import functools
import math

import numpy as np
import jax
import jax.numpy as jnp
from jax import lax
from jax.experimental import pallas as pl
from jax.experimental.pallas import tpu as pltpu

F32 = jnp.float32
BF16 = jnp.bfloat16

LANES = 128
VMEM_LIMIT_BYTES = 52 * 1024 * 1024

NORM_EPS = 1e-6
ROPE_BASE = 10000.0
GRID_W = 64
N_MOD = 6

MLA_HEADS = 8
MLA_NOPE = 64
MLA_ROPE = 32
MLA_V = 64
Q_LORA = 384
KV_LORA = 256
MLA_SCALE = (MLA_NOPE + MLA_ROPE) ** -0.5
MLA_EXP2_SCALE = MLA_SCALE * math.log2(math.e)

NA_HEADS = 8
NA_HEAD_DIM = 64
NA_WIN_H = 8
NA_WIN_W = 16
NA_SCALE = NA_HEAD_DIM ** -0.5
NA_EXP2_SCALE = NA_SCALE * math.log2(math.e)
NA_ROWS_PER_BLOCK = 4
NA_KEY_ROWS = NA_ROWS_PER_BLOCK + NA_WIN_H - 1

S5_GROUP = 16
S5_STATE = 64
S5_CHUNK = 16
S5_LANE_GROUPS = LANES // S5_GROUP

N_EXPERTS = 8
TOP_K = 2

NEG_BIG = -1e30


def _cparams(*sem):
    return pltpu.CompilerParams(dimension_semantics=sem, vmem_limit_bytes=VMEM_LIMIT_BYTES)


def _rms(x):
    return x * lax.rsqrt(jnp.mean(x * x, axis=-1, keepdims=True) + NORM_EPS)


def _norm_mod(x, g, scale, shift):
    return (_rms(x) * g) * (1 + scale) + shift


def _dot(a, b):
    return jnp.dot(a, b, preferred_element_type=F32)


def _dot_nt(a, b):
    return lax.dot_general(a, b, (((1,), (1,)), ((), ())), preferred_element_type=F32)


def _adaln_kernel(c_ref, w_ref, b_ref, o_ref):
    c = c_ref[...]
    s = (c * jax.nn.sigmoid(c)).astype(BF16)
    o_ref[0] = _dot(s, w_ref[0].astype(BF16)) + b_ref[0]


def _adaln(cond, mod_w, mod_b):
    nl, d, n = mod_w.shape
    tn = 1536
    return pl.pallas_call(
        _adaln_kernel,
        grid=(nl, n // tn),
        in_specs=[pl.BlockSpec((8, d), lambda l, j: (0, 0)),
                  pl.BlockSpec((1, d, tn), lambda l, j: (l, 0, j)),
                  pl.BlockSpec((1, 1, tn), lambda l, j: (l, 0, j))],
        out_specs=pl.BlockSpec((1, 8, tn), lambda l, j: (l, 0, j)),
        out_shape=jax.ShapeDtypeStruct((nl, 8, n), F32),
        compiler_params=_cparams("parallel", "parallel"),
        name="adaln",
    )(cond, mod_w, mod_b.reshape(nl, 1, n))


def _tile_rows(x_ref, c_ref):
    return jnp.where(pl.program_id(1) < pl.num_programs(1) - 1, x_ref[0], c_ref[0])


def _tile_specs(tm, d, n_lat):
    return [pl.BlockSpec((1, tm, d), lambda b, t: (b, jnp.minimum(t, n_lat - 1), 0)),
            pl.BlockSpec((1, tm, d), lambda b, t: (b, 0, 0))]


def _even_proj_kernel(x_ref, c_ref, g_ref, sc_ref, sh_ref, win_ref, qg_ref, kvg_ref, wq_ref, wqr_ref, wkk_ref,
                      wkv_ref, cos_ref, sin_ref, q_ref, k_ref, v_ref, nq_ref, nk_ref, nv_ref):
    h = _norm_mod(_tile_rows(x_ref, c_ref), g_ref[...], sc_ref[0], sh_ref[0]).astype(BF16)
    p = _dot(h, win_ref[...])
    c0, c1, c2, c3 = Q_LORA, Q_LORA + KV_LORA, Q_LORA + KV_LORA + LANES, Q_LORA + KV_LORA + 2 * LANES
    cqn = (_rms(p[:, :c0]) * qg_ref[...]).astype(BF16)
    ckvn = (_rms(p[:, c0:c1]) * kvg_ref[...]).astype(BF16)
    cos = cos_ref[...]
    sin = sin_ref[...]
    kr = p[:, c1:c2] * cos + p[:, c2:c3] * sin
    qa = _dot(cqn, wq_ref[...])
    qb = _dot(cqn, wqr_ref[...])
    kk = _dot(ckvn, wkk_ref[...])
    for hd in range(MLA_HEADS):
        sl = slice(hd * LANES, (hd + 1) * LANES)
        q_ref[0, :, sl] = ((qa[:, sl] * cos + qb[:, sl] * sin) * MLA_EXP2_SCALE).astype(BF16)
        k_ref[0, :, sl] = (kk[:, sl] + kr).astype(BF16)
    vlane = lax.broadcasted_iota(jnp.int32, (1, MLA_HEADS * LANES), 1) & (LANES - 1)
    v_ref[0] = (_dot(ckvn, wkv_ref[...]) + jnp.where(vlane == MLA_V, 1.0, 0.0)).astype(BF16)
    w = NA_HEADS * NA_HEAD_DIM
    nq_ref[0] = (p[:, c3:c3 + w] * NA_EXP2_SCALE).astype(BF16)
    nk_ref[0] = p[:, c3 + w:c3 + 2 * w].astype(BF16)
    nv_ref[0] = (p[:, c3 + 2 * w:c3 + 2 * w + NA_HEADS * LANES]
                 + jnp.where(vlane == NA_HEAD_DIM, 1.0, 0.0)).astype(BF16)


def _rot_half_cols(w):
    q = MLA_ROPE // 4
    return jnp.concatenate([-w[:, q:2 * q], w[:, :q], -w[:, 3 * q:], w[:, 2 * q:3 * q]], axis=1)


def _rope_tables(seq, ctx_len):
    q = MLA_ROPE // 4
    t = np.arange(seq)
    inv_freq = np.float32(ROPE_BASE) ** (-np.arange(q, dtype=np.float32) / np.float32(q))
    ang_r = (t // GRID_W).astype(np.float32)[:, None] * inv_freq[None, :]
    ang_c = (t % GRID_W).astype(np.float32)[:, None] * inv_freq[None, :]
    cos = np.ones((seq + ctx_len, LANES), np.float32)
    sin = np.zeros((seq + ctx_len, LANES), np.float32)
    cos[:seq, MLA_NOPE:MLA_NOPE + MLA_ROPE] = np.concatenate([np.cos(ang_r)] * 2 + [np.cos(ang_c)] * 2, axis=1)
    sin[:seq, MLA_NOPE:MLA_NOPE + MLA_ROPE] = np.concatenate([np.sin(ang_r)] * 2 + [np.sin(ang_c)] * 2, axis=1)
    return jnp.asarray(cos), jnp.asarray(sin)


def _even_weights(w_in, w_qb, w_kvb):
    d = w_in.shape[0]
    c1 = Q_LORA + KV_LORA
    wkr = w_in[:, c1:c1 + MLA_ROPE]
    pad = lambda w: jnp.zeros((d, LANES), F32).at[:, MLA_NOPE:MLA_NOPE + MLA_ROPE].set(w)
    w_na = NA_HEADS * NA_HEAD_DIM
    na_qk = w_in[:, c1 + MLA_ROPE:c1 + MLA_ROPE + 2 * w_na]
    na_v = w_in[:, c1 + MLA_ROPE + 2 * w_na:].reshape(d, NA_HEADS, NA_HEAD_DIM)
    na_v = jnp.concatenate([na_v, jnp.zeros((d, NA_HEADS, LANES - NA_HEAD_DIM), F32)], axis=2)
    win = jnp.concatenate([w_in[:, :c1], pad(wkr), pad(_rot_half_cols(wkr)), na_qk,
                           na_v.reshape(d, NA_HEADS * LANES)], axis=1)
    hq = MLA_NOPE + MLA_ROPE
    wq = w_qb.reshape(Q_LORA, MLA_HEADS, hq)
    zq = jnp.zeros((Q_LORA, MLA_HEADS, LANES - hq), F32)
    wq_main = jnp.concatenate([wq, zq], axis=2).reshape(Q_LORA, MLA_HEADS * LANES)
    rot = jnp.stack([_rot_half_cols(wq[:, h, MLA_NOPE:]) for h in range(MLA_HEADS)], axis=1)
    wq_rot = jnp.concatenate([jnp.zeros((Q_LORA, MLA_HEADS, MLA_NOPE), F32), rot, zq], axis=2)
    wq_rot = wq_rot.reshape(Q_LORA, MLA_HEADS * LANES)
    wkv = w_kvb.reshape(KV_LORA, MLA_HEADS, MLA_NOPE + MLA_V)
    wkk = jnp.concatenate([wkv[:, :, :MLA_NOPE], jnp.zeros((KV_LORA, MLA_HEADS, LANES - MLA_NOPE), F32)], axis=2)
    wkk = wkk.reshape(KV_LORA, MLA_HEADS * LANES)
    wv = jnp.concatenate([wkv[:, :, MLA_NOPE:], jnp.zeros((KV_LORA, MLA_HEADS, LANES - MLA_V), F32)], axis=2)
    wv = wv.reshape(KV_LORA, MLA_HEADS * LANES)
    return tuple(a.astype(BF16) for a in (win, wq_main, wq_rot, wkk, wv))


def _mod_spec(n_lat_tiles, nb, d):
    return pl.BlockSpec((1, 1, d), lambda b, t: (jnp.where(t < n_lat_tiles, b, nb), 0, 0))


def _even_project(x, ctx, norm_g, scale, shift, weights, q_norm_g, kv_norm_g, cos, sin):
    nb, seq, d = x.shape
    tm = ctx.shape[1]
    lt = seq + tm
    win, wq, wqr, wkk, wv = weights
    n_lat = seq // tm
    row = lambda b, t: (b, t, 0)
    full = lambda a: pl.BlockSpec(a.shape, lambda b, t: (0,) * a.ndim)
    mod = _mod_spec(n_lat, nb, d)
    g2 = norm_g.reshape(1, d)
    qg2 = q_norm_g.reshape(1, Q_LORA)
    kvg2 = kv_norm_g.reshape(1, KV_LORA)
    wide = MLA_HEADS * LANES
    half = MLA_HEADS * MLA_V
    outs = [jax.ShapeDtypeStruct((nb, lt, wide), BF16), jax.ShapeDtypeStruct((nb, lt, wide), BF16)] + \
           [jax.ShapeDtypeStruct((nb, lt, wide), BF16)] + [jax.ShapeDtypeStruct((nb, lt, half), BF16)] * 2 + \
           [jax.ShapeDtypeStruct((nb, lt, wide), BF16)]
    return pl.pallas_call(
        _even_proj_kernel,
        grid=(nb, lt // tm),
        in_specs=_tile_specs(tm, d, n_lat) + [full(g2), mod, mod, full(win), full(qg2), full(kvg2),
                  full(wq), full(wqr), full(wkk), full(wv),
                  pl.BlockSpec((tm, LANES), lambda b, t: (t, 0)), pl.BlockSpec((tm, LANES), lambda b, t: (t, 0))],
        out_specs=[pl.BlockSpec((1, tm, wide), row), pl.BlockSpec((1, tm, wide), row)] +
                  [pl.BlockSpec((1, tm, wide), row)] + [pl.BlockSpec((1, tm, half), row)] * 2 +
                  [pl.BlockSpec((1, tm, wide), row)],
        out_shape=outs,
        compiler_params=_cparams("parallel", "parallel"),
        name="even_project",
    )(x, ctx, g2, scale, shift, win, qg2, kvg2, wq, wqr, wkk, wv, cos, sin)


MLA_KEY_CHUNK = 512


def _mla_kernel(q_ref, k_ref, v_ref, o_ref, s_a, s_b, p_a, p_b, *, seq, ctx_len):
    tq = q_ref.shape[1]
    n_lat_tiles = seq // tq
    t = pl.program_id(1)
    lane = lax.broadcasted_iota(jnp.int32, (tq, LANES), 1)
    stages = ((s_a, p_a), (s_b, p_b))
    head_lanes = lambda h: slice(h * LANES, (h + 1) * LANES)

    def run(key0, n_keys):
        def scores(h):
            s_scr = stages[h % 2][0]
            q = q_ref[0, :, head_lanes(h)]
            m = None
            for c0 in range(0, n_keys, MLA_KEY_CHUNK):
                w = min(MLA_KEY_CHUNK, n_keys - c0)
                s = _dot_nt(q, k_ref[0, key0 + c0:key0 + c0 + w, head_lanes(h)])
                s_scr[:, c0:c0 + w] = s
                for l0 in range(0, w, LANES):
                    m = s[:, l0:l0 + LANES] if m is None else jnp.maximum(m, s[:, l0:l0 + LANES])
            return jnp.max(m, axis=-1, keepdims=True)

        def attend(h, m):
            s_scr, p_scr = stages[h % 2]
            p_scr[:, 0:n_keys] = jnp.exp2((s_scr[:, 0:n_keys] - m).astype(BF16))
            acc = _dot(p_scr[:, 0:n_keys], v_ref[0, key0:key0 + n_keys, head_lanes(h)])
            return acc / acc[:, MLA_V:MLA_V + 1]

        m_next = scores(0)
        for h in range(MLA_HEADS):
            m_cur = m_next
            if h + 1 < MLA_HEADS:
                m_next = scores(h + 1)
            out = attend(h, m_cur)
            if h % 2 == 0:
                first = out
            else:
                pair = h // 2
                second = pltpu.roll(out, MLA_V, 1)
                o_ref[0, :, pair * LANES:(pair + 1) * LANES] = jnp.where(lane < MLA_V, first, second).astype(BF16)

    @pl.when(t < n_lat_tiles)
    def _():
        run(0, seq + ctx_len)

    @pl.when(t >= n_lat_tiles)
    def _():
        run(seq, ctx_len)


def _mla_attention(q, k, v, seq, ctx_len):
    nb, lt, wide = q.shape
    half = MLA_HEADS * MLA_V
    tq = 256
    kern = functools.partial(_mla_kernel, seq=seq, ctx_len=ctx_len)
    per_batch = pl.BlockSpec((1, lt, wide), lambda b, t: (b, 0, 0), pipeline_mode=pl.Buffered(1))
    return pl.pallas_call(
        kern,
        grid=(nb, lt // tq),
        in_specs=[pl.BlockSpec((1, tq, wide), lambda b, t: (b, t, 0)), per_batch, per_batch],
        out_specs=pl.BlockSpec((1, tq, half), lambda b, t: (b, t, 0)),
        out_shape=jax.ShapeDtypeStruct((nb, lt, half), BF16),
        scratch_shapes=[pltpu.VMEM((tq, lt), F32), pltpu.VMEM((tq, lt), F32),
                        pltpu.VMEM((tq, lt), BF16), pltpu.VMEM((tq, lt), BF16)],
        compiler_params=_cparams("parallel", "parallel"),
        name="mla_attention",
    )(q, k, v)


def _na_bias_tables(rpb, rows):
    kh = min(NA_WIN_H, rows)
    last_r0 = rows - NA_ROWS_PER_BLOCK
    tabs = []
    for r0 in (0, 2 * NA_ROWS_PER_BLOCK, last_r0):
        kstart = int(np.clip(r0 - kh // 2, 0, rows - NA_KEY_ROWS))
        r = r0 + np.arange(NA_ROWS_PER_BLOCK)
        rs = np.clip(r - kh // 2, 0, rows - kh)
        kr = kstart + np.arange(NA_KEY_ROWS)
        row_ok = (kr[None, :] >= rs[:, None]) & (kr[None, :] < rs[:, None] + kh)
        row_off = np.clip(kr[None, :] - r[:, None] + (NA_WIN_H - 1), 0, 2 * NA_WIN_H - 2)
        c = np.arange(GRID_W)
        cs = np.clip(c - NA_WIN_W // 2, 0, GRID_W - NA_WIN_W)
        col_ok = (c[None, :] >= cs[:, None]) & (c[None, :] < cs[:, None] + NA_WIN_W)
        col_off = np.clip(c[None, :] - c[:, None] + (NA_WIN_W - 1), 0, 2 * NA_WIN_W - 2)
        sel_r = jnp.asarray(row_off[..., None] == np.arange(2 * NA_WIN_H - 1), F32)
        sel_c = jnp.asarray(col_off[..., None] == np.arange(2 * NA_WIN_W - 1), F32)
        b = jnp.einsum('aeu,huv,cdv->haced', sel_r, rpb, sel_c, precision=lax.Precision.HIGHEST)
        ok = row_ok[:, None, :, None] & col_ok[None, :, None, :]
        b = jnp.where(jnp.asarray(ok)[None], b * math.log2(math.e), NEG_BIG)
        tabs.append(b.reshape(rpb.shape[0], NA_ROWS_PER_BLOCK * GRID_W, NA_KEY_ROWS * GRID_W))
    return jnp.stack(tabs)


def _na_kernel(q_ref, k_ref, v_ref, bias_ref, o_ref, *, seq, ctx_len):
    tq = q_ref.shape[1]
    rows = seq // GRID_W
    n_lat_tiles = seq // tq
    nkeys = NA_KEY_ROWS * GRID_W
    t = pl.program_id(1)
    lane = lax.broadcasted_iota(jnp.int32, (tq, LANES), 1)
    kstart = jnp.clip(t * NA_ROWS_PER_BLOCK - NA_WIN_H // 2, 0, rows - NA_KEY_ROWS)
    koff = pl.multiple_of(kstart * GRID_W, GRID_W)

    def run(with_window):
        for pair in range(NA_HEADS // 2):
            sl = slice(pair * LANES, (pair + 1) * LANES)
            q2 = q_ref[0, :, sl].astype(F32)
            kc = k_ref[0, seq:seq + ctx_len, sl]
            if with_window:
                kw = k_ref[0, pl.ds(koff, nkeys), sl]
            outs = []
            for sub in range(2):
                hl = slice((2 * pair + sub) * LANES, (2 * pair + sub + 1) * LANES)
                keep = (lane < NA_HEAD_DIM) if sub == 0 else (lane >= NA_HEAD_DIM)
                q = jnp.where(keep, q2, 0.0).astype(BF16)
                s_c = _dot_nt(q, kc)
                m = jnp.max(s_c, axis=-1, keepdims=True)
                if with_window:
                    s_w = _dot_nt(q, kw) + bias_ref[0, 2 * pair + sub]
                    m = jnp.maximum(m, jnp.max(s_w, axis=-1, keepdims=True))
                acc = _dot(jnp.exp2((s_c - m).astype(BF16)), v_ref[0, seq:seq + ctx_len, hl])
                if with_window:
                    acc = acc + _dot(jnp.exp2((s_w - m).astype(BF16)), v_ref[0, pl.ds(koff, nkeys), hl])
                outs.append(acc / acc[:, NA_HEAD_DIM:NA_HEAD_DIM + 1])
            second = pltpu.roll(outs[1], NA_HEAD_DIM, 1)
            o_ref[0, :, sl] = jnp.where(lane < NA_HEAD_DIM, outs[0], second).astype(BF16)

    @pl.when(t < n_lat_tiles)
    def _():
        run(True)

    @pl.when(t >= n_lat_tiles)
    def _():
        run(False)


def _na_attention(q, k, v, bias, seq, ctx_len):
    nb, lt, w = q.shape
    tq = NA_ROWS_PER_BLOCK * GRID_W
    n_lat = seq // tq
    kern = functools.partial(_na_kernel, seq=seq, ctx_len=ctx_len)

    def variant(b, t):
        return (jnp.where(t == 0, 0, jnp.where(t >= n_lat - 1, 2, 1)), 0, 0, 0)

    return pl.pallas_call(
        kern,
        grid=(nb, lt // tq),
        in_specs=[pl.BlockSpec((1, tq, w), lambda b, t: (b, t, 0)),
                  pl.BlockSpec((1, lt, w), lambda b, t: (b, 0, 0), pipeline_mode=pl.Buffered(1)),
                  pl.BlockSpec((1, lt, v.shape[2]), lambda b, t: (b, 0, 0), pipeline_mode=pl.Buffered(1)),
                  pl.BlockSpec((1,) + bias.shape[1:], variant)],
        out_specs=pl.BlockSpec((1, tq, w), lambda b, t: (b, t, 0)),
        out_shape=jax.ShapeDtypeStruct((nb, lt, w), BF16),
        compiler_params=_cparams("parallel", "parallel"),
        name="na_attention",
    )(q, k, v, bias)


def _even_tail_kernel(x_ref, c_ref, a_ref, b_ref, wa_ref, wb_ref, gate1_ref, g2_ref, sc2_ref, sh2_ref, gate2_ref,
                      wg_ref, wu_ref, wd_ref, g3_ref, sc3_ref, sh3_ref, win_ref, o_ref, u_ref):
    x1 = _tile_rows(x_ref, c_ref) + gate1_ref[0] * (_dot(a_ref[0], wa_ref[...]) + _dot(b_ref[0], wb_ref[...]))
    h = _norm_mod(x1, g2_ref[...], sc2_ref[0], sh2_ref[0]).astype(BF16)
    a = _dot(h, wg_ref[...])
    up = _dot(h, wu_ref[...])
    x2 = x1 + gate2_ref[0] * _dot((a * jax.nn.sigmoid(a) * up).astype(BF16), wd_ref[...])
    o_ref[0] = x2
    u = _dot(_norm_mod(x2, g3_ref[...], sc3_ref[0], sh3_ref[0]).astype(BF16), win_ref[...])
    for k in range(u_ref.shape[0]):
        u_ref[k, 0] = u[:, k * LANES:(k + 1) * LANES]


def _even_tail(x, ctx, a, b, w_out, gate1, norm2_g, scale2, shift2, gate2, w_gate, w_up, w_down,
               norm3_g, scale3, shift3, w_s5_in):
    nb, seq, d = x.shape
    tm = ctx.shape[1]
    lt = seq + tm
    ka = a.shape[2]
    weights = [w.astype(BF16) for w in (w_out[:ka], w_out[ka:], w_gate, w_up, w_down, w_s5_in)]
    wa, wb, wg, wu, wd, win = weights
    row = lambda bb, t: (bb, t, 0)
    once = lambda arr: pl.BlockSpec(arr.shape, lambda bb, t: (0,) * arr.ndim, pipeline_mode=pl.Buffered(1))
    mod = _mod_spec(seq // tm, nb, d)
    vec = pl.BlockSpec((1, d), lambda bb, t: (0, 0))
    nblk = win.shape[1] // LANES
    return pl.pallas_call(
        _even_tail_kernel,
        grid=(nb, lt // tm),
        in_specs=_tile_specs(tm, d, seq // tm) + [pl.BlockSpec((1, tm, ka), row), pl.BlockSpec((1, tm, b.shape[2]), row),
                  once(wa), once(wb), mod, vec, mod, mod, mod, once(wg), once(wu), once(wd), vec, mod, mod, once(win)],
        out_specs=[pl.BlockSpec((1, tm, d), row), pl.BlockSpec((nblk, 1, tm, LANES), lambda bb, t: (0, bb, t, 0))],
        out_shape=[jax.ShapeDtypeStruct((nb, lt, d), F32), jax.ShapeDtypeStruct((nblk, nb, lt, LANES), F32)],
        compiler_params=_cparams("parallel", "parallel"),
        name="even_tail",
    )(x, ctx, a, b, wa, wb, gate1, norm2_g.reshape(1, d), scale2, shift2, gate2, wg, wu, wd,
      norm3_g.reshape(1, d), scale3, shift3, win)


def _s5_operators(a_re, a_im, log_step, b_re, b_im, c_re, c_im):
    hp = lax.Precision.HIGHEST
    t_len = S5_CHUNK
    ops = []
    for direction in range(2):
        dt = jnp.exp(log_step[direction])[:, None]
        lre, lim = a_re[direction] * dt, a_im[direction] * dt
        decay = jnp.exp(lre)
        ab_re, ab_im = decay * jnp.cos(lim), decay * jnp.sin(lim)
        den = a_re[direction] ** 2 + a_im[direction] ** 2
        f_re = ((ab_re - 1) * a_re[direction] + ab_im * a_im[direction]) / den
        f_im = (ab_im * a_re[direction] - (ab_re - 1) * a_im[direction]) / den
        bb_re = f_re[..., None] * b_re[direction] - f_im[..., None] * b_im[direction]
        bb_im = f_re[..., None] * b_im[direction] + f_im[..., None] * b_re[direction]
        cr, ci = c_re[direction], c_im[direction]
        tau = jnp.arange(t_len + 1, dtype=F32)[:, None, None]
        pw = jnp.exp(tau * lre[None])
        pw_re, pw_im = pw * jnp.cos(tau * lim[None]), pw * jnp.sin(tau * lim[None])
        ab_b_re = pw_re[..., None] * bb_re[None] - pw_im[..., None] * bb_im[None]
        ab_b_im = pw_re[..., None] * bb_im[None] + pw_im[..., None] * bb_re[None]
        kk = (jnp.einsum('gip,tgpj->tgij', cr, ab_b_re, precision=hp)
              - jnp.einsum('gip,tgpj->tgij', ci, ab_b_im, precision=hp))
        ca_re = cr[None] * pw_re[:, :, None, :] - ci[None] * pw_im[:, :, None, :]
        ca_im = cr[None] * pw_im[:, :, None, :] + ci[None] * pw_re[:, :, None, :]
        s = np.arange(t_len)
        z_pow = (t_len - 1 - s) if direction == 0 else s
        c_pow = (s + 1) if direction == 0 else (t_len - s)
        ops.append((kk, [ab_b_re[z_pow], ab_b_im[z_pow]],
                    [ca_re[c_pow], -ca_im[c_pow]],
                    [pw_re[t_len], pw_im[t_len]]))
    lg = S5_LANE_GROUPS
    nblk = a_re.shape[1] // lg
    eye = jnp.eye(lg, dtype=BF16)
    n_state = 4 * lg * S5_STATE
    kf, kb = ops[0][0], ops[1][0]
    kd = jnp.concatenate([kb[t_len - 1:0:-1], (kf[0] + kb[0])[None], kf[1:t_len]])
    kd = kd.astype(BF16).reshape(2 * t_len - 1, nblk, lg, S5_GROUP, S5_GROUP).transpose(1, 0, 4, 2, 3)
    d_blk = kd[:, :, None] * eye[:, None, :, None]
    d_cat = d_blk.reshape(nblk, 2 * t_len - 1, LANES, LANES).transpose(0, 2, 1, 3).reshape(nblk, LANES, -1)
    wz4 = jnp.stack(ops[0][1] + ops[1][1]).astype(BF16)
    wz4 = wz4.reshape(4, t_len, nblk, lg, S5_STATE, S5_GROUP).transpose(2, 1, 5, 0, 3, 4)
    wc4 = jnp.stack(ops[0][2] + ops[1][2]).astype(BF16)
    wc4 = wc4.reshape(4, t_len, nblk, lg, S5_GROUP, S5_STATE).transpose(2, 1, 4, 0, 3, 5)
    compact = (nblk, t_len * S5_GROUP, n_state)
    group_of_lane = (np.arange(n_state) // S5_STATE) % lg
    mask = np.broadcast_to((group_of_lane[None, :] == np.arange(lg)[:, None])[:, None, :], (lg, S5_GROUP, n_state))
    a_t = jnp.stack(ops[0][3] + ops[1][3]).reshape(4, nblk, lg * S5_STATE).transpose(1, 0, 2)
    return d_cat, wz4.reshape(compact), wc4.reshape(compact), jnp.asarray(mask, BF16), a_t


def _s5_core_kernel(u_ref, dcat_ref, wzc_ref, wcc_ref, mask_ref, at_ref, y_ref,
                    wi_ref, wz_ref, wct_ref, z_scr, h_scr, *, n_ctx_chunks):
    n_chunks = z_scr.shape[0]
    n_lat = n_chunks - n_ctx_chunks
    w = z_scr.shape[1] // 4
    width = wi_ref.shape[1]

    @pl.when(pl.program_id(1) == 0)
    def _():
        for s in range(S5_CHUNK):
            lag0 = (S5_CHUNK - 1 - s) * LANES
            wi_ref[s * LANES:(s + 1) * LANES, :] = dcat_ref[0, :, lag0:lag0 + width]
            rows = slice(s * S5_GROUP, (s + 1) * S5_GROUP)
            for a in range(S5_LANE_GROUPS):
                dst = slice(s * LANES + a * S5_GROUP, s * LANES + (a + 1) * S5_GROUP)
                wz_ref[dst, :] = wzc_ref[0, rows, :] * mask_ref[a]
                wct_ref[dst, :] = wcc_ref[0, rows, :] * mask_ref[a]

    a = jnp.concatenate([u_ref[0, 0, pl.ds(s, n_chunks, stride=S5_CHUNK), :].astype(BF16)
                         for s in range(S5_CHUNK)], axis=1)
    z_scr[...] = _dot(a, wz_ref[...])
    ar_f, ai_f, ar_b, ai_b = [at_ref[0, k:k + 1, :] for k in range(4)]

    def step(k, carry):
        (fr, fi), (br, bi) = carry
        cf = jnp.where(k < n_ctx_chunks, n_lat + k, k - n_ctx_chunks)
        cb = n_chunks - 1 - k
        rf, rb = pl.ds(cf, 1), pl.ds(cb, 1)
        h_scr[rf, 0:w] = fr
        h_scr[rf, w:2 * w] = fi
        h_scr[rb, 2 * w:3 * w] = br
        h_scr[rb, 3 * w:4 * w] = bi
        fwd = (ar_f * fr - ai_f * fi + z_scr[rf, 0:w], ar_f * fi + ai_f * fr + z_scr[rf, w:2 * w])
        bwd = (ar_b * br - ai_b * bi + z_scr[rb, 2 * w:3 * w], ar_b * bi + ai_b * br + z_scr[rb, 3 * w:4 * w])
        return fwd, bwd

    zero = jnp.zeros((1, w), F32)
    lax.fori_loop(0, n_chunks, step, ((zero, zero), (zero, zero)))
    y = _dot(a, wi_ref[...]) + _dot_nt(h_scr[...].astype(BF16), wct_ref[...])
    for t in range(S5_CHUNK):
        y_ref[0, 0, pl.ds(t, n_chunks, stride=S5_CHUNK), :] = y[:, t * LANES:(t + 1) * LANES]


def _s5_core(u_blocks, ops, n_ctx_chunks):
    nblk, nb, lt, lanes = u_blocks.shape
    n_chunks = lt // S5_CHUNK
    d_cat, wz_c, wc_c, mask, a_t = ops
    width = S5_CHUNK * lanes
    n_state = wz_c.shape[2]
    kern = functools.partial(_s5_core_kernel, n_ctx_chunks=n_ctx_chunks)
    per_block = lambda a: pl.BlockSpec((1,) + a.shape[1:], lambda g, b: (g, 0, 0))
    tok = pl.BlockSpec((1, 1, lt, lanes), lambda g, b: (g, b, 0, 0))
    return pl.pallas_call(
        kern,
        grid=(nblk, nb),
        in_specs=[tok, per_block(d_cat), per_block(wz_c), per_block(wc_c),
                  pl.BlockSpec(mask.shape, lambda g, b: (0, 0, 0)), per_block(a_t)],
        out_specs=tok,
        out_shape=jax.ShapeDtypeStruct(u_blocks.shape, F32),
        scratch_shapes=[pltpu.VMEM((width, width), BF16), pltpu.VMEM((width, n_state), BF16),
                        pltpu.VMEM((width, n_state), BF16),
                        pltpu.VMEM((n_chunks, n_state), F32), pltpu.VMEM((n_chunks, n_state), F32)],
        compiler_params=_cparams("arbitrary", "arbitrary"),
        name="s5_core",
    )(u_blocks, d_cat, wz_c, wc_c, mask, a_t)


def _s5_out_kernel(x_ref, u_ref, y_ref, d_ref, w_ref, gate_ref, o_ref):
    d = x_ref.shape[2]
    u = jnp.concatenate([u_ref[k, 0] for k in range(u_ref.shape[0])], axis=1)
    y = jnp.concatenate([y_ref[k, 0] for k in range(y_ref.shape[0])], axis=1)
    y = u * d_ref[...] + y
    z = _dot(jax.nn.gelu(y).astype(BF16), w_ref[...])
    o_ref[0] = x_ref[0] + gate_ref[0] * (z[:, :d] * jax.nn.sigmoid(z[:, d:]))


def _s5_out(x, u_blocks, y_blocks, d_skip, w_glu, gate, seq):
    nb, _, d = x.shape
    tm = 256
    nblk, _, _, lanes = u_blocks.shape
    w = nblk * lanes
    row = lambda b, t: (b, t, 0)
    blk = pl.BlockSpec((nblk, 1, tm, lanes), lambda b, t: (0, b, t, 0))
    wg = w_glu.astype(BF16)
    return pl.pallas_call(
        _s5_out_kernel,
        grid=(nb, seq // tm),
        in_specs=[pl.BlockSpec((1, tm, d), row), blk, blk,
                  pl.BlockSpec((1, w), lambda b, t: (0, 0)), pl.BlockSpec(wg.shape, lambda b, t: (0, 0)),
                  pl.BlockSpec((1, 1, d), lambda b, t: (b, 0, 0))],
        out_specs=pl.BlockSpec((1, tm, d), row),
        out_shape=jax.ShapeDtypeStruct((nb, seq, d), F32),
        compiler_params=_cparams("parallel", "parallel"),
        name="s5_out",
    )(x, u_blocks, y_blocks, d_skip.reshape(1, w), wg, gate)


R_IDX, R_WT, R_RANK = 0, TOP_K, 2 * TOP_K


def _router_kernel(x_ref, g_ref, sc_ref, sh_ref, w_ref, tri_ref, rec_ref, cnt_ref, base_scr):
    @pl.when((pl.program_id(0) == 0) & (pl.program_id(1) == 0))
    def _():
        base_scr[...] = jnp.zeros_like(base_scr)

    t = _norm_mod(x_ref[0], g_ref[...], sc_ref[0], sh_ref[0]).astype(BF16)
    logits = _dot(t, w_ref[...])
    lane = lax.broadcasted_iota(jnp.int32, logits.shape, 1)
    big = jnp.int32(LANES)
    l1 = jnp.where(lane < N_EXPERTS, logits, -jnp.inf)
    v1 = jnp.max(l1, axis=-1, keepdims=True)
    i1 = jnp.min(jnp.where(l1 == v1, lane, big), axis=-1, keepdims=True)
    l2 = jnp.where(lane == i1, -jnp.inf, l1)
    v2 = jnp.max(l2, axis=-1, keepdims=True)
    i2 = jnp.min(jnp.where(l2 == v2, lane, big), axis=-1, keepdims=True)
    e2 = jnp.exp(v2 - v1)
    w1 = 1.0 / (1.0 + e2)
    w2 = e2 / (1.0 + e2)
    hit = jnp.where((lane == i1) | (lane == i2), 1.0, 0.0)
    before = _dot(tri_ref[...], hit.astype(BF16)) + base_scr[0:1, :]
    r1 = jnp.sum(jnp.where(lane == i1, before, 0.0), axis=-1, keepdims=True)
    r2 = jnp.sum(jnp.where(lane == i2, before, 0.0), axis=-1, keepdims=True)
    base_scr[...] = base_scr[...] + jnp.sum(hit, axis=0, keepdims=True)
    cnt_ref[...] = base_scr[...]
    rec = jnp.zeros(logits.shape, F32)
    for k, val in enumerate((i1.astype(F32), i2.astype(F32), w1, w2, r1, r2)):
        rec = jnp.where(lane == k, val, rec)
    rec_ref[0] = rec


def _router(x, norm_g, scale, shift, w_router):
    nb, seq, d = x.shape
    tm = 256
    wr = jnp.zeros((d, LANES), F32).at[:, :N_EXPERTS].set(w_router).astype(BF16)
    tri = jnp.asarray(np.tril(np.ones((tm, tm), np.float32), -1), BF16)
    row = lambda b, t: (b, t, 0)
    mod = pl.BlockSpec((1, 1, d), lambda b, t: (b, 0, 0))
    g2 = norm_g.reshape(1, d)
    return pl.pallas_call(
        _router_kernel,
        grid=(nb, seq // tm),
        in_specs=[pl.BlockSpec((1, tm, d), row), pl.BlockSpec((1, d), lambda b, t: (0, 0)), mod, mod,
                  pl.BlockSpec(wr.shape, lambda b, t: (0, 0)), pl.BlockSpec(tri.shape, lambda b, t: (0, 0))],
        out_specs=[pl.BlockSpec((1, tm, LANES), row), pl.BlockSpec((8, LANES), lambda b, t: (0, 0))],
        out_shape=[jax.ShapeDtypeStruct((nb, seq, LANES), F32), jax.ShapeDtypeStruct((8, LANES), F32)],
        scratch_shapes=[pltpu.VMEM((8, LANES), F32)],
        compiler_params=_cparams("arbitrary", "arbitrary"),
        name="router",
    )(x, g2, scale, shift, wr, tri)


def _row_copies_wait(src_ref, dst_ref, sem):
    pltpu.make_async_copy(src_ref, dst_ref, sem).wait()


def _dispatch_kernel(dest_ref, fill_ref, x_ref, g_ref, sc_ref, sh_ref, xs_ref, tbuf, zbuf, sem, zsem):
    tm = x_ref.shape[1]
    ztm = zbuf.shape[0]
    step = pl.program_id(0) * pl.num_programs(1) + pl.program_id(1)
    n_steps = pl.num_programs(0) * pl.num_programs(1)
    slot = step % 2

    def drain(s):
        for _ in range(TOP_K):
            _row_copies_wait(tbuf.at[s], xs_ref.at[pl.ds(0, tm)], sem.at[s])

    @pl.when(step == 0)
    def _():
        zbuf[...] = jnp.zeros_like(zbuf)

        def fill(start):
            def body(i, carry):
                @pl.when(fill_ref[i] > 0)
                def _():
                    cp = pltpu.make_async_copy(zbuf, xs_ref.at[pl.ds(pl.multiple_of(i * ztm, ztm), ztm)], zsem)
                    if start:
                        cp.start()
                    else:
                        cp.wait()
                return carry
            lax.fori_loop(0, fill_ref.shape[0], body, 0)

        fill(True)
        fill(False)

    @pl.when(step >= 2)
    def _():
        drain(slot)

    tbuf[slot] = _norm_mod(x_ref[0], g_ref[...], sc_ref[0], sh_ref[0])
    base = step * (tm * TOP_K)

    def body(r, carry):
        for k in range(TOP_K):
            dst = dest_ref[base + r * TOP_K + k]
            pltpu.make_async_copy(tbuf.at[slot, pl.ds(r, 1)], xs_ref.at[pl.ds(dst, 1)],
                                  sem.at[slot]).start(priority=k % 2)
        return carry

    lax.fori_loop(0, tm, body, 0, unroll=8)

    @pl.when(step == n_steps - 1)
    def _():
        drain(slot)

        @pl.when(n_steps > 1)
        def _():
            drain(1 - slot)


def _dispatch(x, norm_g, scale, shift, dest, tile_fill, n_rows, expert_tm):
    nb, seq, d = x.shape
    tm = 256
    row = lambda b, t, dr, fr: (b, t, 0)
    mod = pl.BlockSpec((1, 1, d), lambda b, t, dr, fr: (b, 0, 0))
    g2 = norm_g.reshape(1, d)
    grid_spec = pltpu.PrefetchScalarGridSpec(
        num_scalar_prefetch=2,
        grid=(nb, seq // tm),
        in_specs=[pl.BlockSpec((1, tm, d), row), pl.BlockSpec((1, d), lambda b, t, dr, fr: (0, 0)), mod, mod],
        out_specs=pl.BlockSpec(memory_space=pl.ANY),
        scratch_shapes=[pltpu.VMEM((2, tm, d), F32), pltpu.VMEM((expert_tm, d), F32),
                        pltpu.SemaphoreType.DMA((2,)), pltpu.SemaphoreType.DMA(())],
    )
    return pl.pallas_call(
        _dispatch_kernel,
        grid_spec=grid_spec,
        out_shape=jax.ShapeDtypeStruct((n_rows, d), F32),
        compiler_params=_cparams("arbitrary", "arbitrary"),
        name="dispatch",
    )(dest, tile_fill, x, g2, scale, shift)


def _expert_kernel(te_ref, tr_ref, x_ref, wg_ref, wu_ref, wd_ref, o_ref, xb_scr, acc_scr):
    i = pl.program_id(0)
    f = pl.program_id(1)
    nf = pl.num_programs(1)
    live = tr_ref[i] > 0

    @pl.when(jnp.logical_not(live) & (f == nf - 1))
    def _():
        o_ref[...] = jnp.zeros_like(o_ref)

    @pl.when(live)
    def _():
        @pl.when(f == 0)
        def _():
            xb_scr[...] = x_ref[...].astype(BF16)
            acc_scr[...] = jnp.zeros_like(acc_scr)

        x = xb_scr[...]
        a = _dot(x, wg_ref[0])
        u = _dot(x, wu_ref[0])
        acc_scr[...] += _dot((a * jax.nn.sigmoid(a) * u).astype(BF16), wd_ref[0])

        @pl.when(f == nf - 1)
        def _():
            o_ref[...] = acc_scr[...]


def _experts(xs, tile_expert, tile_rows, w_gate, w_up, w_down, tm, tf):
    n_rows, d = xs.shape
    dff = w_gate.shape[2]
    nf = dff // tf

    def fsel(i, f, te, tr):
        return jnp.where(tr[i] > 0, f, nf - 1)

    grid_spec = pltpu.PrefetchScalarGridSpec(
        num_scalar_prefetch=2,
        grid=(n_rows // tm, nf),
        in_specs=[pl.BlockSpec((tm, d), lambda i, f, te, tr: (i, 0)),
                  pl.BlockSpec((1, d, tf), lambda i, f, te, tr: (te[i], 0, fsel(i, f, te, tr))),
                  pl.BlockSpec((1, d, tf), lambda i, f, te, tr: (te[i], 0, fsel(i, f, te, tr))),
                  pl.BlockSpec((1, tf, d), lambda i, f, te, tr: (te[i], fsel(i, f, te, tr), 0))],
        out_specs=pl.BlockSpec((tm, d), lambda i, f, te, tr: (i, 0)),
        scratch_shapes=[pltpu.VMEM((tm, d), BF16), pltpu.VMEM((tm, d), F32)],
    )
    return pl.pallas_call(
        _expert_kernel,
        grid_spec=grid_spec,
        out_shape=jax.ShapeDtypeStruct((n_rows, d), F32),
        compiler_params=_cparams("arbitrary", "arbitrary"),
        name="experts",
    )(tile_expert, tile_rows, xs, w_gate, w_up, w_down)


def _combine_kernel(dest_ref, x_ref, rec_ref, gate_ref, g_ref, ys_ref, o_ref, ybuf, sem):
    tm = x_ref.shape[1]
    step = pl.program_id(0) * pl.num_programs(1) + pl.program_id(1)
    n_steps = pl.num_programs(0) * pl.num_programs(1)
    slot = step % 2

    def issue(st, sl):
        base = st * (tm * TOP_K)

        def body(r, carry):
            for k in range(TOP_K):
                src = dest_ref[base + r * TOP_K + k]
                pltpu.make_async_copy(ys_ref.at[pl.ds(src, 1)], ybuf.at[sl, k, pl.ds(r, 1)],
                                      sem.at[sl]).start(priority=k % 2)
            return carry

        lax.fori_loop(0, tm, body, 0, unroll=8)

    @pl.when(step == 0)
    def _():
        issue(step, slot)

    @pl.when(step + 1 < n_steps)
    def _():
        issue(step + 1, 1 - slot)

    for k in range(TOP_K):
        _row_copies_wait(ys_ref.at[pl.ds(0, tm)], ybuf.at[slot, k], sem.at[slot])

    rec = rec_ref[0]
    y = rec[:, R_WT:R_WT + 1] * ybuf[slot, 0] + rec[:, R_WT + 1:R_WT + 2] * ybuf[slot, 1]
    x = x_ref[0] + gate_ref[0] * y
    o_ref[0] = _rms(x) * g_ref[...]


def _combine(x, rec, gate, final_g, ys, dest):
    nb, seq, d = x.shape
    tm = 256
    row = lambda b, t, dr: (b, t, 0)
    grid_spec = pltpu.PrefetchScalarGridSpec(
        num_scalar_prefetch=1,
        grid=(nb, seq // tm),
        in_specs=[pl.BlockSpec((1, tm, d), row), pl.BlockSpec((1, tm, LANES), row),
                  pl.BlockSpec((1, 1, d), lambda b, t, dr: (b, 0, 0)), pl.BlockSpec((1, d), lambda b, t, dr: (0, 0)),
                  pl.BlockSpec(memory_space=pl.ANY)],
        out_specs=pl.BlockSpec((1, tm, d), row),
        scratch_shapes=[pltpu.VMEM((2, TOP_K, tm, d), F32), pltpu.SemaphoreType.DMA((2,))],
    )
    return pl.pallas_call(
        _combine_kernel,
        grid_spec=grid_spec,
        out_shape=jax.ShapeDtypeStruct(x.shape, F32),
        compiler_params=_cparams("arbitrary", "arbitrary"),
        name="combine_final_norm",
    )(dest, x, rec, gate, final_g.reshape(1, d), ys)


def _moe(x, norm_g, scale, shift, gate, w_router, w_gate, w_up, w_down, final_g):
    nb, seq, d = x.shape
    n_tok = nb * seq
    tm, tf = 512, w_gate.shape[2] // 2
    rec, cnt = _router(x, norm_g, scale, shift, w_router)
    counts = cnt[0, :N_EXPERTS].astype(jnp.int32)
    padded = ((counts + tm - 1) // tm) * tm
    pend = jnp.cumsum(padded)
    pstart = pend - padded
    idx = rec[:, :, R_IDX:R_IDX + TOP_K].astype(jnp.int32)
    rank = rec[:, :, R_RANK:R_RANK + TOP_K].astype(jnp.int32)
    seg = jnp.sum(jnp.where(idx[..., None] == jnp.arange(N_EXPERTS), pstart, 0), axis=-1)
    dest = (seg + rank).reshape(n_tok * TOP_K)
    n_rows = n_tok * TOP_K + N_EXPERTS * tm
    tile_start = jnp.arange(n_rows // tm, dtype=jnp.int32) * tm
    tile_expert = jnp.sum((tile_start[:, None] >= pend[None, :]).astype(jnp.int32), axis=1)
    live = tile_expert < N_EXPERTS
    last_live = jnp.max(jnp.where(live, tile_expert, 0))
    tile_expert = jnp.where(live, tile_expert, last_live)
    seg_rows = jnp.sum(jnp.where(tile_expert[:, None] == jnp.arange(N_EXPERTS), (pstart + counts)[None, :], 0), axis=1)
    tile_rows = jnp.where(live, jnp.clip(seg_rows - tile_start, 0, tm), 0).astype(jnp.int32)
    tile_fill = (tile_rows < tm).astype(jnp.int32)

    xs = _dispatch(x, norm_g, scale, shift, dest, tile_fill, n_rows, tm)
    ys = _experts(xs, tile_expert.astype(jnp.int32), tile_rows,
                  w_gate.astype(BF16), w_up.astype(BF16), w_down.astype(BF16), tm, tf)
    return _combine(x, rec, gate, final_g, ys, dest)


def kernel(x, c, ctx, c_ctx, mod_w, mod_b, norm1_g, norm2_g, ev_w_in, ev_q_norm_g, ev_w_qb, ev_kv_norm_g, ev_w_kvb, ev_na_rpb, ev_w_out, ev_ffn_w_gate, ev_ffn_w_up, ev_ffn_w_down, od_w_in, od_a_re, od_a_im, od_log_step, od_b_re, od_b_im, od_c_re, od_c_im, od_d, od_w_glu, moe_w_router, moe_w_gate, moe_w_up, moe_w_down, final_g):
    nb, seq, d = x.shape
    ctx_len = ctx.shape[1]
    assert mod_w.shape[0] == 2 and nb < 8
    assert seq % (NA_ROWS_PER_BLOCK * GRID_W) == 0 and ctx_len == NA_ROWS_PER_BLOCK * GRID_W

    cond = jnp.zeros((8, d), F32).at[:nb].set(c).at[nb].set(c_ctx)
    mods = _adaln(cond, mod_w, mod_b)

    def mod_parts(layer):
        return [mods[layer, :, i * d:(i + 1) * d].reshape(8, 1, d) for i in range(N_MOD)]

    sh1, sc1, g1, sh2, sc2, g2 = mod_parts(0)
    cos, sin = _rope_tables(seq, ctx_len)
    weights = _even_weights(ev_w_in[0], ev_w_qb[0], ev_w_kvb[0])
    q, k, v, nq, nk, nv = _even_project(x, ctx, norm1_g[0], sc1, sh1, weights, ev_q_norm_g[0], ev_kv_norm_g[0],
                                        cos, sin)
    mla = _mla_attention(q, k, v, seq, ctx_len)
    bias = _na_bias_tables(ev_na_rpb[0], seq // GRID_W)
    na = _na_attention(nq, nk, nv, bias, seq, ctx_len)
    next_sh1, next_sc1 = mod_parts(1)[:2]
    xu, u = _even_tail(x, ctx, mla, na, ev_w_out[0], g1, norm2_g[0], sc2, sh2, g2,
                       ev_ffn_w_gate[0], ev_ffn_w_up[0], ev_ffn_w_down[0],
                       norm1_g[1], next_sc1, next_sh1, od_w_in[0])

    sh1, sc1, g1, sh2, sc2, g2 = mod_parts(1)
    ops = _s5_operators(od_a_re[0], od_a_im[0], od_log_step[0], od_b_re[0], od_b_im[0], od_c_re[0], od_c_im[0])
    y = _s5_core(u, ops, ctx_len // S5_CHUNK)
    xl = _s5_out(xu, u, y, od_d[0], od_w_glu[0], g1[:nb], seq)
    return _moe(xl, norm2_g[1], sc2[:nb], sh2[:nb], g2[:nb], moe_w_router[0], moe_w_gate[0], moe_w_up[0],
                moe_w_down[0], final_g)
```

```python
import functools
import math

import numpy as np
import jax
import jax.numpy as jnp
from jax import lax
from jax.experimental import pallas as pl
from jax.experimental.pallas import tpu as pltpu

F32 = jnp.float32
BF16 = jnp.bfloat16

LANES = 128
VMEM_LIMIT_BYTES = 52 * 1024 * 1024

ROW_TILE = 256
EXPERT_ROW_TILE = 512

NORM_EPS = 1e-6
ROPE_BASE = 10000.0
GRID_W = 64
N_MOD = 6

MLA_HEADS = 8
MLA_NOPE = 64
MLA_ROPE = 32
MLA_V = 64
Q_LORA = 384
KV_LORA = 256
MLA_SCALE = (MLA_NOPE + MLA_ROPE) ** -0.5
MLA_EXP2_SCALE = MLA_SCALE * math.log2(math.e)

NA_HEADS = 8
NA_HEAD_DIM = 64
NA_WIN_H = 8
NA_WIN_W = 16
NA_SCALE = NA_HEAD_DIM ** -0.5
NA_EXP2_SCALE = NA_SCALE * math.log2(math.e)
NA_ROWS_PER_BLOCK = 4
NA_KEY_ROWS = NA_ROWS_PER_BLOCK + NA_WIN_H - 1

S5_GROUP = 16
S5_STATE = 64
S5_CHUNK = 16
S5_LANE_GROUPS = LANES // S5_GROUP

N_EXPERTS = 8
TOP_K = 2

NEG_BIG = -1e30


def _cparams(*sem):
    return pltpu.CompilerParams(dimension_semantics=sem, vmem_limit_bytes=VMEM_LIMIT_BYTES)


def _rms(x):
    return x * lax.rsqrt(jnp.mean(x * x, axis=-1, keepdims=True) + NORM_EPS)


def _norm_mod(x, g, scale, shift):
    return (_rms(x) * g) * (1 + scale) + shift


def _dot(a, b):
    return jnp.dot(a, b, preferred_element_type=F32)


def _dot_nt(a, b):
    return lax.dot_general(a, b, (((1,), (1,)), ((), ())), preferred_element_type=F32)


def _adaln_kernel(c_ref, w_ref, b_ref, o_ref):
    c = c_ref[...]
    s = (c * jax.nn.sigmoid(c)).astype(BF16)
    o_ref[0] = _dot(s, w_ref[0].astype(BF16)) + b_ref[0]


def _adaln(cond, mod_w, mod_b):
    nl, d, n = mod_w.shape
    tn = 1536
    return pl.pallas_call(
        _adaln_kernel,
        grid=(nl, n // tn),
        in_specs=[pl.BlockSpec((8, d), lambda l, j: (0, 0)),
                  pl.BlockSpec((1, d, tn), lambda l, j: (l, 0, j)),
                  pl.BlockSpec((1, 1, tn), lambda l, j: (l, 0, j))],
        out_specs=pl.BlockSpec((1, 8, tn), lambda l, j: (l, 0, j)),
        out_shape=jax.ShapeDtypeStruct((nl, 8, n), F32),
        compiler_params=_cparams("parallel", "parallel"),
        name="adaln",
    )(cond, mod_w, mod_b.reshape(nl, 1, n))


def _tile_rows(x_ref, c_ref):
    return jnp.where(pl.program_id(1) < pl.num_programs(1) - 1, x_ref[0], c_ref[0])


def _tile_specs(tm, d, n_lat):
    return [pl.BlockSpec((1, tm, d), lambda b, t: (b, jnp.minimum(t, n_lat - 1), 0)),
            pl.BlockSpec((1, tm, d), lambda b, t: (b, 0, 0))]


def _even_proj_kernel(x_ref, c_ref, g_ref, sc_ref, sh_ref, win_ref, qg_ref, kvg_ref, wq_ref, wqr_ref, wkk_ref,
                      wkv_ref, cos_ref, sin_ref, q_ref, k_ref, v_ref, nq_ref, nk_ref, nv_ref):
    h = _norm_mod(_tile_rows(x_ref, c_ref), g_ref[...], sc_ref[0], sh_ref[0]).astype(BF16)
    p = _dot(h, win_ref[...])
    c0, c1, c2, c3 = Q_LORA, Q_LORA + KV_LORA, Q_LORA + KV_LORA + LANES, Q_LORA + KV_LORA + 2 * LANES
    cqn = (_rms(p[:, :c0]) * qg_ref[...]).astype(BF16)
    ckvn = (_rms(p[:, c0:c1]) * kvg_ref[...]).astype(BF16)
    cos = cos_ref[...]
    sin = sin_ref[...]
    kr = p[:, c1:c2] * cos + p[:, c2:c3] * sin
    qa = _dot(cqn, wq_ref[...])
    qb = _dot(cqn, wqr_ref[...])
    kk = _dot(ckvn, wkk_ref[...])
    for hd in range(MLA_HEADS):
        sl = slice(hd * LANES, (hd + 1) * LANES)
        q_ref[0, :, sl] = ((qa[:, sl] * cos + qb[:, sl] * sin) * MLA_EXP2_SCALE).astype(BF16)
        k_ref[0, :, sl] = (kk[:, sl] + kr).astype(BF16)
    vlane = lax.broadcasted_iota(jnp.int32, (1, MLA_HEADS * LANES), 1) & (LANES - 1)
    v_ref[0] = (_dot(ckvn, wkv_ref[...]) + jnp.where(vlane == MLA_V, 1.0, 0.0)).astype(BF16)
    w = NA_HEADS * NA_HEAD_DIM
    nq_ref[0] = (p[:, c3:c3 + w] * NA_EXP2_SCALE).astype(BF16)
    nk_ref[0] = p[:, c3 + w:c3 + 2 * w].astype(BF16)
    nv_ref[0] = (p[:, c3 + 2 * w:c3 + 2 * w + NA_HEADS * LANES]
                 + jnp.where(vlane == NA_HEAD_DIM, 1.0, 0.0)).astype(BF16)


def _rot_half_cols(w):
    q = MLA_ROPE // 4
    return jnp.concatenate([-w[:, q:2 * q], w[:, :q], -w[:, 3 * q:], w[:, 2 * q:3 * q]], axis=1)


def _rope_tables(seq, ctx_len):
    q = MLA_ROPE // 4
    t = np.arange(seq)
    inv_freq = np.float32(ROPE_BASE) ** (-np.arange(q, dtype=np.float32) / np.float32(q))
    ang_r = (t // GRID_W).astype(np.float32)[:, None] * inv_freq[None, :]
    ang_c = (t % GRID_W).astype(np.float32)[:, None] * inv_freq[None, :]
    cos = np.ones((seq + ctx_len, LANES), np.float32)
    sin = np.zeros((seq + ctx_len, LANES), np.float32)
    cos[:seq, MLA_NOPE:MLA_NOPE + MLA_ROPE] = np.concatenate([np.cos(ang_r)] * 2 + [np.cos(ang_c)] * 2, axis=1)
    sin[:seq, MLA_NOPE:MLA_NOPE + MLA_ROPE] = np.concatenate([np.sin(ang_r)] * 2 + [np.sin(ang_c)] * 2, axis=1)
    return jnp.asarray(cos), jnp.asarray(sin)


def _even_weights(w_in, w_qb, w_kvb):
    d = w_in.shape[0]
    c1 = Q_LORA + KV_LORA
    wkr = w_in[:, c1:c1 + MLA_ROPE]
    pad = lambda w: jnp.zeros((d, LANES), F32).at[:, MLA_NOPE:MLA_NOPE + MLA_ROPE].set(w)
    w_na = NA_HEADS * NA_HEAD_DIM
    na_qk = w_in[:, c1 + MLA_ROPE:c1 + MLA_ROPE + 2 * w_na]
    na_v = w_in[:, c1 + MLA_ROPE + 2 * w_na:].reshape(d, NA_HEADS, NA_HEAD_DIM)
    na_v = jnp.concatenate([na_v, jnp.zeros((d, NA_HEADS, LANES - NA_HEAD_DIM), F32)], axis=2)
    win = jnp.concatenate([w_in[:, :c1], pad(wkr), pad(_rot_half_cols(wkr)), na_qk,
                           na_v.reshape(d, NA_HEADS * LANES)], axis=1)
    hq = MLA_NOPE + MLA_ROPE
    wq = w_qb.reshape(Q_LORA, MLA_HEADS, hq)
    zq = jnp.zeros((Q_LORA, MLA_HEADS, LANES - hq), F32)
    wq_main = jnp.concatenate([wq, zq], axis=2).reshape(Q_LORA, MLA_HEADS * LANES)
    rot = jnp.stack([_rot_half_cols(wq[:, h, MLA_NOPE:]) for h in range(MLA_HEADS)], axis=1)
    wq_rot = jnp.concatenate([jnp.zeros((Q_LORA, MLA_HEADS, MLA_NOPE), F32), rot, zq], axis=2)
    wq_rot = wq_rot.reshape(Q_LORA, MLA_HEADS * LANES)
    wkv = w_kvb.reshape(KV_LORA, MLA_HEADS, MLA_NOPE + MLA_V)
    wkk = jnp.concatenate([wkv[:, :, :MLA_NOPE], jnp.zeros((KV_LORA, MLA_HEADS, LANES - MLA_NOPE), F32)], axis=2)
    wkk = wkk.reshape(KV_LORA, MLA_HEADS * LANES)
    wv = jnp.concatenate([wkv[:, :, MLA_NOPE:], jnp.zeros((KV_LORA, MLA_HEADS, LANES - MLA_V), F32)], axis=2)
    wv = wv.reshape(KV_LORA, MLA_HEADS * LANES)
    return tuple(a.astype(BF16) for a in (win, wq_main, wq_rot, wkk, wv))


def _mod_spec(n_lat_tiles, nb, d):
    return pl.BlockSpec((1, 1, d), lambda b, t: (jnp.where(t < n_lat_tiles, b, nb), 0, 0))


def _even_project(x, ctx, norm_g, scale, shift, weights, q_norm_g, kv_norm_g, cos, sin):
    nb, seq, d = x.shape
    tm = ctx.shape[1]
    lt = seq + tm
    win, wq, wqr, wkk, wv = weights
    n_lat = seq // tm
    row = lambda b, t: (b, t, 0)
    full = lambda a: pl.BlockSpec(a.shape, lambda b, t: (0,) * a.ndim)
    mod = _mod_spec(n_lat, nb, d)
    g2 = norm_g.reshape(1, d)
    qg2 = q_norm_g.reshape(1, Q_LORA)
    kvg2 = kv_norm_g.reshape(1, KV_LORA)
    wide = MLA_HEADS * LANES
    half = MLA_HEADS * MLA_V
    outs = [jax.ShapeDtypeStruct((nb, lt, wide), BF16), jax.ShapeDtypeStruct((nb, lt, wide), BF16)] + \
           [jax.ShapeDtypeStruct((nb, lt, wide), BF16)] + [jax.ShapeDtypeStruct((nb, lt, half), BF16)] * 2 + \
           [jax.ShapeDtypeStruct((nb, lt, wide), BF16)]
    return pl.pallas_call(
        _even_proj_kernel,
        grid=(nb, lt // tm),
        in_specs=_tile_specs(tm, d, n_lat) + [full(g2), mod, mod, full(win), full(qg2), full(kvg2),
                  full(wq), full(wqr), full(wkk), full(wv),
                  pl.BlockSpec((tm, LANES), lambda b, t: (t, 0)), pl.BlockSpec((tm, LANES), lambda b, t: (t, 0))],
        out_specs=[pl.BlockSpec((1, tm, wide), row), pl.BlockSpec((1, tm, wide), row)] +
                  [pl.BlockSpec((1, tm, wide), row)] + [pl.BlockSpec((1, tm, half), row)] * 2 +
                  [pl.BlockSpec((1, tm, wide), row)],
        out_shape=outs,
        compiler_params=_cparams("parallel", "parallel"),
        name="even_project",
    )(x, ctx, g2, scale, shift, win, qg2, kvg2, wq, wqr, wkk, wv, cos, sin)


MLA_KEY_CHUNK = 512


def _mla_kernel(q_ref, k_ref, v_ref, o_ref, s_a, s_b, p_a, p_b, *, seq, ctx_len):
    tq = q_ref.shape[1]
    n_lat_tiles = seq // tq
    t = pl.program_id(1)
    lane = lax.broadcasted_iota(jnp.int32, (tq, LANES), 1)
    stages = ((s_a, p_a), (s_b, p_b))
    head_lanes = lambda h: slice(h * LANES, (h + 1) * LANES)

    def run(key0, n_keys):
        def scores(h):
            s_scr = stages[h % 2][0]
            q = q_ref[0, :, head_lanes(h)]
            m = None
            for c0 in range(0, n_keys, MLA_KEY_CHUNK):
                w = min(MLA_KEY_CHUNK, n_keys - c0)
                s = _dot_nt(q, k_ref[0, key0 + c0:key0 + c0 + w, head_lanes(h)])
                s_scr[:, c0:c0 + w] = s
                for l0 in range(0, w, LANES):
                    m = s[:, l0:l0 + LANES] if m is None else jnp.maximum(m, s[:, l0:l0 + LANES])
            return jnp.max(m, axis=-1, keepdims=True)

        def attend(h, m):
            s_scr, p_scr = stages[h % 2]
            p_scr[:, 0:n_keys] = jnp.exp2((s_scr[:, 0:n_keys] - m).astype(BF16))
            acc = _dot(p_scr[:, 0:n_keys], v_ref[0, key0:key0 + n_keys, head_lanes(h)])
            return acc / acc[:, MLA_V:MLA_V + 1]

        m_next = scores(0)
        for h in range(MLA_HEADS):
            m_cur = m_next
            if h + 1 < MLA_HEADS:
                m_next = scores(h + 1)
            out = attend(h, m_cur)
            if h % 2 == 0:
                first = out
            else:
                pair = h // 2
                second = pltpu.roll(out, MLA_V, 1)
                o_ref[0, :, pair * LANES:(pair + 1) * LANES] = jnp.where(lane < MLA_V, first, second).astype(BF16)

    @pl.when(t < n_lat_tiles)
    def _():
        run(0, seq + ctx_len)

    @pl.when(t >= n_lat_tiles)
    def _():
        run(seq, ctx_len)


def _mla_attention(q, k, v, seq, ctx_len):
    nb, lt, wide = q.shape
    half = MLA_HEADS * MLA_V
    tq = ROW_TILE
    kern = functools.partial(_mla_kernel, seq=seq, ctx_len=ctx_len)
    per_batch = pl.BlockSpec((1, lt, wide), lambda b, t: (b, 0, 0), pipeline_mode=pl.Buffered(1))
    return pl.pallas_call(
        kern,
        grid=(nb, lt // tq),
        in_specs=[pl.BlockSpec((1, tq, wide), lambda b, t: (b, t, 0)), per_batch, per_batch],
        out_specs=pl.BlockSpec((1, tq, half), lambda b, t: (b, t, 0)),
        out_shape=jax.ShapeDtypeStruct((nb, lt, half), BF16),
        scratch_shapes=[pltpu.VMEM((tq, lt), F32), pltpu.VMEM((tq, lt), F32),
                        pltpu.VMEM((tq, lt), BF16), pltpu.VMEM((tq, lt), BF16)],
        compiler_params=_cparams("parallel", "parallel"),
        name="mla_attention",
    )(q, k, v)


def _na_bias_tables(rpb, rows):
    kh = min(NA_WIN_H, rows)
    last_r0 = rows - NA_ROWS_PER_BLOCK
    tabs = []
    for r0 in (0, 2 * NA_ROWS_PER_BLOCK, last_r0):
        kstart = int(np.clip(r0 - kh // 2, 0, rows - NA_KEY_ROWS))
        r = r0 + np.arange(NA_ROWS_PER_BLOCK)
        rs = np.clip(r - kh // 2, 0, rows - kh)
        kr = kstart + np.arange(NA_KEY_ROWS)
        row_ok = (kr[None, :] >= rs[:, None]) & (kr[None, :] < rs[:, None] + kh)
        row_off = np.clip(kr[None, :] - r[:, None] + (NA_WIN_H - 1), 0, 2 * NA_WIN_H - 2)
        c = np.arange(GRID_W)
        cs = np.clip(c - NA_WIN_W // 2, 0, GRID_W - NA_WIN_W)
        col_ok = (c[None, :] >= cs[:, None]) & (c[None, :] < cs[:, None] + NA_WIN_W)
        col_off = np.clip(c[None, :] - c[:, None] + (NA_WIN_W - 1), 0, 2 * NA_WIN_W - 2)
        sel_r = jnp.asarray(row_off[..., None] == np.arange(2 * NA_WIN_H - 1), F32)
        sel_c = jnp.asarray(col_off[..., None] == np.arange(2 * NA_WIN_W - 1), F32)
        b = jnp.einsum('aeu,huv,cdv->haced', sel_r, rpb, sel_c, precision=lax.Precision.HIGHEST)
        ok = row_ok[:, None, :, None] & col_ok[None, :, None, :]
        b = jnp.where(jnp.asarray(ok)[None], b * math.log2(math.e), NEG_BIG)
        tabs.append(b.reshape(rpb.shape[0], NA_ROWS_PER_BLOCK * GRID_W, NA_KEY_ROWS * GRID_W))
    return jnp.stack(tabs)


def _na_kernel(q_ref, k_ref, v_ref, bias_ref, o_ref, *, seq, ctx_len):
    tq = q_ref.shape[1]
    rows = seq // GRID_W
    n_lat_tiles = seq // tq
    nkeys = NA_KEY_ROWS * GRID_W
    t = pl.program_id(1)
    lane = lax.broadcasted_iota(jnp.int32, (tq, LANES), 1)
    kstart = jnp.clip(t * NA_ROWS_PER_BLOCK - NA_WIN_H // 2, 0, rows - NA_KEY_ROWS)
    koff = pl.multiple_of(kstart * GRID_W, GRID_W)

    def run(with_window):
        for pair in range(NA_HEADS // 2):
            sl = slice(pair * LANES, (pair + 1) * LANES)
            q2 = q_ref[0, :, sl].astype(F32)
            kc = k_ref[0, seq:seq + ctx_len, sl]
            if with_window:
                kw = k_ref[0, pl.ds(koff, nkeys), sl]
            outs = []
            for sub in range(2):
                hl = slice((2 * pair + sub) * LANES, (2 * pair + sub + 1) * LANES)
                keep = (lane < NA_HEAD_DIM) if sub == 0 else (lane >= NA_HEAD_DIM)
                q = jnp.where(keep, q2, 0.0).astype(BF16)
                s_c = _dot_nt(q, kc)
                m = jnp.max(s_c, axis=-1, keepdims=True)
                if with_window:
                    s_w = _dot_nt(q, kw) + bias_ref[0, 2 * pair + sub]
                    m = jnp.maximum(m, jnp.max(s_w, axis=-1, keepdims=True))
                acc = _dot(jnp.exp2((s_c - m).astype(BF16)), v_ref[0, seq:seq + ctx_len, hl])
                if with_window:
                    acc = acc + _dot(jnp.exp2((s_w - m).astype(BF16)), v_ref[0, pl.ds(koff, nkeys), hl])
                outs.append(acc / acc[:, NA_HEAD_DIM:NA_HEAD_DIM + 1])
            second = pltpu.roll(outs[1], NA_HEAD_DIM, 1)
            o_ref[0, :, sl] = jnp.where(lane < NA_HEAD_DIM, outs[0], second).astype(BF16)

    @pl.when(t < n_lat_tiles)
    def _():
        run(True)

    @pl.when(t >= n_lat_tiles)
    def _():
        run(False)


def _na_attention(q, k, v, bias, seq, ctx_len):
    nb, lt, w = q.shape
    tq = NA_ROWS_PER_BLOCK * GRID_W
    n_lat = seq // tq
    kern = functools.partial(_na_kernel, seq=seq, ctx_len=ctx_len)

    def variant(b, t):
        return (jnp.where(t == 0, 0, jnp.where(t >= n_lat - 1, 2, 1)), 0, 0, 0)

    return pl.pallas_call(
        kern,
        grid=(nb, lt // tq),
        in_specs=[pl.BlockSpec((1, tq, w), lambda b, t: (b, t, 0)),
                  pl.BlockSpec((1, lt, w), lambda b, t: (b, 0, 0), pipeline_mode=pl.Buffered(1)),
                  pl.BlockSpec((1, lt, v.shape[2]), lambda b, t: (b, 0, 0), pipeline_mode=pl.Buffered(1)),
                  pl.BlockSpec((1,) + bias.shape[1:], variant)],
        out_specs=pl.BlockSpec((1, tq, w), lambda b, t: (b, t, 0)),
        out_shape=jax.ShapeDtypeStruct((nb, lt, w), BF16),
        compiler_params=_cparams("parallel", "parallel"),
        name="na_attention",
    )(q, k, v, bias)


def _even_tail_kernel(x_ref, c_ref, a_ref, b_ref, wa_ref, wb_ref, gate1_ref, g2_ref, sc2_ref, sh2_ref, gate2_ref,
                      wg_ref, wu_ref, wd_ref, g3_ref, sc3_ref, sh3_ref, win_ref, o_ref, u_ref):
    x1 = _tile_rows(x_ref, c_ref) + gate1_ref[0] * (_dot(a_ref[0], wa_ref[...]) + _dot(b_ref[0], wb_ref[...]))
    h = _norm_mod(x1, g2_ref[...], sc2_ref[0], sh2_ref[0]).astype(BF16)
    a = _dot(h, wg_ref[...])
    up = _dot(h, wu_ref[...])
    x2 = x1 + gate2_ref[0] * _dot((a * jax.nn.sigmoid(a) * up).astype(BF16), wd_ref[...])
    o_ref[0] = x2
    u = _dot(_norm_mod(x2, g3_ref[...], sc3_ref[0], sh3_ref[0]).astype(BF16), win_ref[...])
    for k in range(u_ref.shape[0]):
        u_ref[k, 0] = u[:, k * LANES:(k + 1) * LANES]


def _even_tail(x, ctx, a, b, w_out, gate1, norm2_g, scale2, shift2, gate2, w_gate, w_up, w_down,
               norm3_g, scale3, shift3, w_s5_in):
    nb, seq, d = x.shape
    tm = ctx.shape[1]
    lt = seq + tm
    ka = a.shape[2]
    weights = [w.astype(BF16) for w in (w_out[:ka], w_out[ka:], w_gate, w_up, w_down, w_s5_in)]
    wa, wb, wg, wu, wd, win = weights
    row = lambda bb, t: (bb, t, 0)
    once = lambda arr: pl.BlockSpec(arr.shape, lambda bb, t: (0,) * arr.ndim, pipeline_mode=pl.Buffered(1))
    mod = _mod_spec(seq // tm, nb, d)
    vec = pl.BlockSpec((1, d), lambda bb, t: (0, 0))
    nblk = win.shape[1] // LANES
    return pl.pallas_call(
        _even_tail_kernel,
        grid=(nb, lt // tm),
        in_specs=_tile_specs(tm, d, seq // tm) + [pl.BlockSpec((1, tm, ka), row), pl.BlockSpec((1, tm, b.shape[2]), row),
                  once(wa), once(wb), mod, vec, mod, mod, mod, once(wg), once(wu), once(wd), vec, mod, mod, once(win)],
        out_specs=[pl.BlockSpec((1, tm, d), row), pl.BlockSpec((nblk, 1, tm, LANES), lambda bb, t: (0, bb, t, 0))],
        out_shape=[jax.ShapeDtypeStruct((nb, lt, d), F32), jax.ShapeDtypeStruct((nblk, nb, lt, LANES), F32)],
        compiler_params=_cparams("parallel", "parallel"),
        name="even_tail",
    )(x, ctx, a, b, wa, wb, gate1, norm2_g.reshape(1, d), scale2, shift2, gate2, wg, wu, wd,
      norm3_g.reshape(1, d), scale3, shift3, win)


def _s5_operators(a_re, a_im, log_step, b_re, b_im, c_re, c_im):
    hp = lax.Precision.HIGHEST
    t_len = S5_CHUNK
    ops = []
    for direction in range(2):
        dt = jnp.exp(log_step[direction])[:, None]
        lre, lim = a_re[direction] * dt, a_im[direction] * dt
        decay = jnp.exp(lre)
        ab_re, ab_im = decay * jnp.cos(lim), decay * jnp.sin(lim)
        den = a_re[direction] ** 2 + a_im[direction] ** 2
        f_re = ((ab_re - 1) * a_re[direction] + ab_im * a_im[direction]) / den
        f_im = (ab_im * a_re[direction] - (ab_re - 1) * a_im[direction]) / den
        bb_re = f_re[..., None] * b_re[direction] - f_im[..., None] * b_im[direction]
        bb_im = f_re[..., None] * b_im[direction] + f_im[..., None] * b_re[direction]
        cr, ci = c_re[direction], c_im[direction]
        tau = jnp.arange(t_len + 1, dtype=F32)[:, None, None]
        pw = jnp.exp(tau * lre[None])
        pw_re, pw_im = pw * jnp.cos(tau * lim[None]), pw * jnp.sin(tau * lim[None])
        ab_b_re = pw_re[..., None] * bb_re[None] - pw_im[..., None] * bb_im[None]
        ab_b_im = pw_re[..., None] * bb_im[None] + pw_im[..., None] * bb_re[None]
        kk = (jnp.einsum('gip,tgpj->tgij', cr, ab_b_re, precision=hp)
              - jnp.einsum('gip,tgpj->tgij', ci, ab_b_im, precision=hp))
        ca_re = cr[None] * pw_re[:, :, None, :] - ci[None] * pw_im[:, :, None, :]
        ca_im = cr[None] * pw_im[:, :, None, :] + ci[None] * pw_re[:, :, None, :]
        s = np.arange(t_len)
        z_pow = (t_len - 1 - s) if direction == 0 else s
        c_pow = (s + 1) if direction == 0 else (t_len - s)
        ops.append((kk, [ab_b_re[z_pow], ab_b_im[z_pow]],
                    [ca_re[c_pow], -ca_im[c_pow]],
                    [pw_re[t_len], pw_im[t_len]]))
    lg = S5_LANE_GROUPS
    nblk = a_re.shape[1] // lg
    eye = jnp.eye(lg, dtype=BF16)
    n_state = 4 * lg * S5_STATE
    kf, kb = ops[0][0], ops[1][0]
    kd = jnp.concatenate([kb[t_len - 1:0:-1], (kf[0] + kb[0])[None], kf[1:t_len]])
    kd = kd.astype(BF16).reshape(2 * t_len - 1, nblk, lg, S5_GROUP, S5_GROUP).transpose(1, 0, 4, 2, 3)
    d_blk = kd[:, :, None] * eye[:, None, :, None]
    d_cat = d_blk.reshape(nblk, 2 * t_len - 1, LANES, LANES).transpose(0, 2, 1, 3).reshape(nblk, LANES, -1)
    wz4 = jnp.stack(ops[0][1] + ops[1][1]).astype(BF16)
    wz4 = wz4.reshape(4, t_len, nblk, lg, S5_STATE, S5_GROUP).transpose(2, 1, 5, 0, 3, 4)
    wc4 = jnp.stack(ops[0][2] + ops[1][2]).astype(BF16)
    wc4 = wc4.reshape(4, t_len, nblk, lg, S5_GROUP, S5_STATE).transpose(2, 1, 4, 0, 3, 5)
    compact = (nblk, t_len * S5_GROUP, n_state)
    group_of_lane = (np.arange(n_state) // S5_STATE) % lg
    mask = np.broadcast_to((group_of_lane[None, :] == np.arange(lg)[:, None])[:, None, :], (lg, S5_GROUP, n_state))
    a_t = jnp.stack(ops[0][3] + ops[1][3]).reshape(4, nblk, lg * S5_STATE).transpose(1, 0, 2)
    return d_cat, wz4.reshape(compact), wc4.reshape(compact), jnp.asarray(mask, BF16), a_t


def _s5_core_kernel(u_ref, dcat_ref, wzc_ref, wcc_ref, mask_ref, at_ref, y_ref,
                    wi_ref, wz_ref, wct_ref, z_scr, h_scr, *, n_ctx_chunks):
    n_chunks = z_scr.shape[0]
    n_lat = n_chunks - n_ctx_chunks
    w = z_scr.shape[1] // 4
    width = wi_ref.shape[1]

    @pl.when(pl.program_id(1) == 0)
    def _():
        for s in range(S5_CHUNK):
            lag0 = (S5_CHUNK - 1 - s) * LANES
            wi_ref[s * LANES:(s + 1) * LANES, :] = dcat_ref[0, :, lag0:lag0 + width]
            rows = slice(s * S5_GROUP, (s + 1) * S5_GROUP)
            for a in range(S5_LANE_GROUPS):
                dst = slice(s * LANES + a * S5_GROUP, s * LANES + (a + 1) * S5_GROUP)
                wz_ref[dst, :] = wzc_ref[0, rows, :] * mask_ref[a]
                wct_ref[dst, :] = wcc_ref[0, rows, :] * mask_ref[a]

    a = jnp.concatenate([u_ref[0, 0, pl.ds(s, n_chunks, stride=S5_CHUNK), :].astype(BF16)
                         for s in range(S5_CHUNK)], axis=1)
    z_scr[...] = _dot(a, wz_ref[...])
    ar_f, ai_f, ar_b, ai_b = [at_ref[0, k:k + 1, :] for k in range(4)]

    def step(k, carry):
        (fr, fi), (br, bi) = carry
        cf = jnp.where(k < n_ctx_chunks, n_lat + k, k - n_ctx_chunks)
        cb = n_chunks - 1 - k
        rf, rb = pl.ds(cf, 1), pl.ds(cb, 1)
        h_scr[rf, 0:w] = fr
        h_scr[rf, w:2 * w] = fi
        h_scr[rb, 2 * w:3 * w] = br
        h_scr[rb, 3 * w:4 * w] = bi
        fwd = (ar_f * fr - ai_f * fi + z_scr[rf, 0:w], ar_f * fi + ai_f * fr + z_scr[rf, w:2 * w])
        bwd = (ar_b * br - ai_b * bi + z_scr[rb, 2 * w:3 * w], ar_b * bi + ai_b * br + z_scr[rb, 3 * w:4 * w])
        return fwd, bwd

    zero = jnp.zeros((1, w), F32)
    lax.fori_loop(0, n_chunks, step, ((zero, zero), (zero, zero)))
    y = _dot(a, wi_ref[...]) + _dot_nt(h_scr[...].astype(BF16), wct_ref[...])
    for t in range(S5_CHUNK):
        y_ref[0, 0, pl.ds(t, n_chunks, stride=S5_CHUNK), :] = y[:, t * LANES:(t + 1) * LANES]


def _s5_core(u_blocks, ops, n_ctx_chunks):
    nblk, nb, lt, lanes = u_blocks.shape
    n_chunks = lt // S5_CHUNK
    d_cat, wz_c, wc_c, mask, a_t = ops
    width = S5_CHUNK * lanes
    n_state = wz_c.shape[2]
    kern = functools.partial(_s5_core_kernel, n_ctx_chunks=n_ctx_chunks)
    per_block = lambda a: pl.BlockSpec((1,) + a.shape[1:], lambda g, b: (g, 0, 0))
    tok = pl.BlockSpec((1, 1, lt, lanes), lambda g, b: (g, b, 0, 0))
    return pl.pallas_call(
        kern,
        grid=(nblk, nb),
        in_specs=[tok, per_block(d_cat), per_block(wz_c), per_block(wc_c),
                  pl.BlockSpec(mask.shape, lambda g, b: (0, 0, 0)), per_block(a_t)],
        out_specs=tok,
        out_shape=jax.ShapeDtypeStruct(u_blocks.shape, F32),
        scratch_shapes=[pltpu.VMEM((width, width), BF16), pltpu.VMEM((width, n_state), BF16),
                        pltpu.VMEM((width, n_state), BF16),
                        pltpu.VMEM((n_chunks, n_state), F32), pltpu.VMEM((n_chunks, n_state), F32)],
        compiler_params=_cparams("arbitrary", "arbitrary"),
        name="s5_core",
    )(u_blocks, d_cat, wz_c, wc_c, mask, a_t)


def _s5_out_kernel(x_ref, u_ref, y_ref, d_ref, w_ref, gate_ref, g2_ref, sc2_ref, sh2_ref, wr_ref, tri_ref,
                   o_ref, rec_ref, cnt_ref, base_scr):
    d = x_ref.shape[2]
    u = jnp.concatenate([u_ref[k, 0] for k in range(u_ref.shape[0])], axis=1)
    y = jnp.concatenate([y_ref[k, 0] for k in range(y_ref.shape[0])], axis=1)
    y = u * d_ref[...] + y
    z = _dot(jax.nn.gelu(y).astype(BF16), w_ref[...])
    x2 = x_ref[0] + gate_ref[0] * (z[:, :d] * jax.nn.sigmoid(z[:, d:]))
    o_ref[0] = x2
    _route_tile(x2, g2_ref, sc2_ref, sh2_ref, wr_ref, tri_ref, rec_ref, cnt_ref, base_scr)


def _s5_out(x, u_blocks, y_blocks, d_skip, w_glu, gate, seq, norm2_g, scale2, shift2, w_router):
    nb, _, d = x.shape
    tm = ROW_TILE
    nblk, _, _, lanes = u_blocks.shape
    w = nblk * lanes
    row = lambda b, t: (b, t, 0)
    blk = pl.BlockSpec((nblk, 1, tm, lanes), lambda b, t: (0, b, t, 0))
    full = lambda arr: pl.BlockSpec(arr.shape, lambda b, t: (0,) * arr.ndim)
    mod = pl.BlockSpec((1, 1, d), lambda b, t: (b, 0, 0))
    wg = w_glu.astype(BF16)
    wr = jnp.zeros((d, LANES), F32).at[:, :N_EXPERTS].set(w_router).astype(BF16)
    tri = jnp.asarray(np.tril(np.ones((tm, tm), np.float32), -1), BF16)
    return pl.pallas_call(
        _s5_out_kernel,
        grid=(nb, seq // tm),
        in_specs=[pl.BlockSpec((1, tm, d), row), blk, blk, pl.BlockSpec((1, w), lambda b, t: (0, 0)), full(wg), mod,
                  pl.BlockSpec((1, d), lambda b, t: (0, 0)), mod, mod, full(wr), full(tri)],
        out_specs=[pl.BlockSpec((1, tm, d), row), pl.BlockSpec((1, tm, LANES), row),
                   pl.BlockSpec((8, LANES), lambda b, t: (0, 0))],
        out_shape=[jax.ShapeDtypeStruct((nb, seq, d), F32), jax.ShapeDtypeStruct((nb, seq, LANES), F32),
                   jax.ShapeDtypeStruct((8, LANES), F32)],
        scratch_shapes=[pltpu.VMEM((8, LANES), F32)],
        compiler_params=_cparams("arbitrary", "arbitrary"),
        name="s5_out_route",
    )(x, u_blocks, y_blocks, d_skip.reshape(1, w), wg, gate, norm2_g.reshape(1, d), scale2, shift2, wr, tri)


R_IDX, R_WT, R_RANK = 0, TOP_K, 2 * TOP_K


def _route_tile(x, g_ref, sc_ref, sh_ref, w_ref, tri_ref, rec_ref, cnt_ref, base_scr):
    @pl.when((pl.program_id(0) == 0) & (pl.program_id(1) == 0))
    def _():
        base_scr[...] = jnp.zeros_like(base_scr)

    t = _norm_mod(x, g_ref[...], sc_ref[0], sh_ref[0]).astype(BF16)
    logits = _dot(t, w_ref[...])
    lane = lax.broadcasted_iota(jnp.int32, logits.shape, 1)
    big = jnp.int32(LANES)
    l1 = jnp.where(lane < N_EXPERTS, logits, -jnp.inf)
    v1 = jnp.max(l1, axis=-1, keepdims=True)
    i1 = jnp.min(jnp.where(l1 == v1, lane, big), axis=-1, keepdims=True)
    l2 = jnp.where(lane == i1, -jnp.inf, l1)
    v2 = jnp.max(l2, axis=-1, keepdims=True)
    i2 = jnp.min(jnp.where(l2 == v2, lane, big), axis=-1, keepdims=True)
    e2 = jnp.exp(v2 - v1)
    w1 = 1.0 / (1.0 + e2)
    w2 = e2 / (1.0 + e2)
    hit = jnp.where((lane == i1) | (lane == i2), 1.0, 0.0)
    before = _dot(tri_ref[...], hit.astype(BF16)) + base_scr[0:1, :]
    r1 = jnp.sum(jnp.where(lane == i1, before, 0.0), axis=-1, keepdims=True)
    r2 = jnp.sum(jnp.where(lane == i2, before, 0.0), axis=-1, keepdims=True)
    base_scr[...] = base_scr[...] + jnp.sum(hit, axis=0, keepdims=True)
    cnt_ref[...] = base_scr[...]
    rec = jnp.zeros(logits.shape, F32)
    for k, val in enumerate((i1.astype(F32), i2.astype(F32), w1, w2, r1, r2)):
        rec = jnp.where(lane == k, val, rec)
    rec_ref[0] = rec


def _row_copies_wait(src_ref, dst_ref, sem):
    pltpu.make_async_copy(src_ref, dst_ref, sem).wait()


def _dispatch_kernel(dest_ref, fill_ref, x_ref, g_ref, sc_ref, sh_ref, xs_ref, tbuf, zbuf, sem, zsem):
    tm = x_ref.shape[1]
    ztm = zbuf.shape[0]
    step = pl.program_id(0) * pl.num_programs(1) + pl.program_id(1)
    n_steps = pl.num_programs(0) * pl.num_programs(1)
    slot = step % 2

    def drain(s):
        for _ in range(TOP_K):
            _row_copies_wait(tbuf.at[s], xs_ref.at[pl.ds(0, tm)], sem.at[s])

    @pl.when(step == 0)
    def _():
        zbuf[...] = jnp.zeros_like(zbuf)

        def fill(start):
            def body(i, carry):
                @pl.when(fill_ref[i] > 0)
                def _():
                    cp = pltpu.make_async_copy(zbuf, xs_ref.at[pl.ds(pl.multiple_of(i * ztm, ztm), ztm)], zsem)
                    if start:
                        cp.start()
                    else:
                        cp.wait()
                return carry
            lax.fori_loop(0, fill_ref.shape[0], body, 0)

        fill(True)
        fill(False)

    @pl.when(step >= 2)
    def _():
        drain(slot)

    tbuf[slot] = _norm_mod(x_ref[0], g_ref[...], sc_ref[0], sh_ref[0])
    base = step * (tm * TOP_K)

    def body(r, carry):
        for k in range(TOP_K):
            dst = dest_ref[base + r * TOP_K + k]
            pltpu.make_async_copy(tbuf.at[slot, pl.ds(r, 1)], xs_ref.at[pl.ds(dst, 1)],
                                  sem.at[slot]).start(priority=k % 2)
        return carry

    lax.fori_loop(0, tm, body, 0, unroll=8)

    @pl.when(step == n_steps - 1)
    def _():
        drain(slot)

        @pl.when(n_steps > 1)
        def _():
            drain(1 - slot)


def _dispatch(x, norm_g, scale, shift, dest, tile_fill, n_rows, expert_tm):
    nb, seq, d = x.shape
    tm = ROW_TILE
    row = lambda b, t, dr, fr: (b, t, 0)
    mod = pl.BlockSpec((1, 1, d), lambda b, t, dr, fr: (b, 0, 0))
    g2 = norm_g.reshape(1, d)
    grid_spec = pltpu.PrefetchScalarGridSpec(
        num_scalar_prefetch=2,
        grid=(nb, seq // tm),
        in_specs=[pl.BlockSpec((1, tm, d), row), pl.BlockSpec((1, d), lambda b, t, dr, fr: (0, 0)), mod, mod],
        out_specs=pl.BlockSpec(memory_space=pl.ANY),
        scratch_shapes=[pltpu.VMEM((2, tm, d), F32), pltpu.VMEM((expert_tm, d), F32),
                        pltpu.SemaphoreType.DMA((2,)), pltpu.SemaphoreType.DMA(())],
    )
    return pl.pallas_call(
        _dispatch_kernel,
        grid_spec=grid_spec,
        out_shape=jax.ShapeDtypeStruct((n_rows, d), F32),
        compiler_params=_cparams("arbitrary", "arbitrary"),
        name="dispatch",
    )(dest, tile_fill, x, g2, scale, shift)


def _expert_kernel(te_ref, tr_ref, x_ref, wg_ref, wu_ref, wd_ref, o_ref, xb_scr, acc_scr):
    i = pl.program_id(0)
    f = pl.program_id(1)
    nf = pl.num_programs(1)
    live = tr_ref[i] > 0

    @pl.when(jnp.logical_not(live) & (f == nf - 1))
    def _():
        o_ref[...] = jnp.zeros_like(o_ref)

    @pl.when(live)
    def _():
        @pl.when(f == 0)
        def _():
            xb_scr[...] = x_ref[...].astype(BF16)
            acc_scr[...] = jnp.zeros_like(acc_scr)

        x = xb_scr[...]
        a = _dot(x, wg_ref[0])
        u = _dot(x, wu_ref[0])
        acc_scr[...] += _dot((a * jax.nn.sigmoid(a) * u).astype(BF16), wd_ref[0])

        @pl.when(f == nf - 1)
        def _():
            o_ref[...] = acc_scr[...]


def _experts(xs, tile_expert, tile_rows, w_gate, w_up, w_down, tm, tf):
    n_rows, d = xs.shape
    dff = w_gate.shape[2]
    nf = dff // tf

    def fsel(i, f, te, tr):
        return jnp.where(tr[i] > 0, f, nf - 1)

    grid_spec = pltpu.PrefetchScalarGridSpec(
        num_scalar_prefetch=2,
        grid=(n_rows // tm, nf),
        in_specs=[pl.BlockSpec((tm, d), lambda i, f, te, tr: (i, 0)),
                  pl.BlockSpec((1, d, tf), lambda i, f, te, tr: (te[i], 0, fsel(i, f, te, tr))),
                  pl.BlockSpec((1, d, tf), lambda i, f, te, tr: (te[i], 0, fsel(i, f, te, tr))),
                  pl.BlockSpec((1, tf, d), lambda i, f, te, tr: (te[i], fsel(i, f, te, tr), 0))],
        out_specs=pl.BlockSpec((tm, d), lambda i, f, te, tr: (i, 0)),
        scratch_shapes=[pltpu.VMEM((tm, d), BF16), pltpu.VMEM((tm, d), F32)],
    )
    return pl.pallas_call(
        _expert_kernel,
        grid_spec=grid_spec,
        out_shape=jax.ShapeDtypeStruct((n_rows, d), F32),
        compiler_params=_cparams("arbitrary", "arbitrary"),
        name="experts",
    )(tile_expert, tile_rows, xs, w_gate, w_up, w_down)


def _combine_kernel(dest_ref, x_ref, rec_ref, gate_ref, g_ref, ys_ref, o_ref, ybuf, sem):
    tm = x_ref.shape[1]
    step = pl.program_id(0) * pl.num_programs(1) + pl.program_id(1)
    n_steps = pl.num_programs(0) * pl.num_programs(1)
    slot = step % 2

    def issue(st, sl):
        base = st * (tm * TOP_K)

        def body(r, carry):
            for k in range(TOP_K):
                src = dest_ref[base + r * TOP_K + k]
                pltpu.make_async_copy(ys_ref.at[pl.ds(src, 1)], ybuf.at[sl, k, pl.ds(r, 1)],
                                      sem.at[sl]).start(priority=k % 2)
            return carry

        lax.fori_loop(0, tm, body, 0, unroll=8)

    @pl.when(step == 0)
    def _():
        issue(step, slot)

    @pl.when(step + 1 < n_steps)
    def _():
        issue(step + 1, 1 - slot)

    for k in range(TOP_K):
        _row_copies_wait(ys_ref.at[pl.ds(0, tm)], ybuf.at[slot, k], sem.at[slot])

    rec = rec_ref[0]
    y = rec[:, R_WT:R_WT + 1] * ybuf[slot, 0] + rec[:, R_WT + 1:R_WT + 2] * ybuf[slot, 1]
    x = x_ref[0] + gate_ref[0] * y
    o_ref[0] = _rms(x) * g_ref[...]


def _combine(x, rec, gate, final_g, ys, dest):
    nb, seq, d = x.shape
    tm = ROW_TILE
    row = lambda b, t, dr: (b, t, 0)
    grid_spec = pltpu.PrefetchScalarGridSpec(
        num_scalar_prefetch=1,
        grid=(nb, seq // tm),
        in_specs=[pl.BlockSpec((1, tm, d), row), pl.BlockSpec((1, tm, LANES), row),
                  pl.BlockSpec((1, 1, d), lambda b, t, dr: (b, 0, 0)), pl.BlockSpec((1, d), lambda b, t, dr: (0, 0)),
                  pl.BlockSpec(memory_space=pl.ANY)],
        out_specs=pl.BlockSpec((1, tm, d), row),
        scratch_shapes=[pltpu.VMEM((2, TOP_K, tm, d), F32), pltpu.SemaphoreType.DMA((2,))],
    )
    return pl.pallas_call(
        _combine_kernel,
        grid_spec=grid_spec,
        out_shape=jax.ShapeDtypeStruct(x.shape, F32),
        compiler_params=_cparams("arbitrary", "arbitrary"),
        name="combine_final_norm",
    )(dest, x, rec, gate, final_g.reshape(1, d), ys)


def _moe(x, rec, cnt, norm_g, scale, shift, gate, w_gate, w_up, w_down, final_g):
    nb, seq, d = x.shape
    n_tok = nb * seq
    tm, tf = EXPERT_ROW_TILE, w_gate.shape[2] // 2
    counts = cnt[0, :N_EXPERTS].astype(jnp.int32)
    padded = ((counts + tm - 1) // tm) * tm
    pend = jnp.cumsum(padded)
    pstart = pend - padded
    idx = rec[:, :, R_IDX:R_IDX + TOP_K].astype(jnp.int32)
    rank = rec[:, :, R_RANK:R_RANK + TOP_K].astype(jnp.int32)
    seg = jnp.sum(jnp.where(idx[..., None] == jnp.arange(N_EXPERTS), pstart, 0), axis=-1)
    dest = (seg + rank).reshape(n_tok * TOP_K)
    n_rows = n_tok * TOP_K + N_EXPERTS * tm
    tile_start = jnp.arange(n_rows // tm, dtype=jnp.int32) * tm
    tile_expert = jnp.sum((tile_start[:, None] >= pend[None, :]).astype(jnp.int32), axis=1)
    live = tile_expert < N_EXPERTS
    last_live = jnp.max(jnp.where(live, tile_expert, 0))
    tile_expert = jnp.where(live, tile_expert, last_live)
    seg_rows = jnp.sum(jnp.where(tile_expert[:, None] == jnp.arange(N_EXPERTS), (pstart + counts)[None, :], 0), axis=1)
    tile_rows = jnp.where(live, jnp.clip(seg_rows - tile_start, 0, tm), 0).astype(jnp.int32)
    tile_fill = (tile_rows < tm).astype(jnp.int32)

    xs = _dispatch(x, norm_g, scale, shift, dest, tile_fill, n_rows, tm)
    ys = _experts(xs, tile_expert.astype(jnp.int32), tile_rows,
                  w_gate.astype(BF16), w_up.astype(BF16), w_down.astype(BF16), tm, tf)
    return _combine(x, rec, gate, final_g, ys, dest)


def kernel(x, c, ctx, c_ctx, mod_w, mod_b, norm1_g, norm2_g, ev_w_in, ev_q_norm_g, ev_w_qb, ev_kv_norm_g, ev_w_kvb, ev_na_rpb, ev_w_out, ev_ffn_w_gate, ev_ffn_w_up, ev_ffn_w_down, od_w_in, od_a_re, od_a_im, od_log_step, od_b_re, od_b_im, od_c_re, od_c_im, od_d, od_w_glu, moe_w_router, moe_w_gate, moe_w_up, moe_w_down, final_g):
    nb, seq, d = x.shape
    ctx_len = ctx.shape[1]
    assert mod_w.shape[0] == 2 and nb < 8
    assert seq % ROW_TILE == 0 and ctx_len == ROW_TILE == NA_ROWS_PER_BLOCK * GRID_W

    cond = jnp.zeros((8, d), F32).at[:nb].set(c).at[nb].set(c_ctx)
    mods = _adaln(cond, mod_w, mod_b)

    def mod_parts(layer):
        return [mods[layer, :, i * d:(i + 1) * d].reshape(8, 1, d) for i in range(N_MOD)]

    sh1, sc1, g1, sh2, sc2, g2 = mod_parts(0)
    cos, sin = _rope_tables(seq, ctx_len)
    weights = _even_weights(ev_w_in[0], ev_w_qb[0], ev_w_kvb[0])
    q, k, v, nq, nk, nv = _even_project(x, ctx, norm1_g[0], sc1, sh1, weights, ev_q_norm_g[0], ev_kv_norm_g[0],
                                        cos, sin)
    mla = _mla_attention(q, k, v, seq, ctx_len)
    bias = _na_bias_tables(ev_na_rpb[0], seq // GRID_W)
    na = _na_attention(nq, nk, nv, bias, seq, ctx_len)
    next_sh1, next_sc1 = mod_parts(1)[:2]
    xu, u = _even_tail(x, ctx, mla, na, ev_w_out[0], g1, norm2_g[0], sc2, sh2, g2,
                       ev_ffn_w_gate[0], ev_ffn_w_up[0], ev_ffn_w_down[0],
                       norm1_g[1], next_sc1, next_sh1, od_w_in[0])

    sh1, sc1, g1, sh2, sc2, g2 = mod_parts(1)
    ops = _s5_operators(od_a_re[0], od_a_im[0], od_log_step[0], od_b_re[0], od_b_im[0], od_c_re[0], od_c_im[0])
    y = _s5_core(u, ops, ctx_len // S5_CHUNK)
    xl, rec, cnt = _s5_out(xu, u, y, od_d[0], od_w_glu[0], g1[:nb], seq,
                           norm2_g[1], sc2[:nb], sh2[:nb], moe_w_router[0])
    return _moe(xl, rec, cnt, norm2_g[1], sc2[:nb], sh2[:nb], g2[:nb], moe_w_gate[0], moe_w_up[0],
                moe_w_down[0], final_g)
```

```python
import functools
import math

import numpy as np
import jax
import jax.numpy as jnp
from jax import lax
from jax.experimental import pallas as pl
from jax.experimental.pallas import tpu as pltpu

F32 = jnp.float32
BF16 = jnp.bfloat16

LANES = 128
VMEM_LIMIT_BYTES = 52 * 1024 * 1024

ROW_TILE = 256
EXPERT_ROW_TILE = 512
MOE_TOKEN_TILE = 512

NORM_EPS = 1e-6
ROPE_BASE = 10000.0
GRID_W = 64
N_MOD = 6

MLA_HEADS = 8
MLA_NOPE = 64
MLA_ROPE = 32
MLA_V = 64
Q_LORA = 384
KV_LORA = 256
MLA_SCALE = (MLA_NOPE + MLA_ROPE) ** -0.5
MLA_EXP2_SCALE = MLA_SCALE * math.log2(math.e)

NA_HEADS = 8
NA_HEAD_DIM = 64
NA_WIN_H = 8
NA_WIN_W = 16
NA_SCALE = NA_HEAD_DIM ** -0.5
NA_EXP2_SCALE = NA_SCALE * math.log2(math.e)
NA_ROWS_PER_BLOCK = 4
NA_KEY_ROWS = NA_ROWS_PER_BLOCK + NA_WIN_H - 1

S5_GROUP = 16
S5_STATE = 64
S5_CHUNK = 16
S5_LANE_GROUPS = LANES // S5_GROUP

N_EXPERTS = 8
TOP_K = 2

NEG_BIG = -1e30


def _cparams(*sem):
    return pltpu.CompilerParams(dimension_semantics=sem, vmem_limit_bytes=VMEM_LIMIT_BYTES)


def _rms(x):
    return x * lax.rsqrt(jnp.mean(x * x, axis=-1, keepdims=True) + NORM_EPS)


def _norm_mod(x, g, scale, shift):
    return (_rms(x) * g) * (1 + scale) + shift


def _dot(a, b):
    return jnp.dot(a, b, preferred_element_type=F32)


def _dot_nt(a, b):
    return lax.dot_general(a, b, (((1,), (1,)), ((), ())), preferred_element_type=F32)


def _adaln_kernel(c_ref, w_ref, b_ref, o_ref):
    c = c_ref[...]
    s = (c * jax.nn.sigmoid(c)).astype(BF16)
    o_ref[0] = _dot(s, w_ref[0].astype(BF16)) + b_ref[0]


def _adaln(cond, mod_w, mod_b):
    nl, d, n = mod_w.shape
    tn = 1536
    return pl.pallas_call(
        _adaln_kernel,
        grid=(nl, n // tn),
        in_specs=[pl.BlockSpec((8, d), lambda l, j: (0, 0)),
                  pl.BlockSpec((1, d, tn), lambda l, j: (l, 0, j)),
                  pl.BlockSpec((1, 1, tn), lambda l, j: (l, 0, j))],
        out_specs=pl.BlockSpec((1, 8, tn), lambda l, j: (l, 0, j)),
        out_shape=jax.ShapeDtypeStruct((nl, 8, n), F32),
        compiler_params=_cparams("parallel", "parallel"),
        name="adaln",
    )(cond, mod_w, mod_b.reshape(nl, 1, n))


def _tile_rows(x_ref, c_ref):
    return jnp.where(pl.program_id(1) < pl.num_programs(1) - 1, x_ref[0], c_ref[0])


def _tile_specs(tm, d, n_lat):
    return [pl.BlockSpec((1, tm, d), lambda b, t: (b, jnp.minimum(t, n_lat - 1), 0)),
            pl.BlockSpec((1, tm, d), lambda b, t: (b, 0, 0))]


def _even_proj_kernel(x_ref, c_ref, g_ref, sc_ref, sh_ref, win_ref, qg_ref, kvg_ref, wq_ref, wqr_ref, wkk_ref,
                      wkv_ref, cos_ref, sin_ref, q_ref, k_ref, v_ref, nq_ref, nk_ref, nv_ref):
    h = _norm_mod(_tile_rows(x_ref, c_ref), g_ref[...], sc_ref[0], sh_ref[0]).astype(BF16)
    p = _dot(h, win_ref[...])
    c0, c1, c2, c3 = Q_LORA, Q_LORA + KV_LORA, Q_LORA + KV_LORA + LANES, Q_LORA + KV_LORA + 2 * LANES
    cqn = (_rms(p[:, :c0]) * qg_ref[...]).astype(BF16)
    ckvn = (_rms(p[:, c0:c1]) * kvg_ref[...]).astype(BF16)
    cos = cos_ref[...]
    sin = sin_ref[...]
    kr = p[:, c1:c2] * cos + p[:, c2:c3] * sin
    qa = _dot(cqn, wq_ref[...])
    qb = _dot(cqn, wqr_ref[...])
    kk = _dot(ckvn, wkk_ref[...])
    for hd in range(MLA_HEADS):
        sl = slice(hd * LANES, (hd + 1) * LANES)
        q_ref[0, :, sl] = ((qa[:, sl] * cos + qb[:, sl] * sin) * MLA_EXP2_SCALE).astype(BF16)
        k_ref[0, :, sl] = (kk[:, sl] + kr).astype(BF16)
    vlane = lax.broadcasted_iota(jnp.int32, (1, MLA_HEADS * LANES), 1) & (LANES - 1)
    v_ref[0] = (_dot(ckvn, wkv_ref[...]) + jnp.where(vlane == MLA_V, 1.0, 0.0)).astype(BF16)
    w = NA_HEADS * NA_HEAD_DIM
    nq_ref[0] = (p[:, c3:c3 + w] * NA_EXP2_SCALE).astype(BF16)
    nk_ref[0] = p[:, c3 + w:c3 + 2 * w].astype(BF16)
    nv_ref[0] = (p[:, c3 + 2 * w:c3 + 2 * w + NA_HEADS * LANES]
                 + jnp.where(vlane == NA_HEAD_DIM, 1.0, 0.0)).astype(BF16)


def _rot_half_cols(w):
    q = MLA_ROPE // 4
    return jnp.concatenate([-w[:, q:2 * q], w[:, :q], -w[:, 3 * q:], w[:, 2 * q:3 * q]], axis=1)


def _rope_tables(seq, ctx_len):
    q = MLA_ROPE // 4
    t = np.arange(seq)
    inv_freq = np.float32(ROPE_BASE) ** (-np.arange(q, dtype=np.float32) / np.float32(q))
    ang_r = (t // GRID_W).astype(np.float32)[:, None] * inv_freq[None, :]
    ang_c = (t % GRID_W).astype(np.float32)[:, None] * inv_freq[None, :]
    cos = np.ones((seq + ctx_len, LANES), np.float32)
    sin = np.zeros((seq + ctx_len, LANES), np.float32)
    cos[:seq, MLA_NOPE:MLA_NOPE + MLA_ROPE] = np.concatenate([np.cos(ang_r)] * 2 + [np.cos(ang_c)] * 2, axis=1)
    sin[:seq, MLA_NOPE:MLA_NOPE + MLA_ROPE] = np.concatenate([np.sin(ang_r)] * 2 + [np.sin(ang_c)] * 2, axis=1)
    return jnp.asarray(cos), jnp.asarray(sin)


def _even_weights(w_in, w_qb, w_kvb):
    d = w_in.shape[0]
    c1 = Q_LORA + KV_LORA
    wkr = w_in[:, c1:c1 + MLA_ROPE]
    pad = lambda w: jnp.zeros((d, LANES), F32).at[:, MLA_NOPE:MLA_NOPE + MLA_ROPE].set(w)
    w_na = NA_HEADS * NA_HEAD_DIM
    na_qk = w_in[:, c1 + MLA_ROPE:c1 + MLA_ROPE + 2 * w_na]
    na_v = w_in[:, c1 + MLA_ROPE + 2 * w_na:].reshape(d, NA_HEADS, NA_HEAD_DIM)
    na_v = jnp.concatenate([na_v, jnp.zeros((d, NA_HEADS, LANES - NA_HEAD_DIM), F32)], axis=2)
    win = jnp.concatenate([w_in[:, :c1], pad(wkr), pad(_rot_half_cols(wkr)), na_qk,
                           na_v.reshape(d, NA_HEADS * LANES)], axis=1)
    hq = MLA_NOPE + MLA_ROPE
    wq = w_qb.reshape(Q_LORA, MLA_HEADS, hq)
    zq = jnp.zeros((Q_LORA, MLA_HEADS, LANES - hq), F32)
    wq_main = jnp.concatenate([wq, zq], axis=2).reshape(Q_LORA, MLA_HEADS * LANES)
    rot = jnp.stack([_rot_half_cols(wq[:, h, MLA_NOPE:]) for h in range(MLA_HEADS)], axis=1)
    wq_rot = jnp.concatenate([jnp.zeros((Q_LORA, MLA_HEADS, MLA_NOPE), F32), rot, zq], axis=2)
    wq_rot = wq_rot.reshape(Q_LORA, MLA_HEADS * LANES)
    wkv = w_kvb.reshape(KV_LORA, MLA_HEADS, MLA_NOPE + MLA_V)
    wkk = jnp.concatenate([wkv[:, :, :MLA_NOPE], jnp.zeros((KV_LORA, MLA_HEADS, LANES - MLA_NOPE), F32)], axis=2)
    wkk = wkk.reshape(KV_LORA, MLA_HEADS * LANES)
    wv = jnp.concatenate([wkv[:, :, MLA_NOPE:], jnp.zeros((KV_LORA, MLA_HEADS, LANES - MLA_V), F32)], axis=2)
    wv = wv.reshape(KV_LORA, MLA_HEADS * LANES)
    return tuple(a.astype(BF16) for a in (win, wq_main, wq_rot, wkk, wv))


def _mod_spec(n_lat_tiles, nb, d):
    return pl.BlockSpec((1, 1, d), lambda b, t: (jnp.where(t < n_lat_tiles, b, nb), 0, 0))


def _even_project(x, ctx, norm_g, scale, shift, weights, q_norm_g, kv_norm_g, cos, sin):
    nb, seq, d = x.shape
    tm = ctx.shape[1]
    lt = seq + tm
    win, wq, wqr, wkk, wv = weights
    n_lat = seq // tm
    row = lambda b, t: (b, t, 0)
    full = lambda a: pl.BlockSpec(a.shape, lambda b, t: (0,) * a.ndim)
    mod = _mod_spec(n_lat, nb, d)
    g2 = norm_g.reshape(1, d)
    qg2 = q_norm_g.reshape(1, Q_LORA)
    kvg2 = kv_norm_g.reshape(1, KV_LORA)
    wide = MLA_HEADS * LANES
    half = MLA_HEADS * MLA_V
    outs = [jax.ShapeDtypeStruct((nb, lt, wide), BF16), jax.ShapeDtypeStruct((nb, lt, wide), BF16)] + \
           [jax.ShapeDtypeStruct((nb, lt, wide), BF16)] + [jax.ShapeDtypeStruct((nb, lt, half), BF16)] * 2 + \
           [jax.ShapeDtypeStruct((nb, lt, wide), BF16)]
    return pl.pallas_call(
        _even_proj_kernel,
        grid=(nb, lt // tm),
        in_specs=_tile_specs(tm, d, n_lat) + [full(g2), mod, mod, full(win), full(qg2), full(kvg2),
                  full(wq), full(wqr), full(wkk), full(wv),
                  pl.BlockSpec((tm, LANES), lambda b, t: (t, 0)), pl.BlockSpec((tm, LANES), lambda b, t: (t, 0))],
        out_specs=[pl.BlockSpec((1, tm, wide), row), pl.BlockSpec((1, tm, wide), row)] +
                  [pl.BlockSpec((1, tm, wide), row)] + [pl.BlockSpec((1, tm, half), row)] * 2 +
                  [pl.BlockSpec((1, tm, wide), row)],
        out_shape=outs,
        compiler_params=_cparams("parallel", "parallel"),
        name="even_project",
    )(x, ctx, g2, scale, shift, win, qg2, kvg2, wq, wqr, wkk, wv, cos, sin)


MLA_KEY_CHUNK = 512


def _mla_kernel(q_ref, k_ref, v_ref, o_ref, s_a, s_b, p_a, p_b, *, seq, ctx_len):
    tq = q_ref.shape[1]
    n_lat_tiles = seq // tq
    t = pl.program_id(1)
    lane = lax.broadcasted_iota(jnp.int32, (tq, LANES), 1)
    stages = ((s_a, p_a), (s_b, p_b))
    head_lanes = lambda h: slice(h * LANES, (h + 1) * LANES)

    def run(key0, n_keys):
        def scores(h):
            s_scr = stages[h % 2][0]
            q = q_ref[0, :, head_lanes(h)]
            m = None
            for c0 in range(0, n_keys, MLA_KEY_CHUNK):
                w = min(MLA_KEY_CHUNK, n_keys - c0)
                s = _dot_nt(q, k_ref[0, key0 + c0:key0 + c0 + w, head_lanes(h)])
                s_scr[:, c0:c0 + w] = s
                for l0 in range(0, w, LANES):
                    m = s[:, l0:l0 + LANES] if m is None else jnp.maximum(m, s[:, l0:l0 + LANES])
            return jnp.max(m, axis=-1, keepdims=True)

        def attend(h, m):
            s_scr, p_scr = stages[h % 2]
            p_scr[:, 0:n_keys] = jnp.exp2((s_scr[:, 0:n_keys] - m).astype(BF16))
            acc = _dot(p_scr[:, 0:n_keys], v_ref[0, key0:key0 + n_keys, head_lanes(h)])
            return acc / acc[:, MLA_V:MLA_V + 1]

        m_next = scores(0)
        for h in range(MLA_HEADS):
            m_cur = m_next
            if h + 1 < MLA_HEADS:
                m_next = scores(h + 1)
            out = attend(h, m_cur)
            if h % 2 == 0:
                first = out
            else:
                pair = h // 2
                second = pltpu.roll(out, MLA_V, 1)
                o_ref[0, :, pair * LANES:(pair + 1) * LANES] = jnp.where(lane < MLA_V, first, second).astype(BF16)

    @pl.when(t < n_lat_tiles)
    def _():
        run(0, seq + ctx_len)

    @pl.when(t >= n_lat_tiles)
    def _():
        run(seq, ctx_len)


def _mla_attention(q, k, v, seq, ctx_len):
    nb, lt, wide = q.shape
    half = MLA_HEADS * MLA_V
    tq = ROW_TILE
    kern = functools.partial(_mla_kernel, seq=seq, ctx_len=ctx_len)
    per_batch = pl.BlockSpec((1, lt, wide), lambda b, t: (b, 0, 0), pipeline_mode=pl.Buffered(1))
    return pl.pallas_call(
        kern,
        grid=(nb, lt // tq),
        in_specs=[pl.BlockSpec((1, tq, wide), lambda b, t: (b, t, 0)), per_batch, per_batch],
        out_specs=pl.BlockSpec((1, tq, half), lambda b, t: (b, t, 0)),
        out_shape=jax.ShapeDtypeStruct((nb, lt, half), BF16),
        scratch_shapes=[pltpu.VMEM((tq, lt), F32), pltpu.VMEM((tq, lt), F32),
                        pltpu.VMEM((tq, lt), BF16), pltpu.VMEM((tq, lt), BF16)],
        compiler_params=_cparams("parallel", "parallel"),
        name="mla_attention",
    )(q, k, v)


def _na_bias_tables(rpb, rows):
    kh = min(NA_WIN_H, rows)
    last_r0 = rows - NA_ROWS_PER_BLOCK
    tabs = []
    for r0 in (0, 2 * NA_ROWS_PER_BLOCK, last_r0):
        kstart = int(np.clip(r0 - kh // 2, 0, rows - NA_KEY_ROWS))
        r = r0 + np.arange(NA_ROWS_PER_BLOCK)
        rs = np.clip(r - kh // 2, 0, rows - kh)
        kr = kstart + np.arange(NA_KEY_ROWS)
        row_ok = (kr[None, :] >= rs[:, None]) & (kr[None, :] < rs[:, None] + kh)
        row_off = np.clip(kr[None, :] - r[:, None] + (NA_WIN_H - 1), 0, 2 * NA_WIN_H - 2)
        c = np.arange(GRID_W)
        cs = np.clip(c - NA_WIN_W // 2, 0, GRID_W - NA_WIN_W)
        col_ok = (c[None, :] >= cs[:, None]) & (c[None, :] < cs[:, None] + NA_WIN_W)
        col_off = np.clip(c[None, :] - c[:, None] + (NA_WIN_W - 1), 0, 2 * NA_WIN_W - 2)
        sel_r = jnp.asarray(row_off[..., None] == np.arange(2 * NA_WIN_H - 1), F32)
        sel_c = jnp.asarray(col_off[..., None] == np.arange(2 * NA_WIN_W - 1), F32)
        b = jnp.einsum('aeu,huv,cdv->haced', sel_r, rpb, sel_c, precision=lax.Precision.HIGHEST)
        ok = row_ok[:, None, :, None] & col_ok[None, :, None, :]
        b = jnp.where(jnp.asarray(ok)[None], b * math.log2(math.e), NEG_BIG)
        tabs.append(b.reshape(rpb.shape[0], NA_ROWS_PER_BLOCK * GRID_W, NA_KEY_ROWS * GRID_W))
    return jnp.stack(tabs)


def _na_kernel(q_ref, k_ref, v_ref, bias_ref, o_ref, *, seq, ctx_len):
    tq = q_ref.shape[1]
    rows = seq // GRID_W
    n_lat_tiles = seq // tq
    nkeys = NA_KEY_ROWS * GRID_W
    t = pl.program_id(1)
    lane = lax.broadcasted_iota(jnp.int32, (tq, LANES), 1)
    kstart = jnp.clip(t * NA_ROWS_PER_BLOCK - NA_WIN_H // 2, 0, rows - NA_KEY_ROWS)
    koff = pl.multiple_of(kstart * GRID_W, GRID_W)

    def run(with_window):
        for pair in range(NA_HEADS // 2):
            sl = slice(pair * LANES, (pair + 1) * LANES)
            q2 = q_ref[0, :, sl].astype(F32)
            kc = k_ref[0, seq:seq + ctx_len, sl]
            if with_window:
                kw = k_ref[0, pl.ds(koff, nkeys), sl]
            outs = []
            for sub in range(2):
                hl = slice((2 * pair + sub) * LANES, (2 * pair + sub + 1) * LANES)
                keep = (lane < NA_HEAD_DIM) if sub == 0 else (lane >= NA_HEAD_DIM)
                q = jnp.where(keep, q2, 0.0).astype(BF16)
                s_c = _dot_nt(q, kc)
                m = jnp.max(s_c, axis=-1, keepdims=True)
                if with_window:
                    s_w = _dot_nt(q, kw) + bias_ref[0, 2 * pair + sub]
                    m = jnp.maximum(m, jnp.max(s_w, axis=-1, keepdims=True))
                acc = _dot(jnp.exp2((s_c - m).astype(BF16)), v_ref[0, seq:seq + ctx_len, hl])
                if with_window:
                    acc = acc + _dot(jnp.exp2((s_w - m).astype(BF16)), v_ref[0, pl.ds(koff, nkeys), hl])
                outs.append(acc / acc[:, NA_HEAD_DIM:NA_HEAD_DIM + 1])
            second = pltpu.roll(outs[1], NA_HEAD_DIM, 1)
            o_ref[0, :, sl] = jnp.where(lane < NA_HEAD_DIM, outs[0], second).astype(BF16)

    @pl.when(t < n_lat_tiles)
    def _():
        run(True)

    @pl.when(t >= n_lat_tiles)
    def _():
        run(False)


def _na_attention(q, k, v, bias, seq, ctx_len):
    nb, lt, w = q.shape
    tq = NA_ROWS_PER_BLOCK * GRID_W
    n_lat = seq // tq
    kern = functools.partial(_na_kernel, seq=seq, ctx_len=ctx_len)

    def variant(b, t):
        return (jnp.where(t == 0, 0, jnp.where(t >= n_lat - 1, 2, 1)), 0, 0, 0)

    return pl.pallas_call(
        kern,
        grid=(nb, lt // tq),
        in_specs=[pl.BlockSpec((1, tq, w), lambda b, t: (b, t, 0)),
                  pl.BlockSpec((1, lt, w), lambda b, t: (b, 0, 0), pipeline_mode=pl.Buffered(1)),
                  pl.BlockSpec((1, lt, v.shape[2]), lambda b, t: (b, 0, 0), pipeline_mode=pl.Buffered(1)),
                  pl.BlockSpec((1,) + bias.shape[1:], variant)],
        out_specs=pl.BlockSpec((1, tq, w), lambda b, t: (b, t, 0)),
        out_shape=jax.ShapeDtypeStruct((nb, lt, w), BF16),
        compiler_params=_cparams("parallel", "parallel"),
        name="na_attention",
    )(q, k, v, bias)


def _even_tail_kernel(x_ref, c_ref, a_ref, b_ref, wa_ref, wb_ref, gate1_ref, g2_ref, sc2_ref, sh2_ref, gate2_ref,
                      wg_ref, wu_ref, wd_ref, g3_ref, sc3_ref, sh3_ref, win_ref, o_ref, u_ref):
    x1 = _tile_rows(x_ref, c_ref) + gate1_ref[0] * (_dot(a_ref[0], wa_ref[...]) + _dot(b_ref[0], wb_ref[...]))
    h = _norm_mod(x1, g2_ref[...], sc2_ref[0], sh2_ref[0]).astype(BF16)
    a = _dot(h, wg_ref[...])
    up = _dot(h, wu_ref[...])
    x2 = x1 + gate2_ref[0] * _dot((a * jax.nn.sigmoid(a) * up).astype(BF16), wd_ref[...])
    o_ref[0] = x2
    u = _dot(_norm_mod(x2, g3_ref[...], sc3_ref[0], sh3_ref[0]).astype(BF16), win_ref[...])
    for k in range(u_ref.shape[0]):
        u_ref[k, 0] = u[:, k * LANES:(k + 1) * LANES]


def _even_tail(x, ctx, a, b, w_out, gate1, norm2_g, scale2, shift2, gate2, w_gate, w_up, w_down,
               norm3_g, scale3, shift3, w_s5_in):
    nb, seq, d = x.shape
    tm = ctx.shape[1]
    lt = seq + tm
    ka = a.shape[2]
    weights = [w.astype(BF16) for w in (w_out[:ka], w_out[ka:], w_gate, w_up, w_down, w_s5_in)]
    wa, wb, wg, wu, wd, win = weights
    row = lambda bb, t: (bb, t, 0)
    once = lambda arr: pl.BlockSpec(arr.shape, lambda bb, t: (0,) * arr.ndim, pipeline_mode=pl.Buffered(1))
    mod = _mod_spec(seq // tm, nb, d)
    vec = pl.BlockSpec((1, d), lambda bb, t: (0, 0))
    nblk = win.shape[1] // LANES
    return pl.pallas_call(
        _even_tail_kernel,
        grid=(nb, lt // tm),
        in_specs=_tile_specs(tm, d, seq // tm) + [pl.BlockSpec((1, tm, ka), row), pl.BlockSpec((1, tm, b.shape[2]), row),
                  once(wa), once(wb), mod, vec, mod, mod, mod, once(wg), once(wu), once(wd), vec, mod, mod, once(win)],
        out_specs=[pl.BlockSpec((1, tm, d), row), pl.BlockSpec((nblk, 1, tm, LANES), lambda bb, t: (0, bb, t, 0))],
        out_shape=[jax.ShapeDtypeStruct((nb, lt, d), F32), jax.ShapeDtypeStruct((nblk, nb, lt, LANES), F32)],
        compiler_params=_cparams("parallel", "parallel"),
        name="even_tail",
    )(x, ctx, a, b, wa, wb, gate1, norm2_g.reshape(1, d), scale2, shift2, gate2, wg, wu, wd,
      norm3_g.reshape(1, d), scale3, shift3, win)


def _s5_operators(a_re, a_im, log_step, b_re, b_im, c_re, c_im):
    hp = lax.Precision.HIGHEST
    t_len = S5_CHUNK
    ops = []
    for direction in range(2):
        dt = jnp.exp(log_step[direction])[:, None]
        lre, lim = a_re[direction] * dt, a_im[direction] * dt
        decay = jnp.exp(lre)
        ab_re, ab_im = decay * jnp.cos(lim), decay * jnp.sin(lim)
        den = a_re[direction] ** 2 + a_im[direction] ** 2
        f_re = ((ab_re - 1) * a_re[direction] + ab_im * a_im[direction]) / den
        f_im = (ab_im * a_re[direction] - (ab_re - 1) * a_im[direction]) / den
        bb_re = f_re[..., None] * b_re[direction] - f_im[..., None] * b_im[direction]
        bb_im = f_re[..., None] * b_im[direction] + f_im[..., None] * b_re[direction]
        cr, ci = c_re[direction], c_im[direction]
        tau = jnp.arange(t_len + 1, dtype=F32)[:, None, None]
        pw = jnp.exp(tau * lre[None])
        pw_re, pw_im = pw * jnp.cos(tau * lim[None]), pw * jnp.sin(tau * lim[None])
        ab_b_re = pw_re[..., None] * bb_re[None] - pw_im[..., None] * bb_im[None]
        ab_b_im = pw_re[..., None] * bb_im[None] + pw_im[..., None] * bb_re[None]
        kk = (jnp.einsum('gip,tgpj->tgij', cr, ab_b_re, precision=hp)
              - jnp.einsum('gip,tgpj->tgij', ci, ab_b_im, precision=hp))
        ca_re = cr[None] * pw_re[:, :, None, :] - ci[None] * pw_im[:, :, None, :]
        ca_im = cr[None] * pw_im[:, :, None, :] + ci[None] * pw_re[:, :, None, :]
        s = np.arange(t_len)
        z_pow = (t_len - 1 - s) if direction == 0 else s
        c_pow = (s + 1) if direction == 0 else (t_len - s)
        ops.append((kk, [ab_b_re[z_pow], ab_b_im[z_pow]],
                    [ca_re[c_pow], -ca_im[c_pow]],
                    [pw_re[t_len], pw_im[t_len]]))
    lg = S5_LANE_GROUPS
    nblk = a_re.shape[1] // lg
    eye = jnp.eye(lg, dtype=BF16)
    n_state = 4 * lg * S5_STATE
    kf, kb = ops[0][0], ops[1][0]
    kd = jnp.concatenate([kb[t_len - 1:0:-1], (kf[0] + kb[0])[None], kf[1:t_len]])
    kd = kd.astype(BF16).reshape(2 * t_len - 1, nblk, lg, S5_GROUP, S5_GROUP).transpose(1, 0, 4, 2, 3)
    d_blk = kd[:, :, None] * eye[:, None, :, None]
    d_cat = d_blk.reshape(nblk, 2 * t_len - 1, LANES, LANES).transpose(0, 2, 1, 3).reshape(nblk, LANES, -1)
    wz4 = jnp.stack(ops[0][1] + ops[1][1]).astype(BF16)
    wz4 = wz4.reshape(4, t_len, nblk, lg, S5_STATE, S5_GROUP).transpose(2, 1, 5, 0, 3, 4)
    wc4 = jnp.stack(ops[0][2] + ops[1][2]).astype(BF16)
    wc4 = wc4.reshape(4, t_len, nblk, lg, S5_GROUP, S5_STATE).transpose(2, 1, 4, 0, 3, 5)
    compact = (nblk, t_len * S5_GROUP, n_state)
    group_of_lane = (np.arange(n_state) // S5_STATE) % lg
    mask = np.broadcast_to((group_of_lane[None, :] == np.arange(lg)[:, None])[:, None, :], (lg, S5_GROUP, n_state))
    a_t = jnp.stack(ops[0][3] + ops[1][3]).reshape(4, nblk, lg * S5_STATE).transpose(1, 0, 2)
    return d_cat, wz4.reshape(compact), wc4.reshape(compact), jnp.asarray(mask, BF16), a_t


def _s5_core_kernel(u_ref, dcat_ref, wzc_ref, wcc_ref, mask_ref, at_ref, y_ref,
                    wi_ref, wz_ref, wct_ref, z_scr, h_scr, *, n_ctx_chunks):
    n_chunks = z_scr.shape[0]
    n_lat = n_chunks - n_ctx_chunks
    w = z_scr.shape[1] // 4
    width = wi_ref.shape[1]

    @pl.when(pl.program_id(1) == 0)
    def _():
        for s in range(S5_CHUNK):
            lag0 = (S5_CHUNK - 1 - s) * LANES
            wi_ref[s * LANES:(s + 1) * LANES, :] = dcat_ref[0, :, lag0:lag0 + width]
            rows = slice(s * S5_GROUP, (s + 1) * S5_GROUP)
            for a in range(S5_LANE_GROUPS):
                dst = slice(s * LANES + a * S5_GROUP, s * LANES + (a + 1) * S5_GROUP)
                wz_ref[dst, :] = wzc_ref[0, rows, :] * mask_ref[a]
                wct_ref[dst, :] = wcc_ref[0, rows, :] * mask_ref[a]

    a = jnp.concatenate([u_ref[0, 0, pl.ds(s, n_chunks, stride=S5_CHUNK), :].astype(BF16)
                         for s in range(S5_CHUNK)], axis=1)
    z_scr[...] = _dot(a, wz_ref[...])
    ar_f, ai_f, ar_b, ai_b = [at_ref[0, k:k + 1, :] for k in range(4)]

    def step(k, carry):
        (fr, fi), (br, bi) = carry
        cf = jnp.where(k < n_ctx_chunks, n_lat + k, k - n_ctx_chunks)
        cb = n_chunks - 1 - k
        rf, rb = pl.ds(cf, 1), pl.ds(cb, 1)
        h_scr[rf, 0:w] = fr
        h_scr[rf, w:2 * w] = fi
        h_scr[rb, 2 * w:3 * w] = br
        h_scr[rb, 3 * w:4 * w] = bi
        fwd = (ar_f * fr - ai_f * fi + z_scr[rf, 0:w], ar_f * fi + ai_f * fr + z_scr[rf, w:2 * w])
        bwd = (ar_b * br - ai_b * bi + z_scr[rb, 2 * w:3 * w], ar_b * bi + ai_b * br + z_scr[rb, 3 * w:4 * w])
        return fwd, bwd

    zero = jnp.zeros((1, w), F32)
    lax.fori_loop(0, n_chunks, step, ((zero, zero), (zero, zero)))
    y = _dot(a, wi_ref[...]) + _dot_nt(h_scr[...].astype(BF16), wct_ref[...])
    for t in range(S5_CHUNK):
        y_ref[0, 0, pl.ds(t, n_chunks, stride=S5_CHUNK), :] = y[:, t * LANES:(t + 1) * LANES]


def _s5_core(u_blocks, ops, n_ctx_chunks):
    nblk, nb, lt, lanes = u_blocks.shape
    n_chunks = lt // S5_CHUNK
    d_cat, wz_c, wc_c, mask, a_t = ops
    width = S5_CHUNK * lanes
    n_state = wz_c.shape[2]
    kern = functools.partial(_s5_core_kernel, n_ctx_chunks=n_ctx_chunks)
    per_block = lambda a: pl.BlockSpec((1,) + a.shape[1:], lambda g, b: (g, 0, 0))
    tok = pl.BlockSpec((1, 1, lt, lanes), lambda g, b: (g, b, 0, 0))
    return pl.pallas_call(
        kern,
        grid=(nblk, nb),
        in_specs=[tok, per_block(d_cat), per_block(wz_c), per_block(wc_c),
                  pl.BlockSpec(mask.shape, lambda g, b: (0, 0, 0)), per_block(a_t)],
        out_specs=tok,
        out_shape=jax.ShapeDtypeStruct(u_blocks.shape, F32),
        scratch_shapes=[pltpu.VMEM((width, width), BF16), pltpu.VMEM((width, n_state), BF16),
                        pltpu.VMEM((width, n_state), BF16),
                        pltpu.VMEM((n_chunks, n_state), F32), pltpu.VMEM((n_chunks, n_state), F32)],
        compiler_params=_cparams("arbitrary", "arbitrary"),
        name="s5_core",
    )(u_blocks, d_cat, wz_c, wc_c, mask, a_t)


def _s5_out_kernel(x_ref, u_ref, y_ref, d_ref, w_ref, gate_ref, g2_ref, sc2_ref, sh2_ref, wr_ref, tri_ref,
                   o_ref, rec_ref, cnt_ref, base_scr):
    d = x_ref.shape[2]
    u = jnp.concatenate([u_ref[k, 0] for k in range(u_ref.shape[0])], axis=1)
    y = jnp.concatenate([y_ref[k, 0] for k in range(y_ref.shape[0])], axis=1)
    y = u * d_ref[...] + y
    z = _dot(jax.nn.gelu(y).astype(BF16), w_ref[...])
    x2 = x_ref[0] + gate_ref[0] * (z[:, :d] * jax.nn.sigmoid(z[:, d:]))
    o_ref[0] = x2
    _route_tile(x2, g2_ref, sc2_ref, sh2_ref, wr_ref, tri_ref, rec_ref, cnt_ref, base_scr)


def _s5_out(x, u_blocks, y_blocks, d_skip, w_glu, gate, seq, norm2_g, scale2, shift2, w_router):
    nb, _, d = x.shape
    tm = ROW_TILE
    nblk, _, _, lanes = u_blocks.shape
    w = nblk * lanes
    row = lambda b, t: (b, t, 0)
    blk = pl.BlockSpec((nblk, 1, tm, lanes), lambda b, t: (0, b, t, 0))
    full = lambda arr: pl.BlockSpec(arr.shape, lambda b, t: (0,) * arr.ndim)
    mod = pl.BlockSpec((1, 1, d), lambda b, t: (b, 0, 0))
    wg = w_glu.astype(BF16)
    wr = jnp.zeros((d, LANES), F32).at[:, :N_EXPERTS].set(w_router).astype(BF16)
    tri = jnp.asarray(np.tril(np.ones((tm, tm), np.float32), -1), BF16)
    return pl.pallas_call(
        _s5_out_kernel,
        grid=(nb, seq // tm),
        in_specs=[pl.BlockSpec((1, tm, d), row), blk, blk, pl.BlockSpec((1, w), lambda b, t: (0, 0)), full(wg), mod,
                  pl.BlockSpec((1, d), lambda b, t: (0, 0)), mod, mod, full(wr), full(tri)],
        out_specs=[pl.BlockSpec((1, tm, d), row), pl.BlockSpec((1, tm, LANES), row),
                   pl.BlockSpec((8, LANES), lambda b, t: (0, 0))],
        out_shape=[jax.ShapeDtypeStruct((nb, seq, d), F32), jax.ShapeDtypeStruct((nb, seq, LANES), F32),
                   jax.ShapeDtypeStruct((8, LANES), F32)],
        scratch_shapes=[pltpu.VMEM((8, LANES), F32)],
        compiler_params=_cparams("arbitrary", "arbitrary"),
        name="s5_out_route",
    )(x, u_blocks, y_blocks, d_skip.reshape(1, w), wg, gate, norm2_g.reshape(1, d), scale2, shift2, wr, tri)


R_IDX, R_WT, R_RANK = 0, TOP_K, 2 * TOP_K


def _route_tile(x, g_ref, sc_ref, sh_ref, w_ref, tri_ref, rec_ref, cnt_ref, base_scr):
    @pl.when((pl.program_id(0) == 0) & (pl.program_id(1) == 0))
    def _():
        base_scr[...] = jnp.zeros_like(base_scr)

    t = _norm_mod(x, g_ref[...], sc_ref[0], sh_ref[0]).astype(BF16)
    logits = _dot(t, w_ref[...])
    lane = lax.broadcasted_iota(jnp.int32, logits.shape, 1)
    big = jnp.int32(LANES)
    l1 = jnp.where(lane < N_EXPERTS, logits, -jnp.inf)
    v1 = jnp.max(l1, axis=-1, keepdims=True)
    i1 = jnp.min(jnp.where(l1 == v1, lane, big), axis=-1, keepdims=True)
    l2 = jnp.where(lane == i1, -jnp.inf, l1)
    v2 = jnp.max(l2, axis=-1, keepdims=True)
    i2 = jnp.min(jnp.where(l2 == v2, lane, big), axis=-1, keepdims=True)
    e2 = jnp.exp(v2 - v1)
    w1 = 1.0 / (1.0 + e2)
    w2 = e2 / (1.0 + e2)
    hit = jnp.where((lane == i1) | (lane == i2), 1.0, 0.0)
    before = _dot(tri_ref[...], hit.astype(BF16)) + base_scr[0:1, :]
    r1 = jnp.sum(jnp.where(lane == i1, before, 0.0), axis=-1, keepdims=True)
    r2 = jnp.sum(jnp.where(lane == i2, before, 0.0), axis=-1, keepdims=True)
    base_scr[...] = base_scr[...] + jnp.sum(hit, axis=0, keepdims=True)
    cnt_ref[...] = base_scr[...]
    rec = jnp.zeros(logits.shape, F32)
    for k, val in enumerate((i1.astype(F32), i2.astype(F32), w1, w2, r1, r2)):
        rec = jnp.where(lane == k, val, rec)
    rec_ref[0] = rec


def _row_copies_wait(src_ref, dst_ref, sem):
    pltpu.make_async_copy(src_ref, dst_ref, sem).wait()


def _dispatch_kernel(dest_ref, fill_ref, x_ref, g_ref, sc_ref, sh_ref, xs_ref, tbuf, zbuf, sem, zsem):
    tm = x_ref.shape[1]
    ztm = zbuf.shape[0]
    step = pl.program_id(0) * pl.num_programs(1) + pl.program_id(1)
    n_steps = pl.num_programs(0) * pl.num_programs(1)
    slot = step % 2

    def drain(s):
        for _ in range(TOP_K):
            _row_copies_wait(tbuf.at[s], xs_ref.at[pl.ds(0, tm)], sem.at[s])

    @pl.when(step == 0)
    def _():
        zbuf[...] = jnp.zeros_like(zbuf)

        def fill(start):
            def body(i, carry):
                @pl.when(fill_ref[i] > 0)
                def _():
                    cp = pltpu.make_async_copy(zbuf, xs_ref.at[pl.ds(pl.multiple_of(i * ztm, ztm), ztm)], zsem)
                    if start:
                        cp.start()
                    else:
                        cp.wait()
                return carry
            lax.fori_loop(0, fill_ref.shape[0], body, 0)

        fill(True)
        fill(False)

    @pl.when(step >= 2)
    def _():
        drain(slot)

    tbuf[slot] = _norm_mod(x_ref[0], g_ref[...], sc_ref[0], sh_ref[0])
    base = step * (tm * TOP_K)

    def body(r, carry):
        for k in range(TOP_K):
            dst = dest_ref[base + r * TOP_K + k]
            pltpu.make_async_copy(tbuf.at[slot, pl.ds(r, 1)], xs_ref.at[pl.ds(dst, 1)],
                                  sem.at[slot]).start(priority=k % 2)
        return carry

    lax.fori_loop(0, tm, body, 0, unroll=8)

    @pl.when(step == n_steps - 1)
    def _():
        drain(slot)

        @pl.when(n_steps > 1)
        def _():
            drain(1 - slot)


def _dispatch(x, norm_g, scale, shift, dest, tile_fill, n_rows, expert_tm):
    nb, seq, d = x.shape
    tm = MOE_TOKEN_TILE
    row = lambda b, t, dr, fr: (b, t, 0)
    mod = pl.BlockSpec((1, 1, d), lambda b, t, dr, fr: (b, 0, 0))
    g2 = norm_g.reshape(1, d)
    grid_spec = pltpu.PrefetchScalarGridSpec(
        num_scalar_prefetch=2,
        grid=(nb, seq // tm),
        in_specs=[pl.BlockSpec((1, tm, d), row), pl.BlockSpec((1, d), lambda b, t, dr, fr: (0, 0)), mod, mod],
        out_specs=pl.BlockSpec(memory_space=pl.ANY),
        scratch_shapes=[pltpu.VMEM((2, tm, d), F32), pltpu.VMEM((expert_tm, d), F32),
                        pltpu.SemaphoreType.DMA((2,)), pltpu.SemaphoreType.DMA(())],
    )
    return pl.pallas_call(
        _dispatch_kernel,
        grid_spec=grid_spec,
        out_shape=jax.ShapeDtypeStruct((n_rows, d), F32),
        compiler_params=_cparams("arbitrary", "arbitrary"),
        name="dispatch",
    )(dest, tile_fill, x, g2, scale, shift)


def _expert_kernel(te_ref, tr_ref, x_ref, wg_ref, wu_ref, wd_ref, o_ref, xb_scr, acc_scr):
    i = pl.program_id(0)
    f = pl.program_id(1)
    nf = pl.num_programs(1)
    rows = tr_ref[i]
    live = rows > 0
    half = x_ref.shape[0] // 2

    @pl.when(jnp.logical_not(live) & (f == nf - 1))
    def _():
        o_ref[...] = jnp.zeros_like(o_ref)

    @pl.when(live)
    def _():
        @pl.when(f == 0)
        def _():
            xb_scr[...] = x_ref[...].astype(BF16)
            acc_scr[...] = jnp.zeros_like(acc_scr)

        def swiglu(n_rows):
            x = xb_scr[0:n_rows]
            a = _dot(x, wg_ref[0])
            u = _dot(x, wu_ref[0])
            acc_scr[0:n_rows] += _dot((a * jax.nn.sigmoid(a) * u).astype(BF16), wd_ref[0])

        @pl.when(rows > half)
        def _():
            swiglu(x_ref.shape[0])

        @pl.when(rows <= half)
        def _():
            swiglu(half)

        @pl.when(f == nf - 1)
        def _():
            o_ref[...] = acc_scr[...]


def _experts(xs, tile_expert, tile_rows, w_gate, w_up, w_down, tm, tf):
    n_rows, d = xs.shape
    dff = w_gate.shape[2]
    nf = dff // tf

    def fsel(i, f, te, tr):
        return jnp.where(tr[i] > 0, f, nf - 1)

    grid_spec = pltpu.PrefetchScalarGridSpec(
        num_scalar_prefetch=2,
        grid=(n_rows // tm, nf),
        in_specs=[pl.BlockSpec((tm, d), lambda i, f, te, tr: (i, 0)),
                  pl.BlockSpec((1, d, tf), lambda i, f, te, tr: (te[i], 0, fsel(i, f, te, tr))),
                  pl.BlockSpec((1, d, tf), lambda i, f, te, tr: (te[i], 0, fsel(i, f, te, tr))),
                  pl.BlockSpec((1, tf, d), lambda i, f, te, tr: (te[i], fsel(i, f, te, tr), 0))],
        out_specs=pl.BlockSpec((tm, d), lambda i, f, te, tr: (i, 0)),
        scratch_shapes=[pltpu.VMEM((tm, d), BF16), pltpu.VMEM((tm, d), F32)],
    )
    return pl.pallas_call(
        _expert_kernel,
        grid_spec=grid_spec,
        out_shape=jax.ShapeDtypeStruct((n_rows, d), F32),
        compiler_params=_cparams("arbitrary", "arbitrary"),
        name="experts",
    )(tile_expert, tile_rows, xs, w_gate, w_up, w_down)


def _combine_kernel(dest_ref, x_ref, rec_ref, gate_ref, g_ref, ys_ref, o_ref, ybuf, sem):
    tm = x_ref.shape[1]
    step = pl.program_id(0) * pl.num_programs(1) + pl.program_id(1)
    n_steps = pl.num_programs(0) * pl.num_programs(1)
    slot = step % 2

    def issue(st, sl):
        base = st * (tm * TOP_K)

        def body(r, carry):
            for k in range(TOP_K):
                src = dest_ref[base + r * TOP_K + k]
                pltpu.make_async_copy(ys_ref.at[pl.ds(src, 1)], ybuf.at[sl, k, pl.ds(r, 1)],
                                      sem.at[sl]).start(priority=k % 2)
            return carry

        lax.fori_loop(0, tm, body, 0, unroll=8)

    @pl.when(step == 0)
    def _():
        issue(step, slot)

    @pl.when(step + 1 < n_steps)
    def _():
        issue(step + 1, 1 - slot)

    for k in range(TOP_K):
        _row_copies_wait(ys_ref.at[pl.ds(0, tm)], ybuf.at[slot, k], sem.at[slot])

    rec = rec_ref[0]
    y = rec[:, R_WT:R_WT + 1] * ybuf[slot, 0] + rec[:, R_WT + 1:R_WT + 2] * ybuf[slot, 1]
    x = x_ref[0] + gate_ref[0] * y
    o_ref[0] = _rms(x) * g_ref[...]


def _combine(x, rec, gate, final_g, ys, dest):
    nb, seq, d = x.shape
    tm = MOE_TOKEN_TILE
    row = lambda b, t, dr: (b, t, 0)
    grid_spec = pltpu.PrefetchScalarGridSpec(
        num_scalar_prefetch=1,
        grid=(nb, seq // tm),
        in_specs=[pl.BlockSpec((1, tm, d), row), pl.BlockSpec((1, tm, LANES), row),
                  pl.BlockSpec((1, 1, d), lambda b, t, dr: (b, 0, 0)), pl.BlockSpec((1, d), lambda b, t, dr: (0, 0)),
                  pl.BlockSpec(memory_space=pl.ANY)],
        out_specs=pl.BlockSpec((1, tm, d), row),
        scratch_shapes=[pltpu.VMEM((2, TOP_K, tm, d), F32), pltpu.SemaphoreType.DMA((2,))],
    )
    return pl.pallas_call(
        _combine_kernel,
        grid_spec=grid_spec,
        out_shape=jax.ShapeDtypeStruct(x.shape, F32),
        compiler_params=_cparams("arbitrary", "arbitrary"),
        name="combine_final_norm",
    )(dest, x, rec, gate, final_g.reshape(1, d), ys)


def _moe(x, rec, cnt, norm_g, scale, shift, gate, w_gate, w_up, w_down, final_g):
    nb, seq, d = x.shape
    n_tok = nb * seq
    tm, tf = EXPERT_ROW_TILE, w_gate.shape[2] // 2
    counts = cnt[0, :N_EXPERTS].astype(jnp.int32)
    padded = ((counts + tm - 1) // tm) * tm
    pend = jnp.cumsum(padded)
    pstart = pend - padded
    idx = rec[:, :, R_IDX:R_IDX + TOP_K].astype(jnp.int32)
    rank = rec[:, :, R_RANK:R_RANK + TOP_K].astype(jnp.int32)
    seg = jnp.sum(jnp.where(idx[..., None] == jnp.arange(N_EXPERTS), pstart, 0), axis=-1)
    dest = (seg + rank).reshape(n_tok * TOP_K)
    n_rows = n_tok * TOP_K + N_EXPERTS * tm
    tile_start = jnp.arange(n_rows // tm, dtype=jnp.int32) * tm
    tile_expert = jnp.sum((tile_start[:, None] >= pend[None, :]).astype(jnp.int32), axis=1)
    live = tile_expert < N_EXPERTS
    last_live = jnp.max(jnp.where(live, tile_expert, 0))
    tile_expert = jnp.where(live, tile_expert, last_live)
    seg_rows = jnp.sum(jnp.where(tile_expert[:, None] == jnp.arange(N_EXPERTS), (pstart + counts)[None, :], 0), axis=1)
    tile_rows = jnp.where(live, jnp.clip(seg_rows - tile_start, 0, tm), 0).astype(jnp.int32)
    tile_fill = (tile_rows < tm).astype(jnp.int32)

    xs = _dispatch(x, norm_g, scale, shift, dest, tile_fill, n_rows, tm)
    ys = _experts(xs, tile_expert.astype(jnp.int32), tile_rows,
                  w_gate.astype(BF16), w_up.astype(BF16), w_down.astype(BF16), tm, tf)
    return _combine(x, rec, gate, final_g, ys, dest)


def kernel(x, c, ctx, c_ctx, mod_w, mod_b, norm1_g, norm2_g, ev_w_in, ev_q_norm_g, ev_w_qb, ev_kv_norm_g, ev_w_kvb, ev_na_rpb, ev_w_out, ev_ffn_w_gate, ev_ffn_w_up, ev_ffn_w_down, od_w_in, od_a_re, od_a_im, od_log_step, od_b_re, od_b_im, od_c_re, od_c_im, od_d, od_w_glu, moe_w_router, moe_w_gate, moe_w_up, moe_w_down, final_g):
    nb, seq, d = x.shape
    ctx_len = ctx.shape[1]
    assert mod_w.shape[0] == 2 and nb < 8
    assert seq % MOE_TOKEN_TILE == 0 and ctx_len == ROW_TILE == NA_ROWS_PER_BLOCK * GRID_W

    cond = jnp.zeros((8, d), F32).at[:nb].set(c).at[nb].set(c_ctx)
    mods = _adaln(cond, mod_w, mod_b)

    def mod_parts(layer):
        return [mods[layer, :, i * d:(i + 1) * d].reshape(8, 1, d) for i in range(N_MOD)]

    sh1, sc1, g1, sh2, sc2, g2 = mod_parts(0)
    cos, sin = _rope_tables(seq, ctx_len)
    weights = _even_weights(ev_w_in[0], ev_w_qb[0], ev_w_kvb[0])
    q, k, v, nq, nk, nv = _even_project(x, ctx, norm1_g[0], sc1, sh1, weights, ev_q_norm_g[0], ev_kv_norm_g[0],
                                        cos, sin)
    mla = _mla_attention(q, k, v, seq, ctx_len)
    bias = _na_bias_tables(ev_na_rpb[0], seq // GRID_W)
    na = _na_attention(nq, nk, nv, bias, seq, ctx_len)
    next_sh1, next_sc1 = mod_parts(1)[:2]
    xu, u = _even_tail(x, ctx, mla, na, ev_w_out[0], g1, norm2_g[0], sc2, sh2, g2,
                       ev_ffn_w_gate[0], ev_ffn_w_up[0], ev_ffn_w_down[0],
                       norm1_g[1], next_sc1, next_sh1, od_w_in[0])

    sh1, sc1, g1, sh2, sc2, g2 = mod_parts(1)
    ops = _s5_operators(od_a_re[0], od_a_im[0], od_log_step[0], od_b_re[0], od_b_im[0], od_c_re[0], od_c_im[0])
    y = _s5_core(u, ops, ctx_len // S5_CHUNK)
    xl, rec, cnt = _s5_out(xu, u, y, od_d[0], od_w_glu[0], g1[:nb], seq,
                           norm2_g[1], sc2[:nb], sh2[:nb], moe_w_router[0])
    return _moe(xl, rec, cnt, norm2_g[1], sc2[:nb], sh2[:nb], g2[:nb], moe_w_gate[0], moe_w_up[0],
                moe_w_down[0], final_g)
```
